```python
import math
import jax, jax.numpy as jnp
from jax import lax
import numpy as np

D_MODEL = 1024
BATCH = 4
SEQ = 4096
DEPTH = 4

RET_HEADS = 8
RET_DK = 64
RET_DV = 128
RET_CHUNK = 128
CONV_W = 1024
CONV_K = 3
MLA_HEADS = 8
MLA_NOPE = 128
MLA_ROPE = 64
MLA_DV = 128
Q_LORA = 384
KV_LORA = 256
ATTN_BLOCK = 128
ROPE_BASE = 10000.0
EPS = 1e-6
N_EVEN = (DEPTH + 1) // 2
N_ODD = DEPTH // 2

RET_QK = RET_HEADS * RET_DK
RET_V = RET_HEADS * RET_DV
EVEN_SPLITS = (RET_QK, RET_QK, RET_V, RET_V, CONV_W, CONV_W, CONV_W, CONV_W)
EVEN_IN = sum(EVEN_SPLITS)
EVEN_MIX = RET_V + CONV_W
MLA_QK = MLA_NOPE + MLA_ROPE
MLA_V = MLA_HEADS * MLA_DV
ODD_SPLITS = (Q_LORA, KV_LORA, MLA_ROPE, MLA_V)
ODD_IN = sum(ODD_SPLITS)

kernel_name = "hybrid_retnet_shortconv_mla_trunk"


def rms_norm(x, g):
    xf = x.astype(jnp.float32)
    y = xf * lax.rsqrt(jnp.mean(xf * xf, axis=-1, keepdims=True) + EPS)
    return (y * g.astype(jnp.float32)).astype(x.dtype)


def rope(x, positions):
    d = x.shape[-1]
    inv = ROPE_BASE ** (-jnp.arange(0, d, 2, dtype=jnp.float32) / d)
    ang = positions.astype(jnp.float32)[..., None] * inv
    cos = jnp.cos(ang)[:, :, None, :]
    sin = jnp.sin(ang)[:, :, None, :]
    xf = x.astype(jnp.float32)
    x1, x2 = xf[..., : d // 2], xf[..., d // 2:]
    out = jnp.concatenate([x1 * cos - x2 * sin, x1 * sin + x2 * cos], axis=-1)
    return out.astype(x.dtype)


def split_cols(h, sizes):
    return jnp.split(h, np.cumsum(sizes)[:-1].tolist(), axis=-1)


def retention(q, k, v):
    b, s, h, dk = q.shape
    dv = v.shape[-1]
    c = RET_CHUNK
    n = s // c
    dt = q.dtype
    log_gamma = jnp.log1p(-jnp.exp2(-5.0 - jnp.arange(h, dtype=jnp.float32)))
    i = jnp.arange(c, dtype=jnp.float32)
    rel = i[:, None] - i[None, :]
    intra = jnp.where(rel >= 0, jnp.exp(log_gamma[:, None, None] * jnp.maximum(rel, 0.0)), 0.0).astype(dt)
    xi = jnp.exp(log_gamma[:, None] * (i + 1.0)).astype(dt)
    zeta = jnp.exp(log_gamma[:, None] * (c - 1.0 - i)).astype(dt)
    gamma_c = jnp.exp(log_gamma * c).astype(dt)
    qc = q.reshape(b, n, c, h, dk)
    kc = k.reshape(b, n, c, h, dk)
    vc = v.reshape(b, n, c, h, dv)
    scores = jnp.einsum('bnihd,bnjhd->bnhij', qc, kc) * intra
    inner = jnp.einsum('bnhij,bnjhe->bnihe', scores, vc)
    kv = jnp.einsum('bnjhd,hj,bnjhe->nbhde', kc, zeta, vc)

    def step(state, kv_i):
        return gamma_c[None, :, None, None] * state + kv_i, state

    _, prev = lax.scan(step, jnp.zeros((b, h, dk, dv), dt), kv)
    cross = jnp.einsum('bnihd,nbhde,hi->bnihe', qc, prev, xi)
    return (inner + cross).reshape(b, s, h, dv)


def short_conv(u, w, bias):
    s = u.shape[1]
    up = jnp.pad(u, ((0, 0), (CONV_K - 1, 0), (0, 0)))
    out = up[:, 0:s] * w[0]
    for tap in range(1, CONV_K):
        out = out + up[:, tap:tap + s] * w[tap]
    return out + bias


def even_layer(h, positions, w_in, conv_w, conv_b, gn, w_out):
    b, s, _ = h.shape
    q, k, v, g_ret, cb, cc, cx, g_conv = split_cols(h @ w_in, EVEN_SPLITS)
    q = rope(q.reshape(b, s, RET_HEADS, RET_DK), positions)
    k = rope(k.reshape(b, s, RET_HEADS, RET_DK), positions) * (RET_DK ** -0.5)
    v = v.reshape(b, s, RET_HEADS, RET_DV)
    r = retention(q, k, v)
    r = rms_norm(r, gn.reshape(RET_HEADS, RET_DV)).reshape(b, s, RET_V)
    y_conv = cb * short_conv(cc * cx, conv_w, conv_b)
    mix = jnp.concatenate([r * jax.nn.silu(g_ret), y_conv * jax.nn.silu(g_conv)], axis=-1)
    return mix @ w_out


def causal_mla_attention(q_nope, q_rope, k_nope, k_rope, v):
    s = q_nope.shape[1]
    scale = MLA_QK ** -0.5
    outs = []
    for blk in range(s // ATTN_BLOCK):
        lo, hi = blk * ATTN_BLOCK, (blk + 1) * ATTN_BLOCK
        sc = (jnp.einsum('bqhd,bkhd->bhqk', q_nope[:, lo:hi].astype(jnp.float32), k_nope[:, :hi].astype(jnp.float32))
              + jnp.einsum('bqhd,bkd->bhqk', q_rope[:, lo:hi].astype(jnp.float32), k_rope[:, :hi].astype(jnp.float32))) * scale
        qpos = lo + jnp.arange(ATTN_BLOCK)
        kpos = jnp.arange(hi)
        sc = jnp.where(kpos[None, :] <= qpos[:, None], sc, -jnp.inf)
        p = jax.nn.softmax(sc, axis=-1).astype(v.dtype)
        outs.append(jnp.einsum('bhqk,bkhd->bqhd', p, v[:, :hi]))
    return jnp.concatenate(outs, axis=1)


def odd_layer(h, positions, w_in, q_a_norm, w_qb, kv_a_norm, w_kvb, w_out):
    b, s, _ = h.shape
    cq, ckv, k_rope, gate = split_cols(h @ w_in, ODD_SPLITS)
    q = (rms_norm(cq, q_a_norm) @ w_qb).reshape(b, s, MLA_HEADS, MLA_QK)
    q_nope = q[..., :MLA_NOPE]
    q_rope = rope(q[..., MLA_NOPE:], positions)
    kv = (rms_norm(ckv, kv_a_norm) @ w_kvb).reshape(b, s, MLA_HEADS, MLA_NOPE + MLA_DV)
    k_nope, v = kv[..., :MLA_NOPE], kv[..., MLA_NOPE:]
    k_rope = rope(k_rope[:, :, None, :], positions)[:, :, 0]
    o = causal_mla_attention(q_nope, q_rope, k_nope, k_rope, v)
    return (o.reshape(b, s, MLA_V) * jax.nn.silu(gate)) @ w_out


def setup_inputs(seed: int = 0) -> dict:
    key = jax.random.key(seed)
    ks = jax.random.split(key, 16)
    f32 = jnp.float32
    nrm = lambda k, shape, scale: jax.random.normal(k, shape, f32) * scale
    x = jax.random.normal(ks[0], (BATCH, SEQ, D_MODEL), f32)
    offsets = jax.random.randint(ks[1], (BATCH, 1), 0, 1024, dtype=jnp.int32)
    positions = (offsets + jnp.arange(SEQ, dtype=jnp.int32)[None, :]).astype(jnp.int32)
    pre_norm = 1.0 + nrm(ks[2], (DEPTH, D_MODEL), 0.02)
    post_norm = 1.0 + nrm(ks[3], (DEPTH, D_MODEL), 0.02)
    even_w_in = nrm(ks[4], (N_EVEN, D_MODEL, EVEN_IN), D_MODEL ** -0.5)
    even_conv_w = nrm(ks[5], (N_EVEN, CONV_K, CONV_W), CONV_K ** -0.5)
    even_conv_b = nrm(ks[6], (N_EVEN, CONV_W), 0.01)
    ret_gn = 1.0 + nrm(ks[7], (N_EVEN, RET_V), 0.02)
    even_w_out = nrm(ks[8], (N_EVEN, EVEN_MIX, D_MODEL), EVEN_MIX ** -0.5)
    odd_w_in = nrm(ks[9], (N_ODD, D_MODEL, ODD_IN), D_MODEL ** -0.5)
    q_a_norm = 1.0 + nrm(ks[10], (N_ODD, Q_LORA), 0.02)
    w_qb = nrm(ks[11], (N_ODD, Q_LORA, MLA_HEADS * MLA_QK), Q_LORA ** -0.5)
    kv_a_norm = 1.0 + nrm(ks[12], (N_ODD, KV_LORA), 0.02)
    w_kvb = nrm(ks[13], (N_ODD, KV_LORA, MLA_HEADS * (MLA_NOPE + MLA_DV)), KV_LORA ** -0.5)
    odd_w_out = nrm(ks[14], (N_ODD, MLA_V, D_MODEL), MLA_V ** -0.5)
    return {"x": x, "positions": positions, "pre_norm": pre_norm, "post_norm": post_norm,
            "even_w_in": even_w_in, "even_conv_w": even_conv_w, "even_conv_b": even_conv_b,
            "ret_gn": ret_gn, "even_w_out": even_w_out, "odd_w_in": odd_w_in,
            "q_a_norm": q_a_norm, "w_qb": w_qb, "kv_a_norm": kv_a_norm, "w_kvb": w_kvb,
            "odd_w_out": odd_w_out}


def reference(x, positions, pre_norm, post_norm, even_w_in, even_conv_w, even_conv_b, ret_gn,
              even_w_out, odd_w_in, q_a_norm, w_qb, kv_a_norm, w_kvb, odd_w_out):
    h = x
    for layer in range(DEPTH):
        u = rms_norm(h, pre_norm[layer])
        if layer % 2 == 0:
            e = layer // 2
            y = even_layer(u, positions, even_w_in[e], even_conv_w[e], even_conv_b[e], ret_gn[e], even_w_out[e])
        else:
            o = layer // 2
            y = odd_layer(u, positions, odd_w_in[o], q_a_norm[o], w_qb[o], kv_a_norm[o], w_kvb[o], odd_w_out[o])
        h = h + rms_norm(y, post_norm[layer])
    return h
```

```python
import functools

import jax
import jax.numpy as jnp
import numpy as np
from jax import lax
from jax.experimental import pallas as pl
from jax.experimental.pallas import tpu as pltpu

D_MODEL = 1024
BATCH = 4
SEQ = 4096
DEPTH = 4
ROWS = BATCH * SEQ

RET_HEADS = 8
RET_DK = 64
RET_DV = 128
RET_CHUNK = 128
RET_PAIRS = RET_HEADS // 2
CONV_W = 1024
CONV_K = 3
MLA_HEADS = 8
MLA_NOPE = 128
MLA_ROPE = 64
MLA_DV = 128
Q_LORA = 384
KV_LORA = 256
ROPE_BASE = 10000.0
EPS = 1e-6

RET_QK = RET_HEADS * RET_DK
RET_V = RET_HEADS * RET_DV
EVEN_IN = 2 * RET_QK + 2 * RET_V + 4 * CONV_W
EVEN_MIX = RET_V + CONV_W
MLA_QK = MLA_NOPE + MLA_ROPE
MLA_V = MLA_HEADS * MLA_DV
MLA_SCALE = MLA_QK ** -0.5
MLA_QK_PAD = 256
ODD_A = Q_LORA + KV_LORA + 128

_Q0, _K0, _V0, _GR0, _CB0, _CC0, _CX0, _GC0 = 0, 512, 1024, 2048, 3072, 4096, 5120, 6144

LANES = 128
ROPE_HALF = 32

VMEM_LIMIT = 56 * 1024 * 1024

BF16 = jnp.bfloat16
F32 = jnp.float32


def _silu(x):
    return x * (1.0 / (1.0 + jnp.exp(-x)))


def _rms(x, g):
    ms = jnp.mean(x * x, axis=-1, keepdims=True)
    return x * lax.rsqrt(ms + EPS) * g


def _rope_lanes(x, cos, sin_signed, first_half):
    rot = jnp.where(first_half, pltpu.roll(x, LANES - ROPE_HALF, 1), pltpu.roll(x, ROPE_HALF, 1))
    return x * cos + rot * sin_signed


def _first_half_mask(rows):
    lane = lax.broadcasted_iota(jnp.int32, (rows, LANES), 1)
    return (lane % (2 * ROPE_HALF)) < ROPE_HALF


def _rope_table_kernel(pos_ref, inv_ref, cos_ref, sin_ref, nsin_ref):
    ang = pos_ref[...].astype(F32) * inv_ref[...]
    s = jnp.sin(ang)
    cos_ref[...] = jnp.cos(ang)
    sin_ref[...] = s
    nsin_ref[...] = -s


def _rope_tables(positions):
    inv = ROPE_BASE ** (-jnp.arange(0, 2 * ROPE_HALF, 2, dtype=F32) / (2 * ROPE_HALF))
    per_row = LANES // ROPE_HALF
    dense_rows = ROWS // per_row
    pos_dense = jnp.repeat(positions.reshape(ROWS, 1), ROPE_HALF, axis=1).reshape(dense_rows, LANES)
    inv_dense = jnp.tile(inv, per_row).reshape(1, LANES)
    blk = 512
    spec = pl.BlockSpec((blk, LANES), lambda i: (i, 0))
    out = jax.ShapeDtypeStruct((dense_rows, LANES), F32)
    cos_d, sin_d, nsin_d = pl.pallas_call(
        _rope_table_kernel,
        grid=(dense_rows // blk,),
        in_specs=[spec, pl.BlockSpec((1, LANES), lambda i: (0, 0))],
        out_specs=[spec, spec, spec],
        out_shape=[out, out, out],
        name="rope_table",
    )(pos_dense, inv_dense)
    cos32 = cos_d.reshape(ROWS, ROPE_HALF)
    sin32 = sin_d.reshape(ROWS, ROPE_HALF)
    nsin32 = nsin_d.reshape(ROWS, ROPE_HALF)
    cos_t = jnp.tile(cos32, (1, per_row))
    sin_t = jnp.tile(jnp.concatenate([nsin32, sin32], axis=1), (1, per_row // 2))
    return cos_t, sin_t


EVEN_IN_TM = 1024
EVEN_IN_TN = 1024


def _even_in_kernel(x_ref, g_ref, w_ref, cos_ref, sin_ref, o_ref, u_ref):
    j = pl.program_id(1)

    @pl.when(j == 0)
    def _():
        u_ref[...] = _rms(x_ref[...], g_ref[...]).astype(BF16)

    acc = jnp.dot(u_ref[...], w_ref[...], preferred_element_type=F32)

    @pl.when(j == 0)
    def _():
        cos = cos_ref[...]
        sin = sin_ref[...]
        first = _first_half_mask(EVEN_IN_TM)
        for c in range(EVEN_IN_TN // LANES):
            y = _rope_lanes(acc[:, c * LANES:(c + 1) * LANES], cos, sin, first)
            if c * LANES >= _K0:
                y = y * (RET_DK ** -0.5)
            o_ref[:, c * LANES:(c + 1) * LANES] = y.astype(BF16)

    @pl.when(j != 0)
    def _():
        o_ref[...] = acc.astype(BF16)


def _even_in(h, g, w, cos_t, sin_t):
    tm, tn = EVEN_IN_TM, EVEN_IN_TN
    return pl.pallas_call(
        _even_in_kernel,
        grid=(ROWS // tm, EVEN_IN // tn),
        in_specs=[
            pl.BlockSpec((tm, D_MODEL), lambda i, j: (i, 0)),
            pl.BlockSpec((1, D_MODEL), lambda i, j: (0, 0)),
            pl.BlockSpec((D_MODEL, tn), lambda i, j: (0, j)),
            pl.BlockSpec((tm, LANES), lambda i, j: (i, 0)),
            pl.BlockSpec((tm, LANES), lambda i, j: (i, 0)),
        ],
        out_specs=pl.BlockSpec((tm, tn), lambda i, j: (i, j)),
        out_shape=jax.ShapeDtypeStruct((ROWS, EVEN_IN), BF16),
        scratch_shapes=[pltpu.VMEM((tm, D_MODEL), BF16)],
        compiler_params=pltpu.CompilerParams(
            dimension_semantics=("arbitrary", "arbitrary"), vmem_limit_bytes=VMEM_LIMIT),
        name="even_in",
    )(h, g, w, cos_t, sin_t)


EVEN_MIX_R = 512
CONV_HEAD = 8


def _even_mix_kernel(p_ref, h_ref, wout_ref, gn_ref, cw_ref, cb_ref, pn_ref,
                     intra_ref, xi_ref, zeta_ref, gam_ref, bm_ref,
                     o_ref, state_ref, pbuf_ref, mix_ref):
    R = EVEN_MIX_R

    @pl.when(pl.program_id(1) == 0)
    def _():
        state_ref[...] = jnp.zeros_like(state_ref)
        pbuf_ref[0:CONV_HEAD, :] = jnp.zeros((CONV_HEAD, CONV_W), F32)

    pbuf_ref[CONV_HEAD:CONV_HEAD + R, :] = (
        p_ref[:, _CC0:_CC0 + CONV_W].astype(F32) * p_ref[:, _CX0:_CX0 + CONV_W].astype(F32))
    conv = cb_ref[...]
    for tap in range(CONV_K):
        off = CONV_HEAD - (CONV_K - 1) + tap
        conv = conv + pbuf_ref[off:off + R, :] * cw_ref[tap:tap + 1, :]
    pbuf_ref[0:CONV_HEAD, :] = pbuf_ref[R:R + CONV_HEAD, :]
    y_conv = p_ref[:, _CB0:_CB0 + CONV_W].astype(F32) * conv
    mix_ref[:, RET_V:RET_V + CONV_W] = (
        y_conv * _silu(p_ref[:, _GC0:_GC0 + CONV_W].astype(F32))).astype(BF16)

    lane_k = lax.broadcasted_iota(jnp.int32, (RET_CHUNK, LANES), 1)
    lane_v = lax.broadcasted_iota(jnp.int32, (RET_CHUNK, 2 * RET_DV), 1)
    zero_k = jnp.zeros((RET_CHUNK, LANES), BF16)
    zero_v = jnp.zeros((RET_CHUNK, 2 * RET_DV), BF16)

    def chunk_body(c, carry):
        rows = pl.ds(pl.multiple_of(c * RET_CHUNK, RET_CHUNK), RET_CHUNK)
        for j in range(RET_PAIRS):
            qp = p_ref[rows, _Q0 + j * LANES:_Q0 + (j + 1) * LANES]
            kp = p_ref[rows, _K0 + j * LANES:_K0 + (j + 1) * LANES]
            vp = p_ref[rows, _V0 + j * 2 * RET_DV:_V0 + (j + 1) * 2 * RET_DV]
            kstack = jnp.concatenate(
                [jnp.where(lane_k < RET_DK, kp, zero_k), jnp.where(lane_k >= RET_DK, kp, zero_k)], axis=0)
            sc = lax.dot_general(qp, kstack, (((1,), (1,)), ((), ())), preferred_element_type=F32)
            pm = (sc * intra_ref[j]).astype(BF16)
            vblk = jnp.concatenate(
                [jnp.where(lane_v < RET_DV, vp, zero_v), jnp.where(lane_v >= RET_DV, vp, zero_v)], axis=0)
            st = state_ref[j]
            r = (jnp.dot(pm, vblk, preferred_element_type=F32)
                 + jnp.dot(qp, st.astype(BF16), preferred_element_type=F32) * xi_ref[j])
            kz = (kp.astype(F32) * zeta_ref[j]).astype(BF16)
            kv = lax.dot_general(kz, vp, (((0,), (0,)), ((), ())), preferred_element_type=F32)
            state_ref[j] = st * gam_ref[j] + kv * bm_ref[...]
            for hh in range(2):
                hd = 2 * j + hh
                cols = slice(hd * RET_DV, (hd + 1) * RET_DV)
                y = _rms(r[:, hh * RET_DV:(hh + 1) * RET_DV], gn_ref[:, cols])
                gate = p_ref[rows, _GR0 + hd * RET_DV:_GR0 + (hd + 1) * RET_DV].astype(F32)
                mix_ref[rows, cols] = (y * _silu(gate)).astype(BF16)
        return carry

    lax.fori_loop(0, R // RET_CHUNK, chunk_body, 0)

    y = jnp.dot(mix_ref[...], wout_ref[...], preferred_element_type=F32)
    o_ref[...] = h_ref[...] + _rms(y, pn_ref[...])


def _retention_tables():
    h = RET_HEADS
    c = RET_CHUNK
    log_gamma = jnp.log1p(-jnp.exp2(-5.0 - jnp.arange(h, dtype=F32)))
    i = jnp.arange(c, dtype=F32)
    rel = i[:, None] - i[None, :]
    intra = jnp.where(rel >= 0, jnp.exp(log_gamma[:, None, None] * jnp.maximum(rel, 0.0)), 0.0)
    xi = jnp.exp(log_gamma[:, None] * (i + 1.0))
    zeta = jnp.exp(log_gamma[:, None] * (c - 1.0 - i))
    gamma_c = jnp.exp(log_gamma * c)
    intra_p = intra.reshape(RET_PAIRS, 2, c, c).transpose(0, 2, 1, 3).reshape(RET_PAIRS, c, 2 * c)
    xi_p = jnp.repeat(xi.reshape(RET_PAIRS, 2, c).transpose(0, 2, 1), RET_DV, axis=2)
    zeta_p = jnp.repeat(zeta.reshape(RET_PAIRS, 2, c).transpose(0, 2, 1), RET_DK, axis=2)
    gam_rows = jnp.repeat(gamma_c.reshape(RET_PAIRS, 2), RET_DK, axis=1)
    gam_p = jnp.broadcast_to(gam_rows[:, :, None], (RET_PAIRS, 2 * RET_DK, 2 * RET_DV))
    row = np.arange(2 * RET_DK)[:, None] // RET_DK
    col = np.arange(2 * RET_DV)[None, :] // RET_DV
    bm = jnp.asarray((row == col).astype(np.float32))
    return intra_p, xi_p, zeta_p, gam_p, bm


def _even_mix(proj, h, w_out, gn, conv_w, conv_b, post_g, tables):
    R = EVEN_MIX_R
    nb = SEQ // R
    intra_p, xi_p, zeta_p, gam_p, bm = tables
    row_map = lambda b, i: (b * nb + i, 0)
    const2 = lambda b, i: (0, 0)
    const3 = lambda b, i: (0, 0, 0)
    return pl.pallas_call(
        _even_mix_kernel,
        grid=(BATCH, nb),
        in_specs=[
            pl.BlockSpec((R, EVEN_IN), row_map),
            pl.BlockSpec((R, D_MODEL), row_map),
            pl.BlockSpec((EVEN_MIX, D_MODEL), const2),
            pl.BlockSpec((1, RET_V), const2),
            pl.BlockSpec((CONV_K, CONV_W), const2),
            pl.BlockSpec((1, CONV_W), const2),
            pl.BlockSpec((1, D_MODEL), const2),
            pl.BlockSpec((RET_PAIRS, RET_CHUNK, 2 * RET_CHUNK), const3),
            pl.BlockSpec((RET_PAIRS, RET_CHUNK, 2 * RET_DV), const3),
            pl.BlockSpec((RET_PAIRS, RET_CHUNK, 2 * RET_DK), const3),
            pl.BlockSpec((RET_PAIRS, 2 * RET_DK, 2 * RET_DV), const3),
            pl.BlockSpec((2 * RET_DK, 2 * RET_DV), const2),
        ],
        out_specs=pl.BlockSpec((R, D_MODEL), row_map),
        out_shape=jax.ShapeDtypeStruct((ROWS, D_MODEL), F32),
        scratch_shapes=[
            pltpu.VMEM((RET_PAIRS, 2 * RET_DK, 2 * RET_DV), F32),
            pltpu.VMEM((R + CONV_HEAD, CONV_W), F32),
            pltpu.VMEM((R, EVEN_MIX), BF16),
        ],
        compiler_params=pltpu.CompilerParams(
            dimension_semantics=("arbitrary", "arbitrary"), vmem_limit_bytes=VMEM_LIMIT),
        name="even_mix",
    )(proj, h, w_out, gn, conv_w, conv_b, post_g, intra_p, xi_p, zeta_p, gam_p, bm)


ODD_R = 512


def _odd_in_kernel(x_ref, g_ref, wa_ref, wg_ref, qn_ref, wq_ref, kvn_ref, wkv_ref, cos_ref, sin_ref,
                   q_ref, k_ref, vt_ref, sg_ref):
    u = _rms(x_ref[...], g_ref[...]).astype(BF16)
    a = jnp.dot(u, wa_ref[...], preferred_element_type=F32)
    cq = a[:, :Q_LORA]
    ckv = a[:, Q_LORA:Q_LORA + KV_LORA]
    kr = a[:, Q_LORA + KV_LORA:]
    q = jnp.dot(_rms(cq, qn_ref[...]).astype(BF16), wq_ref[...], preferred_element_type=F32) * MLA_SCALE
    kv = jnp.dot(_rms(ckv, kvn_ref[...]).astype(BF16), wkv_ref[...], preferred_element_type=F32)
    cos = cos_ref[...]
    sin = sin_ref[...]
    first = _first_half_mask(ODD_R)
    kr_rot = _rope_lanes(kr, cos, sin, first).astype(BF16)
    for hd in range(MLA_HEADS):
        lo, hi = hd * LANES, (hd + 1) * LANES
        q_ref[0, hd, :, 0:LANES] = q[:, lo:hi].astype(BF16)
        q_ref[0, hd, :, LANES:2 * LANES] = _rope_lanes(q[:, MLA_V + lo:MLA_V + hi], cos, sin, first).astype(BF16)
        k_ref[0, hd, :, 0:LANES] = kv[:, lo:hi].astype(BF16)
        k_ref[0, hd, :, LANES:2 * LANES] = kr_rot
        vt_ref[0, hd, 0] = kv[:, MLA_V + lo:MLA_V + hi].T.astype(BF16)
    gate = jnp.dot(u, wg_ref[...], preferred_element_type=F32)
    sg_ref[...] = _silu(gate).astype(BF16)


def _odd_in(h, g, wa, wg, qn, wq, kvn, wkv, cos_t, sin_t):
    R = ODD_R
    nb = SEQ // R
    row_map = lambda b, i: (b * nb + i, 0)
    const2 = lambda b, i: (0, 0)
    return pl.pallas_call(
        _odd_in_kernel,
        grid=(BATCH, nb),
        in_specs=[
            pl.BlockSpec((R, D_MODEL), row_map),
            pl.BlockSpec((1, D_MODEL), const2),
            pl.BlockSpec((D_MODEL, ODD_A), const2),
            pl.BlockSpec((D_MODEL, MLA_V), const2),
            pl.BlockSpec((1, Q_LORA), const2),
            pl.BlockSpec((Q_LORA, 2 * MLA_V), const2),
            pl.BlockSpec((1, KV_LORA), const2),
            pl.BlockSpec((KV_LORA, 2 * MLA_V), const2),
            pl.BlockSpec((R, LANES), row_map),
            pl.BlockSpec((R, LANES), row_map),
        ],
        out_specs=[
            pl.BlockSpec((1, MLA_HEADS, R, MLA_QK_PAD), lambda b, i: (b, 0, i, 0)),
            pl.BlockSpec((1, MLA_HEADS, R, MLA_QK_PAD), lambda b, i: (b, 0, i, 0)),
            pl.BlockSpec((1, MLA_HEADS, 1, MLA_DV, R), lambda b, i: (b, 0, i, 0, 0)),
            pl.BlockSpec((R, MLA_V), row_map),
        ],
        out_shape=[
            jax.ShapeDtypeStruct((BATCH, MLA_HEADS, SEQ, MLA_QK_PAD), BF16),
            jax.ShapeDtypeStruct((BATCH, MLA_HEADS, SEQ, MLA_QK_PAD), BF16),
            jax.ShapeDtypeStruct((BATCH, MLA_HEADS, nb, MLA_DV, R), BF16),
            jax.ShapeDtypeStruct((ROWS, MLA_V), BF16),
        ],
        compiler_params=pltpu.CompilerParams(
            dimension_semantics=("arbitrary", "arbitrary"), vmem_limit_bytes=VMEM_LIMIT),
        name="odd_in",
    )(h, g, wa, wg, qn, wq, kvn, wkv, cos_t, sin_t)


ATTN_T = ODD_R


def _attn_kernel(q_ref, k_ref, vt_ref, o_ref, m_ref, l_ref, acc_ref):
    T = ATTN_T
    qi = pl.program_id(2)
    q = q_ref[0, 0]
    m_ref[...] = jnp.full((1, T), -jnp.inf, F32)
    l_ref[...] = jnp.zeros((1, T), F32)
    acc_ref[...] = jnp.zeros((MLA_DV, T), F32)

    def block(j, masked):
        k = k_ref[0, 0, pl.ds(pl.multiple_of(j * T, T), T), :]
        st = lax.dot_general(k, q, (((1,), (1,)), ((), ())), preferred_element_type=F32)
        if masked:
            kpos = lax.broadcasted_iota(jnp.int32, (T, T), 0)
            qpos = lax.broadcasted_iota(jnp.int32, (T, T), 1)
            st = jnp.where(kpos <= qpos, st, -jnp.inf)
        m_old = m_ref[...]
        m_new = jnp.maximum(m_old, jnp.max(st, axis=0, keepdims=True))
        alpha = jnp.exp(m_old - m_new)
        p = jnp.exp(st - m_new)
        l_ref[...] = alpha * l_ref[...] + jnp.sum(p, axis=0, keepdims=True)
        acc_ref[...] = alpha * acc_ref[...] + jnp.dot(
            vt_ref[0, 0, j], p.astype(BF16), preferred_element_type=F32)
        m_ref[...] = m_new

    def body(j, carry):
        block(j, masked=False)
        return carry

    lax.fori_loop(0, qi, body, 0)
    block(qi, masked=True)
    o = acc_ref[...] * (1.0 / l_ref[...])
    o_ref[0] = o.T.astype(BF16)


def _attn(q_cat, k_cat, vt):
    T = ATTN_T
    nq = SEQ // T
    return pl.pallas_call(
        _attn_kernel,
        grid=(BATCH, MLA_HEADS, nq),
        in_specs=[
            pl.BlockSpec((1, 1, T, MLA_QK_PAD), lambda b, h, i: (b, h, i, 0)),
            pl.BlockSpec((1, 1, SEQ, MLA_QK_PAD), lambda b, h, i: (b, h, 0, 0)),
            pl.BlockSpec((1, 1, nq, MLA_DV, T), lambda b, h, i: (b, h, 0, 0, 0)),
        ],
        out_specs=pl.BlockSpec((1, T, MLA_DV), lambda b, h, i: (b, i, h)),
        out_shape=jax.ShapeDtypeStruct((BATCH, SEQ, MLA_V), BF16),
        scratch_shapes=[
            pltpu.VMEM((1, T), F32),
            pltpu.VMEM((1, T), F32),
            pltpu.VMEM((MLA_DV, T), F32),
        ],
        compiler_params=pltpu.CompilerParams(
            dimension_semantics=("arbitrary", "arbitrary", "arbitrary"), vmem_limit_bytes=VMEM_LIMIT),
        name="odd_attn",
    )(q_cat, k_cat, vt)


ODD_OUT_R = 1024


def _odd_out_kernel(o_ref, sg_ref, h_ref, w_ref, pn_ref, out_ref):
    og = (o_ref[...].astype(F32) * sg_ref[...].astype(F32)).astype(BF16)
    y = jnp.dot(og, w_ref[...], preferred_element_type=F32)
    out_ref[...] = h_ref[...] + _rms(y, pn_ref[...])


def _odd_out(o, sg, h, w, post_g):
    R = ODD_OUT_R
    row_map = lambda i: (i, 0)
    const2 = lambda i: (0, 0)
    return pl.pallas_call(
        _odd_out_kernel,
        grid=(ROWS // R,),
        in_specs=[
            pl.BlockSpec((R, MLA_V), row_map),
            pl.BlockSpec((R, MLA_V), row_map),
            pl.BlockSpec((R, D_MODEL), row_map),
            pl.BlockSpec((MLA_V, D_MODEL), const2),
            pl.BlockSpec((1, D_MODEL), const2),
        ],
        out_specs=pl.BlockSpec((R, D_MODEL), row_map),
        out_shape=jax.ShapeDtypeStruct((ROWS, D_MODEL), F32),
        compiler_params=pltpu.CompilerParams(
            dimension_semantics=("arbitrary",), vmem_limit_bytes=VMEM_LIMIT),
        name="odd_out",
    )(o, sg, h, w, post_g)


def _prep_odd_weights(w_in, w_qb, w_kvb):
    lat = Q_LORA + KV_LORA
    wa = jnp.concatenate(
        [w_in[:, :lat + MLA_ROPE], jnp.zeros((D_MODEL, LANES - MLA_ROPE), w_in.dtype)], axis=1).astype(BF16)
    wg = w_in[:, lat + MLA_ROPE:].astype(BF16)
    wq3 = w_qb.reshape(Q_LORA, MLA_HEADS, MLA_QK)
    wq_nope = wq3[:, :, :MLA_NOPE].reshape(Q_LORA, MLA_HEADS * MLA_NOPE)
    wq_rope = jnp.pad(wq3[:, :, MLA_NOPE:], ((0, 0), (0, 0), (0, LANES - MLA_ROPE))).reshape(
        Q_LORA, MLA_HEADS * LANES)
    wq = jnp.concatenate([wq_nope, wq_rope], axis=1).astype(BF16)
    wkv3 = w_kvb.reshape(KV_LORA, MLA_HEADS, MLA_NOPE + MLA_DV)
    wkv = jnp.concatenate(
        [wkv3[:, :, :MLA_NOPE].reshape(KV_LORA, MLA_HEADS * MLA_NOPE),
         wkv3[:, :, MLA_NOPE:].reshape(KV_LORA, MLA_V)], axis=1).astype(BF16)
    return wa, wg, wq, wkv


def kernel(x, positions, pre_norm, post_norm, even_w_in, even_conv_w, even_conv_b, ret_gn, even_w_out,
           odd_w_in, q_a_norm, w_qb, kv_a_norm, w_kvb, odd_w_out):
    cos_t, sin_t = _rope_tables(positions)
    tables = _retention_tables()
    h = x.reshape(ROWS, D_MODEL)
    for layer in range(DEPTH):
        pre_g = pre_norm[layer].reshape(1, D_MODEL)
        post_g = post_norm[layer].reshape(1, D_MODEL)
        if layer % 2 == 0:
            e = layer // 2
            proj = _even_in(h, pre_g, even_w_in[e].astype(BF16), cos_t, sin_t)
            h = _even_mix(proj, h, even_w_out[e].astype(BF16), ret_gn[e].reshape(1, RET_V),
                          even_conv_w[e], even_conv_b[e].reshape(1, CONV_W), post_g, tables)
        else:
            o = layer // 2
            wa, wg, wq, wkv = _prep_odd_weights(odd_w_in[o], w_qb[o], w_kvb[o])
            q_cat, k_cat, vt, sg = _odd_in(h, pre_g, wa, wg, q_a_norm[o].reshape(1, Q_LORA), wq,
                                           kv_a_norm[o].reshape(1, KV_LORA), wkv, cos_t, sin_t)
            attn = _attn(q_cat, k_cat, vt)
            h = _odd_out(attn.reshape(ROWS, MLA_V), sg, h, odd_w_out[o].astype(BF16), post_g)
    return h.reshape(BATCH, SEQ, D_MODEL)
```

```python
import math

import jax
import jax.numpy as jnp
import numpy as np
from jax import lax
from jax.experimental import pallas as pl
from jax.experimental.pallas import tpu as pltpu

D_MODEL = 1024
BATCH = 4
SEQ = 4096
DEPTH = 4
ROWS = BATCH * SEQ

RET_HEADS = 8
RET_DK = 64
RET_DV = 128
RET_CHUNK = 128
RET_PAIRS = RET_HEADS // 2
CONV_W = 1024
CONV_K = 3
MLA_HEADS = 8
MLA_NOPE = 128
MLA_ROPE = 64
MLA_DV = 128
Q_LORA = 384
KV_LORA = 256
ROPE_BASE = 10000.0
EPS = 1e-6

RET_QK = RET_HEADS * RET_DK
RET_V = RET_HEADS * RET_DV
EVEN_IN = 2 * RET_QK + 2 * RET_V + 4 * CONV_W
EVEN_MIX = RET_V + CONV_W
MLA_QK = MLA_NOPE + MLA_ROPE
MLA_V = MLA_HEADS * MLA_DV
MLA_SCALE = MLA_QK ** -0.5
LOG2E = math.log2(math.e)
MLA_QK_PAD = 256
ODD_A = Q_LORA + KV_LORA + 128

_Q0, _K0, _V0, _GR0, _CB0, _CC0, _CX0, _GC0 = 0, 512, 1024, 2048, 3072, 4096, 5120, 6144

LANES = 128
SUBLANES = 8
ROPE_HALF = 32
HALF_TILE = LANES // 2

VMEM_LIMIT = 56 * 1024 * 1024

BF16 = jnp.bfloat16
F32 = jnp.float32


def _silu(x):
    return (0.5 * x) * (1.0 + jnp.tanh(0.5 * x))


def _rms(x, g):
    ms = jnp.mean(x * x, axis=-1, keepdims=True)
    return x * lax.rsqrt(ms + EPS) * g


def _rope_tile(x, cos, sin_signed):
    return x * cos + pltpu.roll(x, HALF_TILE, 1) * sin_signed


ROPE_GROUPS = LANES // ROPE_HALF
ROPE_BLK = 512


def _rope_table_kernel(pos_ref, inv_ref, cos_ref, sin_ref):
    ang = pos_ref[...].astype(F32) * inv_ref[...]
    c = jnp.cos(ang)
    s = jnp.sin(ang)
    lane = lax.broadcasted_iota(jnp.int32, (ROPE_BLK, LANES), 1)
    low = lane < ROPE_HALF
    for g in range(ROPE_GROUPS):
        shift = (LANES - g * ROPE_HALF) % LANES
        cg = jnp.where(low, c if shift == 0 else pltpu.roll(c, shift, 1), 0.0)
        sg = jnp.where(low, s if shift == 0 else pltpu.roll(s, shift, 1), 0.0)
        cg = cg + pltpu.roll(cg, ROPE_HALF, 1)
        sg = sg + pltpu.roll(sg, ROPE_HALF, 1)
        cos_ref[g] = cg + pltpu.roll(cg, HALF_TILE, 1)
        sin_ref[g] = pltpu.roll(sg, HALF_TILE, 1) - sg


def _rope_tables(positions):
    inv = ROPE_BASE ** (-jnp.arange(0, 2 * ROPE_HALF, 2, dtype=F32) / (2 * ROPE_HALF))
    dense_rows = ROWS // ROPE_GROUPS
    pos_dense = jnp.repeat(positions.reshape(ROPE_GROUPS, dense_rows).T, ROPE_HALF, axis=1)
    inv_dense = jnp.tile(inv, ROPE_GROUPS).reshape(1, LANES)
    out_spec = pl.BlockSpec((ROPE_GROUPS, ROPE_BLK, LANES), lambda i: (0, i, 0))
    out = jax.ShapeDtypeStruct((ROPE_GROUPS, dense_rows, LANES), F32)
    cos_t, sin_t = pl.pallas_call(
        _rope_table_kernel,
        grid=(dense_rows // ROPE_BLK,),
        in_specs=[pl.BlockSpec((ROPE_BLK, LANES), lambda i: (i, 0)),
                  pl.BlockSpec((1, LANES), lambda i: (0, 0))],
        out_specs=[out_spec, out_spec],
        out_shape=[out, out],
        name="rope_table",
    )(pos_dense, inv_dense)
    return cos_t.reshape(ROWS, LANES), sin_t.reshape(ROWS, LANES)


EVEN_IN_TM = 1024
EVEN_IN_TN = 1024


def _even_in_kernel(x_ref, g_ref, w_ref, cos_ref, sin_ref, o_ref, u_ref):
    j = pl.program_id(1)

    @pl.when(j == 0)
    def _():
        u = _rms(x_ref[...], g_ref[...]).astype(BF16)
        u_ref[...] = u
        acc = jnp.dot(u, w_ref[...], preferred_element_type=F32)
        cos = cos_ref[...]
        sin = sin_ref[...]
        for c in range(EVEN_IN_TN // LANES):
            y = _rope_tile(acc[:, c * LANES:(c + 1) * LANES], cos, sin)
            if c * LANES >= _K0:
                y = y * (RET_DK ** -0.5)
            o_ref[:, c * LANES:(c + 1) * LANES] = y.astype(BF16)

    @pl.when(j != 0)
    def _():
        o_ref[...] = jnp.dot(u_ref[...], w_ref[...], preferred_element_type=F32).astype(BF16)


def _even_in(h, g, w, cos_t, sin_t):
    tm, tn = EVEN_IN_TM, EVEN_IN_TN
    return pl.pallas_call(
        _even_in_kernel,
        grid=(ROWS // tm, EVEN_IN // tn),
        in_specs=[
            pl.BlockSpec((tm, D_MODEL), lambda i, j: (i, 0)),
            pl.BlockSpec((1, D_MODEL), lambda i, j: (0, 0)),
            pl.BlockSpec((D_MODEL, tn), lambda i, j: (0, j)),
            pl.BlockSpec((tm, LANES), lambda i, j: (i, 0)),
            pl.BlockSpec((tm, LANES), lambda i, j: (i, 0)),
        ],
        out_specs=pl.BlockSpec((tm, tn), lambda i, j: (i, j)),
        out_shape=jax.ShapeDtypeStruct((ROWS, EVEN_IN), BF16),
        scratch_shapes=[pltpu.VMEM((tm, D_MODEL), BF16)],
        compiler_params=pltpu.CompilerParams(
            dimension_semantics=("arbitrary", "arbitrary"), vmem_limit_bytes=VMEM_LIMIT),
        name="even_in",
    )(h, g, w, cos_t, sin_t)


EVEN_MIX_R = 512


def _even_mix_kernel(p_ref, h_ref, wout_ref, gn_ref, cw_ref, cb_ref, pn_ref,
                     intra_ref, xi_ref, zeta_ref, gam_ref, bm_ref,
                     o_ref, state_ref, tail_ref, mix_ref):
    C = RET_CHUNK

    @pl.when(pl.program_id(1) == 0)
    def _():
        state_ref[...] = jnp.zeros_like(state_ref)
        tail_ref[...] = jnp.zeros_like(tail_ref)

    lane_k = lax.broadcasted_iota(jnp.int32, (C, LANES), 1)
    first_head = (lane_k % HALF_TILE) < ROPE_HALF
    lane_v = lax.broadcasted_iota(jnp.int32, (C, 2 * RET_DV), 1)
    zero_k = jnp.zeros((C, LANES), BF16)
    zero_v = jnp.zeros((C, 2 * RET_DV), BF16)

    tail = tail_ref[...]
    states = [state_ref[j] for j in range(RET_PAIRS)]
    for c in range(EVEN_MIX_R // C):
        rows = slice(c * C, (c + 1) * C)

        prod = p_ref[rows, _CC0:_CC0 + CONV_W].astype(F32) * p_ref[rows, _CX0:_CX0 + CONV_W].astype(F32)
        ext = jnp.concatenate([tail, prod], axis=0)
        conv = cb_ref[...] + prod * cw_ref[CONV_K - 1:CONV_K, :]
        for tap in range(CONV_K - 1):
            off = SUBLANES - (CONV_K - 1) + tap
            conv = conv + ext[off:off + C, :] * cw_ref[tap:tap + 1, :]
        tail = prod[C - SUBLANES:, :]
        y_conv = p_ref[rows, _CB0:_CB0 + CONV_W].astype(F32) * conv
        mix_ref[rows, RET_V:RET_V + CONV_W] = (
            y_conv * _silu(p_ref[rows, _GC0:_GC0 + CONV_W].astype(F32))).astype(BF16)

        for j in range(RET_PAIRS):
            qp = p_ref[rows, _Q0 + j * LANES:_Q0 + (j + 1) * LANES]
            kp = p_ref[rows, _K0 + j * LANES:_K0 + (j + 1) * LANES]
            vp = p_ref[rows, _V0 + j * 2 * RET_DV:_V0 + (j + 1) * 2 * RET_DV]
            kstack = jnp.concatenate(
                [jnp.where(first_head, kp, zero_k), jnp.where(first_head, zero_k, kp)], axis=0)
            sc = lax.dot_general(qp, kstack, (((1,), (1,)), ((), ())), preferred_element_type=F32)
            pm = (sc * intra_ref[j]).astype(BF16)
            vblk = jnp.concatenate(
                [jnp.where(lane_v < RET_DV, vp, zero_v), jnp.where(lane_v >= RET_DV, vp, zero_v)], axis=0)
            st = states[j]
            r = (jnp.dot(pm, vblk, preferred_element_type=F32)
                 + jnp.dot(qp, st.astype(BF16), preferred_element_type=F32) * xi_ref[j])
            kz = (kp.astype(F32) * zeta_ref[j]).astype(BF16)
            kv = lax.dot_general(kz, vp, (((0,), (0,)), ((), ())), preferred_element_type=F32)
            states[j] = st * gam_ref[j] + kv * bm_ref[...]
            for hh in range(2):
                hd = 2 * j + hh
                cols = slice(hd * RET_DV, (hd + 1) * RET_DV)
                y = _rms(r[:, hh * RET_DV:(hh + 1) * RET_DV], gn_ref[:, cols])
                gate = p_ref[rows, _GR0 + hd * RET_DV:_GR0 + (hd + 1) * RET_DV].astype(F32)
                mix_ref[rows, cols] = (y * _silu(gate)).astype(BF16)

        y = jnp.dot(mix_ref[rows, :], wout_ref[...], preferred_element_type=F32)
        o_ref[rows, :] = h_ref[rows, :] + _rms(y, pn_ref[...])

    tail_ref[...] = tail
    for j in range(RET_PAIRS):
        state_ref[j] = states[j]


def _retention_tables():
    h, c = RET_HEADS, RET_CHUNK
    log_gamma = np.log1p(-np.exp2(-5.0 - np.arange(h, dtype=np.float64)))
    i = np.arange(c, dtype=np.float64)
    rel = i[:, None] - i[None, :]
    intra = np.where(rel >= 0, np.exp(log_gamma[:, None, None] * np.maximum(rel, 0.0)), 0.0)
    xi = np.exp(log_gamma[:, None] * (i + 1.0))
    zeta = np.exp(log_gamma[:, None] * (c - 1.0 - i))
    gamma_c = np.exp(log_gamma * c)
    pair = np.arange(RET_PAIRS)
    intra_p = intra.reshape(RET_PAIRS, 2, c, c).transpose(0, 2, 1, 3).reshape(RET_PAIRS, c, 2 * c)
    xi_p = np.repeat(xi.reshape(RET_PAIRS, 2, c).transpose(0, 2, 1), RET_DV, axis=2)
    second = ((np.arange(2 * RET_DK) % HALF_TILE) >= ROPE_HALF).astype(np.int64)
    zeta_p = zeta.reshape(RET_PAIRS, 2, c)[pair[:, None, None], second[None, None, :], np.arange(c)[None, :, None]]
    gam_rows = gamma_c.reshape(RET_PAIRS, 2)[:, second]
    gam_p = np.broadcast_to(gam_rows[:, :, None], (RET_PAIRS, 2 * RET_DK, 2 * RET_DV))
    col_head = np.arange(2 * RET_DV) // RET_DV
    bm = (second[:, None] == col_head[None, :])
    as_f32 = lambda a: jnp.asarray(np.ascontiguousarray(a, dtype=np.float32))
    return as_f32(intra_p), as_f32(xi_p), as_f32(zeta_p), as_f32(gam_p), as_f32(bm)


def _even_mix(proj, h, w_out, gn, conv_w, conv_b, post_g, tables):
    R = EVEN_MIX_R
    nb = SEQ // R
    intra_p, xi_p, zeta_p, gam_p, bm = tables
    row_map = lambda b, i: (b * nb + i, 0)
    const2 = lambda b, i: (0, 0)
    const3 = lambda b, i: (0, 0, 0)
    return pl.pallas_call(
        _even_mix_kernel,
        grid=(BATCH, nb),
        in_specs=[
            pl.BlockSpec((R, EVEN_IN), row_map),
            pl.BlockSpec((R, D_MODEL), row_map),
            pl.BlockSpec((EVEN_MIX, D_MODEL), const2),
            pl.BlockSpec((1, RET_V), const2),
            pl.BlockSpec((CONV_K, CONV_W), const2),
            pl.BlockSpec((1, CONV_W), const2),
            pl.BlockSpec((1, D_MODEL), const2),
            pl.BlockSpec((RET_PAIRS, RET_CHUNK, 2 * RET_CHUNK), const3),
            pl.BlockSpec((RET_PAIRS, RET_CHUNK, 2 * RET_DV), const3),
            pl.BlockSpec((RET_PAIRS, RET_CHUNK, 2 * RET_DK), const3),
            pl.BlockSpec((RET_PAIRS, 2 * RET_DK, 2 * RET_DV), const3),
            pl.BlockSpec((2 * RET_DK, 2 * RET_DV), const2),
        ],
        out_specs=pl.BlockSpec((R, D_MODEL), row_map),
        out_shape=jax.ShapeDtypeStruct((ROWS, D_MODEL), F32),
        scratch_shapes=[
            pltpu.VMEM((RET_PAIRS, 2 * RET_DK, 2 * RET_DV), F32),
            pltpu.VMEM((SUBLANES, CONV_W), F32),
            pltpu.VMEM((R, EVEN_MIX), BF16),
        ],
        compiler_params=pltpu.CompilerParams(
            dimension_semantics=("arbitrary", "arbitrary"), vmem_limit_bytes=VMEM_LIMIT),
        name="even_mix",
    )(proj, h, w_out, gn, conv_w, conv_b, post_g, intra_p, xi_p, zeta_p, gam_p, bm)


ODD_R = 512


def _odd_in_kernel(x_ref, g_ref, wa_ref, wg_ref, qn_ref, wq_ref, kvn_ref, wkv_ref, cos_ref, sin_ref,
                   q_ref, k_ref, vt_ref, sg_ref):
    u = _rms(x_ref[...], g_ref[...]).astype(BF16)
    a = jnp.dot(u, wa_ref[...], preferred_element_type=F32)
    cq = a[:, :Q_LORA]
    ckv = a[:, Q_LORA:Q_LORA + KV_LORA]
    kr = a[:, Q_LORA + KV_LORA:]
    q = jnp.dot(_rms(cq, qn_ref[...]).astype(BF16), wq_ref[...], preferred_element_type=F32) * (MLA_SCALE * LOG2E)
    kv = jnp.dot(_rms(ckv, kvn_ref[...]).astype(BF16), wkv_ref[...], preferred_element_type=F32)
    cos = cos_ref[...]
    sin = sin_ref[...]
    kr_rot = _rope_tile(kr, cos, sin).astype(BF16)
    for hd in range(MLA_HEADS):
        lo, hi = hd * LANES, (hd + 1) * LANES
        q_ref[0, hd, :, 0:LANES] = q[:, lo:hi].astype(BF16)
        q_ref[0, hd, :, LANES:2 * LANES] = _rope_tile(q[:, MLA_V + lo:MLA_V + hi], cos, sin).astype(BF16)
        k_ref[0, hd, :, 0:LANES] = kv[:, lo:hi].astype(BF16)
        k_ref[0, hd, :, LANES:2 * LANES] = kr_rot
        vt_ref[0, hd, 0] = kv[:, MLA_V + lo:MLA_V + hi].T.astype(BF16)
    gate = jnp.dot(u, wg_ref[...], preferred_element_type=F32)
    sg_ref[...] = _silu(gate).astype(BF16)


def _odd_in(h, g, wa, wg, qn, wq, kvn, wkv, cos_t, sin_t):
    R = ODD_R
    nb = SEQ // R
    row_map = lambda b, i: (b * nb + i, 0)
    const2 = lambda b, i: (0, 0)
    return pl.pallas_call(
        _odd_in_kernel,
        grid=(BATCH, nb),
        in_specs=[
            pl.BlockSpec((R, D_MODEL), row_map),
            pl.BlockSpec((1, D_MODEL), const2),
            pl.BlockSpec((D_MODEL, ODD_A), const2),
            pl.BlockSpec((D_MODEL, MLA_V), const2),
            pl.BlockSpec((1, Q_LORA), const2),
            pl.BlockSpec((Q_LORA, 2 * MLA_V), const2),
            pl.BlockSpec((1, KV_LORA), const2),
            pl.BlockSpec((KV_LORA, 2 * MLA_V), const2),
            pl.BlockSpec((R, LANES), row_map),
            pl.BlockSpec((R, LANES), row_map),
        ],
        out_specs=[
            pl.BlockSpec((1, MLA_HEADS, R, MLA_QK_PAD), lambda b, i: (b, 0, i, 0)),
            pl.BlockSpec((1, MLA_HEADS, R, MLA_QK_PAD), lambda b, i: (b, 0, i, 0)),
            pl.BlockSpec((1, MLA_HEADS, 1, MLA_DV, R), lambda b, i: (b, 0, i, 0, 0)),
            pl.BlockSpec((R, MLA_V), row_map),
        ],
        out_shape=[
            jax.ShapeDtypeStruct((BATCH, MLA_HEADS, SEQ, MLA_QK_PAD), BF16),
            jax.ShapeDtypeStruct((BATCH, MLA_HEADS, SEQ, MLA_QK_PAD), BF16),
            jax.ShapeDtypeStruct((BATCH, MLA_HEADS, nb, MLA_DV, R), BF16),
            jax.ShapeDtypeStruct((ROWS, MLA_V), BF16),
        ],
        compiler_params=pltpu.CompilerParams(
            dimension_semantics=("arbitrary", "arbitrary"), vmem_limit_bytes=VMEM_LIMIT),
        name="odd_in",
    )(h, g, wa, wg, qn, wq, kvn, wkv, cos_t, sin_t)


ATTN_T = ODD_R
ATTN_HG = 4


def _attn_kernel(q_ref, k_ref, vt_ref, o_ref, m_ref, l_ref, acc_ref, st_ref):
    T = ATTN_T
    qi = pl.program_id(2)
    m_ref[...] = jnp.full(m_ref.shape, -jnp.inf, F32)
    l_ref[...] = jnp.zeros(l_ref.shape, F32)
    acc_ref[...] = jnp.zeros(acc_ref.shape, F32)

    def scores(hh, j):
        k = k_ref[0, hh, pl.ds(pl.multiple_of(j * T, T), T), :]
        return lax.dot_general(k, q_ref[0, hh], (((1,), (1,)), ((), ())), preferred_element_type=F32)

    def consume(hh, j, st):
        m_old = m_ref[hh]
        m_new = jnp.maximum(m_old, jnp.max(st, axis=0, keepdims=True))
        alpha = jnp.exp2(m_old - m_new)
        p = jnp.exp2(st - m_new)
        l_ref[hh] = alpha * l_ref[hh] + jnp.sum(p, axis=0, keepdims=True)
        acc_ref[hh] = alpha * acc_ref[hh] + jnp.dot(
            vt_ref[0, hh, j], p.astype(BF16), preferred_element_type=F32)
        m_ref[hh] = m_new

    for hh in range(ATTN_HG):
        st_ref[hh] = scores(hh, 0)

    def body(j, carry):
        for hh in range(ATTN_HG):
            st = st_ref[hh]
            st_ref[hh] = scores(hh, j + 1)
            consume(hh, j, st)
        return carry

    lax.fori_loop(0, qi, body, 0)
    kpos = lax.broadcasted_iota(jnp.int32, (T, T), 0)
    qpos = lax.broadcasted_iota(jnp.int32, (T, T), 1)
    for hh in range(ATTN_HG):
        consume(hh, qi, jnp.where(kpos <= qpos, st_ref[hh], -jnp.inf))
    for hh in range(ATTN_HG):
        o = acc_ref[hh] * (1.0 / l_ref[hh])
        o_ref[0, :, hh * MLA_DV:(hh + 1) * MLA_DV] = o.T.astype(BF16)


def _attn(q_cat, k_cat, vt):
    T = ATTN_T
    HG = ATTN_HG
    nq = SEQ // T
    return pl.pallas_call(
        _attn_kernel,
        grid=(BATCH, MLA_HEADS // HG, nq),
        in_specs=[
            pl.BlockSpec((1, HG, T, MLA_QK_PAD), lambda b, h, i: (b, h, i, 0)),
            pl.BlockSpec((1, HG, SEQ, MLA_QK_PAD), lambda b, h, i: (b, h, 0, 0)),
            pl.BlockSpec((1, HG, nq, MLA_DV, T), lambda b, h, i: (b, h, 0, 0, 0)),
        ],
        out_specs=pl.BlockSpec((1, T, HG * MLA_DV), lambda b, h, i: (b, i, h)),
        out_shape=jax.ShapeDtypeStruct((BATCH, SEQ, MLA_V), BF16),
        scratch_shapes=[
            pltpu.VMEM((HG, 1, T), F32),
            pltpu.VMEM((HG, 1, T), F32),
            pltpu.VMEM((HG, MLA_DV, T), F32),
            pltpu.VMEM((HG, T, T), F32),
        ],
        compiler_params=pltpu.CompilerParams(
            dimension_semantics=("arbitrary", "arbitrary", "arbitrary"), vmem_limit_bytes=VMEM_LIMIT),
        name="odd_attn",
    )(q_cat, k_cat, vt)


ODD_OUT_R = 1024


def _odd_out_kernel(o_ref, sg_ref, h_ref, w_ref, pn_ref, out_ref):
    og = (o_ref[...].astype(F32) * sg_ref[...].astype(F32)).astype(BF16)
    y = jnp.dot(og, w_ref[...], preferred_element_type=F32)
    out_ref[...] = h_ref[...] + _rms(y, pn_ref[...])


def _odd_out(o, sg, h, w, post_g):
    R = ODD_OUT_R
    row_map = lambda i: (i, 0)
    const2 = lambda i: (0, 0)
    return pl.pallas_call(
        _odd_out_kernel,
        grid=(ROWS // R,),
        in_specs=[
            pl.BlockSpec((R, MLA_V), row_map),
            pl.BlockSpec((R, MLA_V), row_map),
            pl.BlockSpec((R, D_MODEL), row_map),
            pl.BlockSpec((MLA_V, D_MODEL), const2),
            pl.BlockSpec((1, D_MODEL), const2),
        ],
        out_specs=pl.BlockSpec((R, D_MODEL), row_map),
        out_shape=jax.ShapeDtypeStruct((ROWS, D_MODEL), F32),
        compiler_params=pltpu.CompilerParams(
            dimension_semantics=("arbitrary",), vmem_limit_bytes=VMEM_LIMIT),
        name="odd_out",
    )(o, sg, h, w, post_g)


def _prep_even_w_in(w):
    qk = w[:, :_V0].reshape(D_MODEL, 2 * RET_PAIRS, 2, 2, ROPE_HALF)
    qk = qk.transpose(0, 1, 3, 2, 4).reshape(D_MODEL, _V0)
    return jnp.concatenate([qk, w[:, _V0:]], axis=1).astype(BF16)


def _rope_cols(w):
    z = jnp.zeros(w.shape[:-1] + (ROPE_HALF,), w.dtype)
    return jnp.concatenate([w[..., :ROPE_HALF], z, w[..., ROPE_HALF:], z], axis=-1)


def _prep_odd_weights(w_in, w_qb, w_kvb):
    lat = Q_LORA + KV_LORA
    wa = jnp.concatenate([w_in[:, :lat], _rope_cols(w_in[:, lat:lat + MLA_ROPE])], axis=1).astype(BF16)
    wg = w_in[:, lat + MLA_ROPE:].astype(BF16)
    wq3 = w_qb.reshape(Q_LORA, MLA_HEADS, MLA_QK)
    wq_nope = wq3[:, :, :MLA_NOPE].reshape(Q_LORA, MLA_HEADS * MLA_NOPE)
    wq_rope = _rope_cols(wq3[:, :, MLA_NOPE:]).reshape(Q_LORA, MLA_HEADS * LANES)
    wq = jnp.concatenate([wq_nope, wq_rope], axis=1).astype(BF16)
    wkv3 = w_kvb.reshape(KV_LORA, MLA_HEADS, MLA_NOPE + MLA_DV)
    wkv = jnp.concatenate(
        [wkv3[:, :, :MLA_NOPE].reshape(KV_LORA, MLA_HEADS * MLA_NOPE),
         wkv3[:, :, MLA_NOPE:].reshape(KV_LORA, MLA_V)], axis=1).astype(BF16)
    return wa, wg, wq, wkv


def kernel(x, positions, pre_norm, post_norm, even_w_in, even_conv_w, even_conv_b, ret_gn, even_w_out,
           odd_w_in, q_a_norm, w_qb, kv_a_norm, w_kvb, odd_w_out):
    cos_t, sin_t = _rope_tables(positions)
    tables = _retention_tables()
    h = x.reshape(ROWS, D_MODEL)
    for layer in range(DEPTH):
        pre_g = pre_norm[layer].reshape(1, D_MODEL)
        post_g = post_norm[layer].reshape(1, D_MODEL)
        if layer % 2 == 0:
            e = layer // 2
            proj = _even_in(h, pre_g, _prep_even_w_in(even_w_in[e]), cos_t, sin_t)
            h = _even_mix(proj, h, even_w_out[e].astype(BF16), ret_gn[e].reshape(1, RET_V),
                          even_conv_w[e], even_conv_b[e].reshape(1, CONV_W), post_g, tables)
        else:
            o = layer // 2
            wa, wg, wq, wkv = _prep_odd_weights(odd_w_in[o], w_qb[o], w_kvb[o])
            q_cat, k_cat, vt, sg = _odd_in(h, pre_g, wa, wg, q_a_norm[o].reshape(1, Q_LORA), wq,
                                           kv_a_norm[o].reshape(1, KV_LORA), wkv, cos_t, sin_t)
            attn = _attn(q_cat, k_cat, vt)
            h = _odd_out(attn.reshape(ROWS, MLA_V), sg, h, odd_w_out[o].astype(BF16), post_g)
    return h.reshape(BATCH, SEQ, D_MODEL)
```

```python
import math

import jax
import jax.numpy as jnp
import numpy as np
from jax import lax
from jax.experimental import pallas as pl
from jax.experimental.pallas import tpu as pltpu

D_MODEL = 1024
BATCH = 4
SEQ = 4096
DEPTH = 4
ROWS = BATCH * SEQ

RET_HEADS = 8
RET_DK = 64
RET_DV = 128
RET_CHUNK = 128
RET_PAIRS = RET_HEADS // 2
CONV_W = 1024
CONV_K = 3
MLA_HEADS = 8
MLA_NOPE = 128
MLA_ROPE = 64
MLA_DV = 128
Q_LORA = 384
KV_LORA = 256
ROPE_BASE = 10000.0
EPS = 1e-6

RET_QK = RET_HEADS * RET_DK
RET_V = RET_HEADS * RET_DV
EVEN_IN = 2 * RET_QK + 2 * RET_V + 4 * CONV_W
EVEN_MIX = RET_V + CONV_W
MLA_QK = MLA_NOPE + MLA_ROPE
MLA_V = MLA_HEADS * MLA_DV
MLA_SCALE = MLA_QK ** -0.5
LOG2E = math.log2(math.e)
MLA_QK_PAD = 256
ODD_A = Q_LORA + KV_LORA + 128

_Q0, _K0, _V0, _GR0, _CB0, _CC0, _CX0, _GC0 = 0, 512, 1024, 2048, 3072, 4096, 5120, 6144

LANES = 128
SUBLANES = 8
ROPE_HALF = 32
HALF_TILE = LANES // 2

VMEM_LIMIT = 56 * 1024 * 1024

BF16 = jnp.bfloat16
F32 = jnp.float32


def _silu(x):
    return (0.5 * x) * (1.0 + jnp.tanh(0.5 * x))


def _rms(x, g):
    ms = jnp.mean(x * x, axis=-1, keepdims=True)
    return x * lax.rsqrt(ms + EPS) * g


def _rope_tile(x, cos, sin_signed):
    return x * cos + pltpu.roll(x, HALF_TILE, 1) * sin_signed


ROPE_GROUPS = LANES // ROPE_HALF
ROPE_BLK = 512


def _rope_table_kernel(pos_ref, inv_ref, cos_ref, sin_ref):
    ang = pos_ref[...].astype(F32) * inv_ref[...]
    c = jnp.cos(ang)
    s = jnp.sin(ang)
    lane = lax.broadcasted_iota(jnp.int32, (ROPE_BLK, LANES), 1)
    low = lane < ROPE_HALF
    for g in range(ROPE_GROUPS):
        shift = (LANES - g * ROPE_HALF) % LANES
        cg = jnp.where(low, c if shift == 0 else pltpu.roll(c, shift, 1), 0.0)
        sg = jnp.where(low, s if shift == 0 else pltpu.roll(s, shift, 1), 0.0)
        cg = cg + pltpu.roll(cg, ROPE_HALF, 1)
        sg = sg + pltpu.roll(sg, ROPE_HALF, 1)
        cos_ref[g] = cg + pltpu.roll(cg, HALF_TILE, 1)
        sin_ref[g] = pltpu.roll(sg, HALF_TILE, 1) - sg


def _rope_tables(positions):
    inv = ROPE_BASE ** (-jnp.arange(0, 2 * ROPE_HALF, 2, dtype=F32) / (2 * ROPE_HALF))
    dense_rows = ROWS // ROPE_GROUPS
    pos_dense = jnp.repeat(positions.reshape(ROPE_GROUPS, dense_rows).T, ROPE_HALF, axis=1)
    inv_dense = jnp.tile(inv, ROPE_GROUPS).reshape(1, LANES)
    out_spec = pl.BlockSpec((ROPE_GROUPS, ROPE_BLK, LANES), lambda i: (0, i, 0))
    out = jax.ShapeDtypeStruct((ROPE_GROUPS, dense_rows, LANES), F32)
    cos_t, sin_t = pl.pallas_call(
        _rope_table_kernel,
        grid=(dense_rows // ROPE_BLK,),
        in_specs=[pl.BlockSpec((ROPE_BLK, LANES), lambda i: (i, 0)),
                  pl.BlockSpec((1, LANES), lambda i: (0, 0))],
        out_specs=[out_spec, out_spec],
        out_shape=[out, out],
        name="rope_table",
    )(pos_dense, inv_dense)
    return cos_t.reshape(ROWS, LANES), sin_t.reshape(ROWS, LANES)


EVEN_IN_TM = 1024
EVEN_IN_TN = 1792
assert EVEN_IN % EVEN_IN_TN == 0 and EVEN_IN_TN >= _V0


def _even_in_kernel(x_ref, g_ref, w_ref, cos_ref, sin_ref, o_ref, u_ref):
    j = pl.program_id(1)

    @pl.when(j == 0)
    def _():
        u = _rms(x_ref[...], g_ref[...]).astype(BF16)
        u_ref[...] = u
        acc = jnp.dot(u, w_ref[...], preferred_element_type=F32)
        cos = cos_ref[...]
        sin = sin_ref[...]
        for c in range(_V0 // LANES):
            y = _rope_tile(acc[:, c * LANES:(c + 1) * LANES], cos, sin)
            if c * LANES >= _K0:
                y = y * (RET_DK ** -0.5)
            o_ref[:, c * LANES:(c + 1) * LANES] = y.astype(BF16)
        o_ref[:, _V0:] = acc[:, _V0:].astype(BF16)

    @pl.when(j != 0)
    def _():
        o_ref[...] = jnp.dot(u_ref[...], w_ref[...], preferred_element_type=F32).astype(BF16)


def _even_in(h, g, w, cos_t, sin_t):
    tm, tn = EVEN_IN_TM, EVEN_IN_TN
    return pl.pallas_call(
        _even_in_kernel,
        grid=(ROWS // tm, EVEN_IN // tn),
        in_specs=[
            pl.BlockSpec((tm, D_MODEL), lambda i, j: (i, 0)),
            pl.BlockSpec((1, D_MODEL), lambda i, j: (0, 0)),
            pl.BlockSpec((D_MODEL, tn), lambda i, j: (0, j)),
            pl.BlockSpec((tm, LANES), lambda i, j: (i, 0)),
            pl.BlockSpec((tm, LANES), lambda i, j: (i, 0)),
        ],
        out_specs=pl.BlockSpec((tm, tn), lambda i, j: (i, j)),
        out_shape=jax.ShapeDtypeStruct((ROWS, EVEN_IN), BF16),
        scratch_shapes=[pltpu.VMEM((tm, D_MODEL), BF16)],
        compiler_params=pltpu.CompilerParams(
            dimension_semantics=("arbitrary", "arbitrary"), vmem_limit_bytes=VMEM_LIMIT),
        name="even_in",
    )(h, g, w, cos_t, sin_t)


EVEN_MIX_R = 512
EVEN_MIX_STEPS = ROWS // EVEN_MIX_R
PROJ_PIECE = 256


def _even_mix_block(p_ref, h_ref, wout_ref, gn_ref, cw_ref, cb_ref, pn_ref,
                    intra_ref, xi_ref, zeta_ref, gam_ref, bm_ref,
                    o_ref, state_ref, tail_ref, mix_prev_ref, mix_ref):
    C = RET_CHUNK

    lane_k = lax.broadcasted_iota(jnp.int32, (C, LANES), 1)
    first_head = (lane_k % HALF_TILE) < ROPE_HALF
    lane_v = lax.broadcasted_iota(jnp.int32, (C, 2 * RET_DV), 1)
    zero_k = jnp.zeros((C, LANES), BF16)
    zero_v = jnp.zeros((C, 2 * RET_DV), BF16)

    def chunk_body(c, carry):
        rows = pl.ds(pl.multiple_of(c * C, C), C)

        mix_prev = mix_prev_ref[rows, :]
        y_prev = []

        def project_piece():
            n = len(y_prev)
            y_prev.append(jnp.dot(mix_prev, wout_ref[:, n * PROJ_PIECE:(n + 1) * PROJ_PIECE],
                                  preferred_element_type=F32))

        for ct in range(CONV_W // LANES):
            if ct % (CONV_W // LANES // 2) == 0:
                project_piece()
            cs = slice(ct * LANES, (ct + 1) * LANES)
            pc = lambda base: p_ref[rows, base + cs.start:base + cs.stop].astype(F32)
            tail = tail_ref[:, cs]
            prod = pc(_CC0) * pc(_CX0)
            ext = jnp.concatenate([tail, prod], axis=0)
            conv = cb_ref[:, cs] + prod * cw_ref[CONV_K - 1:CONV_K, cs]
            for tap in range(CONV_K - 1):
                off = SUBLANES - (CONV_K - 1) + tap
                conv = conv + ext[off:off + C, :] * cw_ref[tap:tap + 1, cs]
            tail_ref[:, cs] = prod[C - SUBLANES:, :]
            mix_ref[rows, RET_V + cs.start:RET_V + cs.stop] = (pc(_CB0) * conv * _silu(pc(_GC0))).astype(BF16)

        for j in range(RET_PAIRS):
            if len(y_prev) < D_MODEL // PROJ_PIECE:
                project_piece()
            qp = p_ref[rows, _Q0 + j * LANES:_Q0 + (j + 1) * LANES]
            kp = p_ref[rows, _K0 + j * LANES:_K0 + (j + 1) * LANES]
            vp = p_ref[rows, _V0 + j * 2 * RET_DV:_V0 + (j + 1) * 2 * RET_DV]
            kstack = jnp.concatenate(
                [jnp.where(first_head, kp, zero_k), jnp.where(first_head, zero_k, kp)], axis=0)
            sc = lax.dot_general(qp, kstack, (((1,), (1,)), ((), ())), preferred_element_type=F32)
            pm = (sc * intra_ref[j]).astype(BF16)
            vblk = jnp.concatenate(
                [jnp.where(lane_v < RET_DV, vp, zero_v), jnp.where(lane_v >= RET_DV, vp, zero_v)], axis=0)
            st = state_ref[j]
            r = (jnp.dot(pm, vblk, preferred_element_type=F32)
                 + jnp.dot(qp, st.astype(BF16), preferred_element_type=F32) * xi_ref[j])
            kz = (kp.astype(F32) * zeta_ref[j]).astype(BF16)
            kv = lax.dot_general(kz, vp, (((0,), (0,)), ((), ())), preferred_element_type=F32)
            state_ref[j] = st * gam_ref[j] + kv * bm_ref[...]
            for hh in range(2):
                hd = 2 * j + hh
                cols = slice(hd * RET_DV, (hd + 1) * RET_DV)
                y = _rms(r[:, hh * RET_DV:(hh + 1) * RET_DV], gn_ref[:, cols])
                gate = p_ref[rows, _GR0 + hd * RET_DV:_GR0 + (hd + 1) * RET_DV].astype(F32)
                mix_ref[rows, cols] = (y * _silu(gate)).astype(BF16)

        o_ref[rows, :] = h_ref[rows, :] + _rms(jnp.concatenate(y_prev, axis=1), pn_ref[...])
        return carry

    lax.fori_loop(0, EVEN_MIX_R // C, chunk_body, 0)


def _even_mix_kernel(p_ref, h_ref, wout_ref, gn_ref, cw_ref, cb_ref, pn_ref,
                     intra_ref, xi_ref, zeta_ref, gam_ref, bm_ref,
                     o_ref, state_ref, tail_ref, mix_a_ref, mix_b_ref):
    t = pl.program_id(0)
    args = (p_ref, h_ref, wout_ref, gn_ref, cw_ref, cb_ref, pn_ref,
            intra_ref, xi_ref, zeta_ref, gam_ref, bm_ref, o_ref, state_ref, tail_ref)

    @pl.when(t % (SEQ // EVEN_MIX_R) == 0)
    def _():
        state_ref[...] = jnp.zeros_like(state_ref)
        tail_ref[...] = jnp.zeros_like(tail_ref)

    @pl.when(t == 0)
    def _():
        mix_b_ref[...] = jnp.zeros_like(mix_b_ref)

    @pl.when(t % 2 == 0)
    def _():
        _even_mix_block(*args, mix_b_ref, mix_a_ref)

    @pl.when(t % 2 == 1)
    def _():
        _even_mix_block(*args, mix_a_ref, mix_b_ref)


def _retention_tables():
    h, c = RET_HEADS, RET_CHUNK
    log_gamma = np.log1p(-np.exp2(-5.0 - np.arange(h, dtype=np.float64)))
    i = np.arange(c, dtype=np.float64)
    rel = i[:, None] - i[None, :]
    intra = np.where(rel >= 0, np.exp(log_gamma[:, None, None] * np.maximum(rel, 0.0)), 0.0)
    xi = np.exp(log_gamma[:, None] * (i + 1.0))
    zeta = np.exp(log_gamma[:, None] * (c - 1.0 - i))
    gamma_c = np.exp(log_gamma * c)
    pair = np.arange(RET_PAIRS)
    intra_p = intra.reshape(RET_PAIRS, 2, c, c).transpose(0, 2, 1, 3).reshape(RET_PAIRS, c, 2 * c)
    xi_p = np.repeat(xi.reshape(RET_PAIRS, 2, c).transpose(0, 2, 1), RET_DV, axis=2)
    second = ((np.arange(2 * RET_DK) % HALF_TILE) >= ROPE_HALF).astype(np.int64)
    zeta_p = zeta.reshape(RET_PAIRS, 2, c)[pair[:, None, None], second[None, None, :], np.arange(c)[None, :, None]]
    gam_rows = gamma_c.reshape(RET_PAIRS, 2)[:, second]
    gam_p = np.broadcast_to(gam_rows[:, :, None], (RET_PAIRS, 2 * RET_DK, 2 * RET_DV))
    col_head = np.arange(2 * RET_DV) // RET_DV
    bm = (second[:, None] == col_head[None, :])
    as_f32 = lambda a: jnp.asarray(np.ascontiguousarray(a, dtype=np.float32))
    return as_f32(intra_p), as_f32(xi_p), as_f32(zeta_p), as_f32(gam_p), as_f32(bm)


def _even_mix(proj, h, w_out, gn, conv_w, conv_b, post_g, tables):
    R = EVEN_MIX_R
    nt = EVEN_MIX_STEPS
    intra_p, xi_p, zeta_p, gam_p, bm = tables
    mix_map = lambda t: (jnp.minimum(t, nt - 1), 0)
    row_map = lambda t: (jnp.maximum(t - 1, 0), 0)
    const2 = lambda t: (0, 0)
    const3 = lambda t: (0, 0, 0)
    return pl.pallas_call(
        _even_mix_kernel,
        grid=(nt + 1,),
        in_specs=[
            pl.BlockSpec((R, EVEN_IN), mix_map),
            pl.BlockSpec((R, D_MODEL), row_map),
            pl.BlockSpec((EVEN_MIX, D_MODEL), const2),
            pl.BlockSpec((1, RET_V), const2),
            pl.BlockSpec((CONV_K, CONV_W), const2),
            pl.BlockSpec((1, CONV_W), const2),
            pl.BlockSpec((1, D_MODEL), const2),
            pl.BlockSpec((RET_PAIRS, RET_CHUNK, 2 * RET_CHUNK), const3),
            pl.BlockSpec((RET_PAIRS, RET_CHUNK, 2 * RET_DV), const3),
            pl.BlockSpec((RET_PAIRS, RET_CHUNK, 2 * RET_DK), const3),
            pl.BlockSpec((RET_PAIRS, 2 * RET_DK, 2 * RET_DV), const3),
            pl.BlockSpec((2 * RET_DK, 2 * RET_DV), const2),
        ],
        out_specs=pl.BlockSpec((R, D_MODEL), row_map),
        out_shape=jax.ShapeDtypeStruct((ROWS, D_MODEL), F32),
        scratch_shapes=[
            pltpu.VMEM((RET_PAIRS, 2 * RET_DK, 2 * RET_DV), F32),
            pltpu.VMEM((SUBLANES, CONV_W), F32),
            pltpu.VMEM((R, EVEN_MIX), BF16),
            pltpu.VMEM((R, EVEN_MIX), BF16),
        ],
        compiler_params=pltpu.CompilerParams(
            dimension_semantics=("arbitrary",), vmem_limit_bytes=VMEM_LIMIT),
        name="even_mix",
    )(proj, h, w_out, gn, conv_w, conv_b, post_g, intra_p, xi_p, zeta_p, gam_p, bm)


ODD_R = 512


def _odd_in_kernel(x_ref, g_ref, wa_ref, wg_ref, qn_ref, wq_ref, kvn_ref, wkv_ref, cos_ref, sin_ref,
                   q_ref, k_ref, vt_ref, sg_ref):
    u = _rms(x_ref[...], g_ref[...]).astype(BF16)
    a = jnp.dot(u, wa_ref[...], preferred_element_type=F32)
    cq = a[:, :Q_LORA]
    ckv = a[:, Q_LORA:Q_LORA + KV_LORA]
    kr = a[:, Q_LORA + KV_LORA:]
    q = jnp.dot(_rms(cq, qn_ref[...]).astype(BF16), wq_ref[...], preferred_element_type=F32) * (MLA_SCALE * LOG2E)
    kv = jnp.dot(_rms(ckv, kvn_ref[...]).astype(BF16), wkv_ref[...], preferred_element_type=F32)
    cos = cos_ref[...]
    sin = sin_ref[...]
    kr_rot = _rope_tile(kr, cos, sin).astype(BF16)
    for hd in range(MLA_HEADS):
        lo, hi = hd * LANES, (hd + 1) * LANES
        q_ref[0, hd, :, 0:LANES] = q[:, lo:hi].astype(BF16)
        q_ref[0, hd, :, LANES:2 * LANES] = _rope_tile(q[:, MLA_V + lo:MLA_V + hi], cos, sin).astype(BF16)
        k_ref[0, hd, :, 0:LANES] = kv[:, lo:hi].astype(BF16)
        k_ref[0, hd, :, LANES:2 * LANES] = kr_rot
        vt_ref[0, hd, 0] = kv[:, MLA_V + lo:MLA_V + hi].T.astype(BF16)
    gate = jnp.dot(u, wg_ref[...], preferred_element_type=F32)
    sg_ref[...] = _silu(gate).astype(BF16)


def _odd_in(h, g, wa, wg, qn, wq, kvn, wkv, cos_t, sin_t):
    R = ODD_R
    nb = SEQ // R
    row_map = lambda b, i: (b * nb + i, 0)
    const2 = lambda b, i: (0, 0)
    return pl.pallas_call(
        _odd_in_kernel,
        grid=(BATCH, nb),
        in_specs=[
            pl.BlockSpec((R, D_MODEL), row_map),
            pl.BlockSpec((1, D_MODEL), const2),
            pl.BlockSpec((D_MODEL, ODD_A), const2),
            pl.BlockSpec((D_MODEL, MLA_V), const2),
            pl.BlockSpec((1, Q_LORA), const2),
            pl.BlockSpec((Q_LORA, 2 * MLA_V), const2),
            pl.BlockSpec((1, KV_LORA), const2),
            pl.BlockSpec((KV_LORA, 2 * MLA_V), const2),
            pl.BlockSpec((R, LANES), row_map),
            pl.BlockSpec((R, LANES), row_map),
        ],
        out_specs=[
            pl.BlockSpec((1, MLA_HEADS, R, MLA_QK_PAD), lambda b, i: (b, 0, i, 0)),
            pl.BlockSpec((1, MLA_HEADS, R, MLA_QK_PAD), lambda b, i: (b, 0, i, 0)),
            pl.BlockSpec((1, MLA_HEADS, 1, MLA_DV, R), lambda b, i: (b, 0, i, 0, 0)),
            pl.BlockSpec((R, MLA_V), row_map),
        ],
        out_shape=[
            jax.ShapeDtypeStruct((BATCH, MLA_HEADS, SEQ, MLA_QK_PAD), BF16),
            jax.ShapeDtypeStruct((BATCH, MLA_HEADS, SEQ, MLA_QK_PAD), BF16),
            jax.ShapeDtypeStruct((BATCH, MLA_HEADS, nb, MLA_DV, R), BF16),
            jax.ShapeDtypeStruct((ROWS, MLA_V), BF16),
        ],
        compiler_params=pltpu.CompilerParams(
            dimension_semantics=("arbitrary", "arbitrary"), vmem_limit_bytes=VMEM_LIMIT),
        name="odd_in",
    )(h, g, wa, wg, qn, wq, kvn, wkv, cos_t, sin_t)


ATTN_T = ODD_R
ATTN_HG = 4


def _attn_kernel(q_ref, qn_ref, k_ref, vt_ref, sg_ref, o_ref, m_ref, l_ref, acc_ref, st_ref):
    T = ATTN_T
    qi = pl.program_id(2)
    m_ref[...] = jnp.full(m_ref.shape, -jnp.inf, F32)
    l_ref[...] = jnp.zeros(l_ref.shape, F32)
    acc_ref[...] = jnp.zeros(acc_ref.shape, F32)

    def scores(hh, j, queries=q_ref):
        k = k_ref[0, hh, pl.ds(pl.multiple_of(j * T, T), T), :]
        return lax.dot_general(k, queries[0, hh], (((1,), (1,)), ((), ())), preferred_element_type=F32)

    def consume(hh, j, st):
        m_old = m_ref[hh]
        m_new = jnp.maximum(m_old, jnp.max(st, axis=0, keepdims=True))
        alpha = jnp.exp2(m_old - m_new)
        p = jnp.exp2(st - m_new)
        l_ref[hh] = alpha * l_ref[hh] + jnp.sum(p, axis=0, keepdims=True)
        acc_ref[hh] = alpha * acc_ref[hh] + jnp.dot(
            vt_ref[0, hh, j], p.astype(BF16), preferred_element_type=F32)
        m_ref[hh] = m_new

    @pl.when(qi == 0)
    def _():
        for hh in range(ATTN_HG):
            st_ref[hh] = scores(hh, 0)

    def body(j, carry):
        for hh in range(ATTN_HG):
            st = st_ref[hh]
            st_ref[hh] = scores(hh, j + 1)
            consume(hh, j, st)
        return carry

    lax.fori_loop(0, qi, body, 0)
    kpos = lax.broadcasted_iota(jnp.int32, (T, T), 0)
    qpos = lax.broadcasted_iota(jnp.int32, (T, T), 1)
    for hh in range(ATTN_HG):
        st = st_ref[hh]
        st_ref[hh] = scores(hh, 0, qn_ref)
        consume(hh, qi, jnp.where(kpos <= qpos, st, -jnp.inf))
    for hh in range(ATTN_HG):
        cols = slice(hh * MLA_DV, (hh + 1) * MLA_DV)
        o = (acc_ref[hh] * (1.0 / l_ref[hh])).T
        o_ref[0, :, cols] = (o * sg_ref[0, :, cols].astype(F32)).astype(BF16)


def _attn(q_cat, k_cat, vt, sg):
    T = ATTN_T
    HG = ATTN_HG
    nq = SEQ // T
    return pl.pallas_call(
        _attn_kernel,
        grid=(BATCH, MLA_HEADS // HG, nq),
        in_specs=[
            pl.BlockSpec((1, HG, T, MLA_QK_PAD), lambda b, h, i: (b, h, i, 0)),
            pl.BlockSpec((1, HG, T, MLA_QK_PAD), lambda b, h, i: (b, h, jnp.minimum(i + 1, nq - 1), 0)),
            pl.BlockSpec((1, HG, SEQ, MLA_QK_PAD), lambda b, h, i: (b, h, 0, 0)),
            pl.BlockSpec((1, HG, nq, MLA_DV, T), lambda b, h, i: (b, h, 0, 0, 0)),
            pl.BlockSpec((1, T, HG * MLA_DV), lambda b, h, i: (b, i, h)),
        ],
        out_specs=pl.BlockSpec((1, T, HG * MLA_DV), lambda b, h, i: (b, i, h)),
        out_shape=jax.ShapeDtypeStruct((BATCH, SEQ, MLA_V), BF16),
        scratch_shapes=[
            pltpu.VMEM((HG, 1, T), F32),
            pltpu.VMEM((HG, 1, T), F32),
            pltpu.VMEM((HG, MLA_DV, T), F32),
            pltpu.VMEM((HG, T, T), F32),
        ],
        compiler_params=pltpu.CompilerParams(
            dimension_semantics=("arbitrary", "arbitrary", "arbitrary"), vmem_limit_bytes=VMEM_LIMIT),
        name="odd_attn",
    )(q_cat, q_cat, k_cat, vt, sg.reshape(BATCH, SEQ, MLA_V))


ODD_OUT_R = 1024


def _odd_out_kernel(og_ref, h_ref, w_ref, pn_ref, out_ref):
    y = jnp.dot(og_ref[...], w_ref[...], preferred_element_type=F32)
    out_ref[...] = h_ref[...] + _rms(y, pn_ref[...])


def _odd_out(og, h, w, post_g):
    R = ODD_OUT_R
    row_map = lambda i: (i, 0)
    const2 = lambda i: (0, 0)
    return pl.pallas_call(
        _odd_out_kernel,
        grid=(ROWS // R,),
        in_specs=[
            pl.BlockSpec((R, MLA_V), row_map),
            pl.BlockSpec((R, D_MODEL), row_map),
            pl.BlockSpec((MLA_V, D_MODEL), const2),
            pl.BlockSpec((1, D_MODEL), const2),
        ],
        out_specs=pl.BlockSpec((R, D_MODEL), row_map),
        out_shape=jax.ShapeDtypeStruct((ROWS, D_MODEL), F32),
        compiler_params=pltpu.CompilerParams(
            dimension_semantics=("arbitrary",), vmem_limit_bytes=VMEM_LIMIT),
        name="odd_out",
    )(og, h, w, post_g)


def _prep_even_w_in(w):
    qk = w[:, :_V0].reshape(D_MODEL, 2 * RET_PAIRS, 2, 2, ROPE_HALF)
    qk = qk.transpose(0, 1, 3, 2, 4).reshape(D_MODEL, _V0)
    return jnp.concatenate([qk, w[:, _V0:]], axis=1).astype(BF16)


def _rope_cols(w):
    z = jnp.zeros(w.shape[:-1] + (ROPE_HALF,), w.dtype)
    return jnp.concatenate([w[..., :ROPE_HALF], z, w[..., ROPE_HALF:], z], axis=-1)


def _prep_odd_weights(w_in, w_qb, w_kvb):
    lat = Q_LORA + KV_LORA
    wa = jnp.concatenate([w_in[:, :lat], _rope_cols(w_in[:, lat:lat + MLA_ROPE])], axis=1).astype(BF16)
    wg = w_in[:, lat + MLA_ROPE:].astype(BF16)
    wq3 = w_qb.reshape(Q_LORA, MLA_HEADS, MLA_QK)
    wq_nope = wq3[:, :, :MLA_NOPE].reshape(Q_LORA, MLA_HEADS * MLA_NOPE)
    wq_rope = _rope_cols(wq3[:, :, MLA_NOPE:]).reshape(Q_LORA, MLA_HEADS * LANES)
    wq = jnp.concatenate([wq_nope, wq_rope], axis=1).astype(BF16)
    wkv3 = w_kvb.reshape(KV_LORA, MLA_HEADS, MLA_NOPE + MLA_DV)
    wkv = jnp.concatenate(
        [wkv3[:, :, :MLA_NOPE].reshape(KV_LORA, MLA_HEADS * MLA_NOPE),
         wkv3[:, :, MLA_NOPE:].reshape(KV_LORA, MLA_V)], axis=1).astype(BF16)
    return wa, wg, wq, wkv


def kernel(x, positions, pre_norm, post_norm, even_w_in, even_conv_w, even_conv_b, ret_gn, even_w_out,
           odd_w_in, q_a_norm, w_qb, kv_a_norm, w_kvb, odd_w_out):
    cos_t, sin_t = _rope_tables(positions)
    tables = _retention_tables()
    h = x.reshape(ROWS, D_MODEL)
    for layer in range(DEPTH):
        pre_g = pre_norm[layer].reshape(1, D_MODEL)
        post_g = post_norm[layer].reshape(1, D_MODEL)
        if layer % 2 == 0:
            e = layer // 2
            proj = _even_in(h, pre_g, _prep_even_w_in(even_w_in[e]), cos_t, sin_t)
            h = _even_mix(proj, h, even_w_out[e].astype(BF16), ret_gn[e].reshape(1, RET_V),
                          even_conv_w[e], even_conv_b[e].reshape(1, CONV_W), post_g, tables)
        else:
            o = layer // 2
            wa, wg, wq, wkv = _prep_odd_weights(odd_w_in[o], w_qb[o], w_kvb[o])
            q_cat, k_cat, vt, sg = _odd_in(h, pre_g, wa, wg, q_a_norm[o].reshape(1, Q_LORA), wq,
                                           kv_a_norm[o].reshape(1, KV_LORA), wkv, cos_t, sin_t)
            og = _attn(q_cat, k_cat, vt, sg)
            h = _odd_out(og.reshape(ROWS, MLA_V), h, odd_w_out[o].astype(BF16), post_g)
    return h.reshape(BATCH, SEQ, D_MODEL)
```

```python
import math

import jax
import jax.numpy as jnp
import numpy as np
from jax import lax
from jax.experimental import pallas as pl
from jax.experimental.pallas import tpu as pltpu

D_MODEL = 1024
BATCH = 4
SEQ = 4096
DEPTH = 4
ROWS = BATCH * SEQ

RET_HEADS = 8
RET_DK = 64
RET_DV = 128
RET_CHUNK = 128
RET_PAIRS = RET_HEADS // 2
CONV_W = 1024
CONV_K = 3
MLA_HEADS = 8
MLA_NOPE = 128
MLA_ROPE = 64
MLA_DV = 128
Q_LORA = 384
KV_LORA = 256
ROPE_BASE = 10000.0
EPS = 1e-6

RET_QK = RET_HEADS * RET_DK
RET_V = RET_HEADS * RET_DV
EVEN_IN = 2 * RET_QK + 2 * RET_V + 4 * CONV_W
EVEN_MIX = RET_V + CONV_W
MLA_QK = MLA_NOPE + MLA_ROPE
MLA_V = MLA_HEADS * MLA_DV
MLA_SCALE = MLA_QK ** -0.5
LOG2E = math.log2(math.e)
MLA_QK_PAD = 256
ODD_A = Q_LORA + KV_LORA + 128

_Q0, _K0, _V0, _GR0, _CB0, _CC0, _CX0, _GC0 = 0, 512, 1024, 2048, 3072, 4096, 5120, 6144

LANES = 128
SUBLANES = 8
ROPE_HALF = 32
HALF_TILE = LANES // 2

VMEM_LIMIT = 56 * 1024 * 1024

BF16 = jnp.bfloat16
F32 = jnp.float32


def _silu(x):
    return (0.5 * x) * (1.0 + jnp.tanh(0.5 * x))


def _rms(x, g):
    ms = jnp.mean(x * x, axis=-1, keepdims=True)
    return x * lax.rsqrt(ms + EPS) * g


def _rope_tile(x, cos, sin_signed):
    return x * cos + pltpu.roll(x, HALF_TILE, 1) * sin_signed


ROPE_GROUPS = LANES // ROPE_HALF
ROPE_BLK = 512


def _rope_table_kernel(pos_ref, inv_ref, cos_ref, sin_ref):
    ang = pos_ref[...].astype(F32) * inv_ref[...]
    c = jnp.cos(ang)
    s = jnp.sin(ang)
    lane = lax.broadcasted_iota(jnp.int32, (ROPE_BLK, LANES), 1)
    low = lane < ROPE_HALF
    for g in range(ROPE_GROUPS):
        shift = (LANES - g * ROPE_HALF) % LANES
        cg = jnp.where(low, c if shift == 0 else pltpu.roll(c, shift, 1), 0.0)
        sg = jnp.where(low, s if shift == 0 else pltpu.roll(s, shift, 1), 0.0)
        cg = cg + pltpu.roll(cg, ROPE_HALF, 1)
        sg = sg + pltpu.roll(sg, ROPE_HALF, 1)
        cos_ref[g] = cg + pltpu.roll(cg, HALF_TILE, 1)
        sin_ref[g] = pltpu.roll(sg, HALF_TILE, 1) - sg


def _rope_tables(positions):
    inv = ROPE_BASE ** (-jnp.arange(0, 2 * ROPE_HALF, 2, dtype=F32) / (2 * ROPE_HALF))
    dense_rows = ROWS // ROPE_GROUPS
    pos_dense = jnp.repeat(positions.reshape(ROPE_GROUPS, dense_rows).T, ROPE_HALF, axis=1)
    inv_dense = jnp.tile(inv, ROPE_GROUPS).reshape(1, LANES)
    out_spec = pl.BlockSpec((ROPE_GROUPS, ROPE_BLK, LANES), lambda i: (0, i, 0))
    out = jax.ShapeDtypeStruct((ROPE_GROUPS, dense_rows, LANES), F32)
    cos_t, sin_t = pl.pallas_call(
        _rope_table_kernel,
        grid=(dense_rows // ROPE_BLK,),
        in_specs=[pl.BlockSpec((ROPE_BLK, LANES), lambda i: (i, 0)),
                  pl.BlockSpec((1, LANES), lambda i: (0, 0))],
        out_specs=[out_spec, out_spec],
        out_shape=[out, out],
        name="rope_table",
    )(pos_dense, inv_dense)
    return cos_t.reshape(ROWS, LANES), sin_t.reshape(ROWS, LANES)


EVEN_IN_TM = 1024
EVEN_IN_TN = 1024
EVEN_QKVG = 3 * EVEN_IN_TN
CONV_BLK = EVEN_IN_TN // 4
CONV_BLOCKS = CONV_W // CONV_BLK
EVEN_IN_STRIP = 256
CONV_STRIP = 128


def _even_in_kernel(x_ref, g_ref, w_ref, cos_ref, sin_ref, cw_ref, cb_ref,
                    qkvg_ref, cmix_ref, u_ref, tail_ref):
    i = pl.program_id(0)
    j = pl.program_id(1)
    strips = [slice(r * EVEN_IN_STRIP, (r + 1) * EVEN_IN_STRIP) for r in range(EVEN_IN_TM // EVEN_IN_STRIP)]

    halves = [slice(0, EVEN_IN_TN // 2), slice(EVEN_IN_TN // 2, EVEN_IN_TN)]

    @pl.when(j == 0)
    def _():
        for rs in strips:
            u = _rms(x_ref[rs, :], g_ref[...]).astype(BF16)
            u_ref[rs, :] = u
            cos = cos_ref[rs, :]
            sin = sin_ref[rs, :]
            for ns in halves:
                acc = jnp.dot(u, w_ref[:, ns], preferred_element_type=F32)
                for c in range((ns.stop - ns.start) // LANES):
                    y = _rope_tile(acc[:, c * LANES:(c + 1) * LANES], cos, sin)
                    if ns.start >= _K0:
                        y = y * (RET_DK ** -0.5)
                    qkvg_ref[rs, ns.start + c * LANES:ns.start + (c + 1) * LANES] = y.astype(BF16)

    @pl.when(j == 1)
    def _():
        qkvg_ref[...] = jnp.dot(u_ref[...], w_ref[...], preferred_element_type=F32).astype(BF16)

    @pl.when(j == 2)
    def _():
        for rs in strips:
            for ns in halves:
                acc = jnp.dot(u_ref[rs, :], w_ref[:, ns], preferred_element_type=F32)
                qkvg_ref[rs, ns] = _silu(acc).astype(BF16)

    @pl.when(j >= 3)
    def _():
        slot = j - 3
        sequence_start = (i % (SEQ // EVEN_IN_TM)) == 0
        lane_tiles = [slice(ct * LANES, (ct + 1) * LANES) for ct in range(CONV_BLK // LANES)]
        halos = [jnp.where(sequence_start, 0.0, tail_ref[slot, :, cs]) for cs in lane_tiles]
        for rs in strips:
            for ct, cs in enumerate(lane_tiles):
                acc = jnp.dot(u_ref[rs, :], w_ref[:, ct * 4 * LANES:(ct + 1) * 4 * LANES],
                              preferred_element_type=F32)
                for r in range(EVEN_IN_STRIP // CONV_STRIP):
                    rr = slice(r * CONV_STRIP, (r + 1) * CONV_STRIP)
                    part = lambda k: acc[rr, k * LANES:(k + 1) * LANES]
                    prod = part(1) * part(2)
                    ext = jnp.concatenate([halos[ct], prod], axis=0)
                    conv = cb_ref[:, cs] + prod * cw_ref[CONV_K - 1:CONV_K, cs]
                    for tap in range(CONV_K - 1):
                        off = SUBLANES - (CONV_K - 1) + tap
                        conv = conv + ext[off:off + CONV_STRIP, :] * cw_ref[tap:tap + 1, cs]
                    halos[ct] = prod[CONV_STRIP - SUBLANES:, :]
                    out_rows = slice(rs.start + rr.start, rs.start + rr.stop)
                    cmix_ref[out_rows, cs] = (part(0) * conv * _silu(part(3))).astype(BF16)
        for ct, cs in enumerate(lane_tiles):
            tail_ref[slot, :, cs] = halos[ct]


def _even_in(h, g, w, cos_t, sin_t, conv_w, conv_b):
    tm, tn = EVEN_IN_TM, EVEN_IN_TN
    qkvg_blocks = EVEN_QKVG // tn
    conv_map = lambda i, j: (0, jnp.maximum(j - qkvg_blocks, 0))
    return pl.pallas_call(
        _even_in_kernel,
        grid=(ROWS // tm, EVEN_IN // tn),
        in_specs=[
            pl.BlockSpec((tm, D_MODEL), lambda i, j: (i, 0)),
            pl.BlockSpec((1, D_MODEL), lambda i, j: (0, 0)),
            pl.BlockSpec((D_MODEL, tn), lambda i, j: (0, j)),
            pl.BlockSpec((tm, LANES), lambda i, j: (i, 0)),
            pl.BlockSpec((tm, LANES), lambda i, j: (i, 0)),
            pl.BlockSpec((CONV_K, CONV_BLK), conv_map),
            pl.BlockSpec((1, CONV_BLK), conv_map),
        ],
        out_specs=[
            pl.BlockSpec((tm, tn), lambda i, j: (i, jnp.minimum(j, qkvg_blocks - 1))),
            pl.BlockSpec((tm, CONV_BLK), lambda i, j: (i, jnp.maximum(j - qkvg_blocks, 0))),
        ],
        out_shape=[
            jax.ShapeDtypeStruct((ROWS, EVEN_QKVG), BF16),
            jax.ShapeDtypeStruct((ROWS, CONV_W), BF16),
        ],
        scratch_shapes=[
            pltpu.VMEM((tm, D_MODEL), BF16),
            pltpu.VMEM((CONV_BLOCKS, SUBLANES, CONV_BLK), F32),
        ],
        compiler_params=pltpu.CompilerParams(
            dimension_semantics=("arbitrary", "arbitrary"), vmem_limit_bytes=VMEM_LIMIT),
        name="even_in",
    )(h, g, w, cos_t, sin_t, conv_w, conv_b)


EVEN_MIX_R = 512
EVEN_MIX_STEPS = ROWS // EVEN_MIX_R
PROJ_PIECE = 256


def _even_mix_block(p_ref, cmix_ref, h_ref, wout_ref, gn_ref, pn_ref,
                    intra_ref, xi_ref, zeta_ref, gam_ref, bm_ref,
                    o_ref, state_ref, mix_prev_ref, mix_ref):
    C = RET_CHUNK

    lane_k = lax.broadcasted_iota(jnp.int32, (C, LANES), 1)
    first_head = (lane_k % HALF_TILE) < ROPE_HALF
    lane_v = lax.broadcasted_iota(jnp.int32, (C, 2 * RET_DV), 1)
    zero_k = jnp.zeros((C, LANES), BF16)
    zero_v = jnp.zeros((C, 2 * RET_DV), BF16)

    def chunk_body(c, carry):
        rows = pl.ds(pl.multiple_of(c * C, C), C)

        mix_prev = mix_prev_ref[rows, :]
        cmix_prev = cmix_ref[rows, :]
        y_prev = []

        def project_piece():
            ns = slice(len(y_prev) * PROJ_PIECE, (len(y_prev) + 1) * PROJ_PIECE)
            y_prev.append(jnp.dot(mix_prev, wout_ref[:RET_V, ns], preferred_element_type=F32)
                          + jnp.dot(cmix_prev, wout_ref[RET_V:, ns], preferred_element_type=F32))

        for j in range(RET_PAIRS):
            project_piece()
            qp = p_ref[rows, _Q0 + j * LANES:_Q0 + (j + 1) * LANES]
            kp = p_ref[rows, _K0 + j * LANES:_K0 + (j + 1) * LANES]
            vp = p_ref[rows, _V0 + j * 2 * RET_DV:_V0 + (j + 1) * 2 * RET_DV]
            kstack = jnp.concatenate(
                [jnp.where(first_head, kp, zero_k), jnp.where(first_head, zero_k, kp)], axis=0)
            sc = lax.dot_general(qp, kstack, (((1,), (1,)), ((), ())), preferred_element_type=F32)
            pm = (sc * intra_ref[j]).astype(BF16)
            vblk = jnp.concatenate(
                [jnp.where(lane_v < RET_DV, vp, zero_v), jnp.where(lane_v >= RET_DV, vp, zero_v)], axis=0)
            st = state_ref[j]
            r = (jnp.dot(pm, vblk, preferred_element_type=F32)
                 + jnp.dot(qp, st.astype(BF16), preferred_element_type=F32) * xi_ref[j])
            kz = (kp.astype(F32) * zeta_ref[j]).astype(BF16)
            kv = lax.dot_general(kz, vp, (((0,), (0,)), ((), ())), preferred_element_type=F32)
            state_ref[j] = st * gam_ref[j] + kv * bm_ref[...]
            for hh in range(2):
                hd = 2 * j + hh
                cols = slice(hd * RET_DV, (hd + 1) * RET_DV)
                y = _rms(r[:, hh * RET_DV:(hh + 1) * RET_DV], gn_ref[:, cols])
                gate = p_ref[rows, _GR0 + hd * RET_DV:_GR0 + (hd + 1) * RET_DV].astype(F32)
                mix_ref[rows, cols] = (y * gate).astype(BF16)

        o_ref[rows, :] = h_ref[rows, :] + _rms(jnp.concatenate(y_prev, axis=1), pn_ref[...])
        return carry

    lax.fori_loop(0, EVEN_MIX_R // C, chunk_body, 0)


def _even_mix_kernel(p_ref, cmix_ref, h_ref, wout_ref, gn_ref, pn_ref,
                     intra_ref, xi_ref, zeta_ref, gam_ref, bm_ref,
                     o_ref, state_ref, mix_a_ref, mix_b_ref):
    t = pl.program_id(0)
    args = (p_ref, cmix_ref, h_ref, wout_ref, gn_ref, pn_ref,
            intra_ref, xi_ref, zeta_ref, gam_ref, bm_ref, o_ref, state_ref)

    @pl.when(t % (SEQ // EVEN_MIX_R) == 0)
    def _():
        state_ref[...] = jnp.zeros_like(state_ref)

    @pl.when(t == 0)
    def _():
        mix_b_ref[...] = jnp.zeros_like(mix_b_ref)

    @pl.when(t % 2 == 0)
    def _():
        _even_mix_block(*args, mix_b_ref, mix_a_ref)

    @pl.when(t % 2 == 1)
    def _():
        _even_mix_block(*args, mix_a_ref, mix_b_ref)


def _retention_tables():
    h, c = RET_HEADS, RET_CHUNK
    log_gamma = np.log1p(-np.exp2(-5.0 - np.arange(h, dtype=np.float64)))
    i = np.arange(c, dtype=np.float64)
    rel = i[:, None] - i[None, :]
    intra = np.where(rel >= 0, np.exp(log_gamma[:, None, None] * np.maximum(rel, 0.0)), 0.0)
    xi = np.exp(log_gamma[:, None] * (i + 1.0))
    zeta = np.exp(log_gamma[:, None] * (c - 1.0 - i))
    gamma_c = np.exp(log_gamma * c)
    pair = np.arange(RET_PAIRS)
    intra_p = intra.reshape(RET_PAIRS, 2, c, c).transpose(0, 2, 1, 3).reshape(RET_PAIRS, c, 2 * c)
    xi_p = np.repeat(xi.reshape(RET_PAIRS, 2, c).transpose(0, 2, 1), RET_DV, axis=2)
    second = ((np.arange(2 * RET_DK) % HALF_TILE) >= ROPE_HALF).astype(np.int64)
    zeta_p = zeta.reshape(RET_PAIRS, 2, c)[pair[:, None, None], second[None, None, :], np.arange(c)[None, :, None]]
    gam_rows = gamma_c.reshape(RET_PAIRS, 2)[:, second]
    gam_p = np.broadcast_to(gam_rows[:, :, None], (RET_PAIRS, 2 * RET_DK, 2 * RET_DV))
    col_head = np.arange(2 * RET_DV) // RET_DV
    bm = (second[:, None] == col_head[None, :])
    as_f32 = lambda a: jnp.asarray(np.ascontiguousarray(a, dtype=np.float32))
    return as_f32(intra_p), as_f32(xi_p), as_f32(zeta_p), as_f32(gam_p), as_f32(bm)


def _even_mix(qkvg, cmix, h, w_out, gn, post_g, tables):
    R = EVEN_MIX_R
    nt = EVEN_MIX_STEPS
    intra_p, xi_p, zeta_p, gam_p, bm = tables
    mix_map = lambda t: (jnp.minimum(t, nt - 1), 0)
    row_map = lambda t: (jnp.maximum(t - 1, 0), 0)
    const2 = lambda t: (0, 0)
    const3 = lambda t: (0, 0, 0)
    return pl.pallas_call(
        _even_mix_kernel,
        grid=(nt + 1,),
        in_specs=[
            pl.BlockSpec((R, EVEN_QKVG), mix_map),
            pl.BlockSpec((R, CONV_W), row_map),
            pl.BlockSpec((R, D_MODEL), row_map),
            pl.BlockSpec((EVEN_MIX, D_MODEL), const2),
            pl.BlockSpec((1, RET_V), const2),
            pl.BlockSpec((1, D_MODEL), const2),
            pl.BlockSpec((RET_PAIRS, RET_CHUNK, 2 * RET_CHUNK), const3),
            pl.BlockSpec((RET_PAIRS, RET_CHUNK, 2 * RET_DV), const3),
            pl.BlockSpec((RET_PAIRS, RET_CHUNK, 2 * RET_DK), const3),
            pl.BlockSpec((RET_PAIRS, 2 * RET_DK, 2 * RET_DV), const3),
            pl.BlockSpec((2 * RET_DK, 2 * RET_DV), const2),
        ],
        out_specs=pl.BlockSpec((R, D_MODEL), row_map),
        out_shape=jax.ShapeDtypeStruct((ROWS, D_MODEL), F32),
        scratch_shapes=[
            pltpu.VMEM((RET_PAIRS, 2 * RET_DK, 2 * RET_DV), F32),
            pltpu.VMEM((R, RET_V), BF16),
            pltpu.VMEM((R, RET_V), BF16),
        ],
        compiler_params=pltpu.CompilerParams(
            dimension_semantics=("arbitrary",), vmem_limit_bytes=VMEM_LIMIT),
        name="even_mix",
    )(qkvg, cmix, h, w_out, gn, post_g, intra_p, xi_p, zeta_p, gam_p, bm)


ODD_R = 512


def _odd_in_kernel(x_ref, g_ref, wa_ref, wg_ref, qn_ref, wq_ref, kvn_ref, wkv_ref, cos_ref, sin_ref,
                   q_ref, k_ref, vt_ref, sg_ref):
    u = _rms(x_ref[...], g_ref[...]).astype(BF16)
    a = jnp.dot(u, wa_ref[...], preferred_element_type=F32)
    cq = a[:, :Q_LORA]
    ckv = a[:, Q_LORA:Q_LORA + KV_LORA]
    kr = a[:, Q_LORA + KV_LORA:]
    q = jnp.dot(_rms(cq, qn_ref[...]).astype(BF16), wq_ref[...], preferred_element_type=F32) * (MLA_SCALE * LOG2E)
    kv = jnp.dot(_rms(ckv, kvn_ref[...]).astype(BF16), wkv_ref[...], preferred_element_type=F32)
    cos = cos_ref[...]
    sin = sin_ref[...]
    kr_rot = _rope_tile(kr, cos, sin).astype(BF16)
    for hd in range(MLA_HEADS):
        lo, hi = hd * LANES, (hd + 1) * LANES
        q_ref[0, hd, :, 0:LANES] = q[:, lo:hi].astype(BF16)
        q_ref[0, hd, :, LANES:2 * LANES] = _rope_tile(q[:, MLA_V + lo:MLA_V + hi], cos, sin).astype(BF16)
        k_ref[0, hd, :, 0:LANES] = kv[:, lo:hi].astype(BF16)
        k_ref[0, hd, :, LANES:2 * LANES] = kr_rot
        vt_ref[0, hd, 0] = kv[:, MLA_V + lo:MLA_V + hi].T.astype(BF16)
    gate = jnp.dot(u, wg_ref[...], preferred_element_type=F32)
    sg_ref[...] = _silu(gate).astype(BF16)


def _odd_in(h, g, wa, wg, qn, wq, kvn, wkv, cos_t, sin_t):
    R = ODD_R
    nb = SEQ // R
    row_map = lambda b, i: (b * nb + i, 0)
    const2 = lambda b, i: (0, 0)
    return pl.pallas_call(
        _odd_in_kernel,
        grid=(BATCH, nb),
        in_specs=[
            pl.BlockSpec((R, D_MODEL), row_map),
            pl.BlockSpec((1, D_MODEL), const2),
            pl.BlockSpec((D_MODEL, ODD_A), const2),
            pl.BlockSpec((D_MODEL, MLA_V), const2),
            pl.BlockSpec((1, Q_LORA), const2),
            pl.BlockSpec((Q_LORA, 2 * MLA_V), const2),
            pl.BlockSpec((1, KV_LORA), const2),
            pl.BlockSpec((KV_LORA, 2 * MLA_V), const2),
            pl.BlockSpec((R, LANES), row_map),
            pl.BlockSpec((R, LANES), row_map),
        ],
        out_specs=[
            pl.BlockSpec((1, MLA_HEADS, R, MLA_QK_PAD), lambda b, i: (b, 0, i, 0)),
            pl.BlockSpec((1, MLA_HEADS, R, MLA_QK_PAD), lambda b, i: (b, 0, i, 0)),
            pl.BlockSpec((1, MLA_HEADS, 1, MLA_DV, R), lambda b, i: (b, 0, i, 0, 0)),
            pl.BlockSpec((R, MLA_V), row_map),
        ],
        out_shape=[
            jax.ShapeDtypeStruct((BATCH, MLA_HEADS, SEQ, MLA_QK_PAD), BF16),
            jax.ShapeDtypeStruct((BATCH, MLA_HEADS, SEQ, MLA_QK_PAD), BF16),
            jax.ShapeDtypeStruct((BATCH, MLA_HEADS, nb, MLA_DV, R), BF16),
            jax.ShapeDtypeStruct((ROWS, MLA_V), BF16),
        ],
        compiler_params=pltpu.CompilerParams(
            dimension_semantics=("arbitrary", "arbitrary"), vmem_limit_bytes=VMEM_LIMIT),
        name="odd_in",
    )(h, g, wa, wg, qn, wq, kvn, wkv, cos_t, sin_t)


ATTN_T = ODD_R
ATTN_HG = 4


def _attn_kernel(q_ref, qn_ref, k_ref, vt_ref, sg_ref, o_ref, m_ref, l_ref, acc_ref, st_ref):
    T = ATTN_T
    qi = pl.program_id(2)
    m_ref[...] = jnp.full(m_ref.shape, -jnp.inf, F32)
    l_ref[...] = jnp.zeros(l_ref.shape, F32)
    acc_ref[...] = jnp.zeros(acc_ref.shape, F32)

    def scores(hh, j, queries=q_ref):
        k = k_ref[0, hh, pl.ds(pl.multiple_of(j * T, T), T), :]
        return lax.dot_general(k, queries[0, hh], (((1,), (1,)), ((), ())), preferred_element_type=F32)

    def consume(hh, j, st):
        m_old = m_ref[hh]
        m_new = jnp.maximum(m_old, jnp.max(st, axis=0, keepdims=True))
        alpha = jnp.exp2(m_old - m_new)
        p = jnp.exp2(st - m_new)
        l_ref[hh] = alpha * l_ref[hh] + jnp.sum(p, axis=0, keepdims=True)
        acc_ref[hh] = alpha * acc_ref[hh] + jnp.dot(
            vt_ref[0, hh, j], p.astype(BF16), preferred_element_type=F32)
        m_ref[hh] = m_new

    @pl.when(qi == 0)
    def _():
        for hh in range(ATTN_HG):
            st_ref[hh] = scores(hh, 0)

    def body(j, carry):
        for hh in range(ATTN_HG):
            st = st_ref[hh]
            st_ref[hh] = scores(hh, j + 1)
            consume(hh, j, st)
        return carry

    lax.fori_loop(0, qi, body, 0)
    kpos = lax.broadcasted_iota(jnp.int32, (T, T), 0)
    qpos = lax.broadcasted_iota(jnp.int32, (T, T), 1)
    for hh in range(ATTN_HG):
        st = st_ref[hh]
        st_ref[hh] = scores(hh, 0, qn_ref)
        consume(hh, qi, jnp.where(kpos <= qpos, st, -jnp.inf))
    for hh in range(ATTN_HG):
        cols = slice(hh * MLA_DV, (hh + 1) * MLA_DV)
        o = (acc_ref[hh] * (1.0 / l_ref[hh])).T
        o_ref[0, :, cols] = (o * sg_ref[0, :, cols].astype(F32)).astype(BF16)


def _attn(q_cat, k_cat, vt, sg):
    T = ATTN_T
    HG = ATTN_HG
    nq = SEQ // T
    return pl.pallas_call(
        _attn_kernel,
        grid=(BATCH, MLA_HEADS // HG, nq),
        in_specs=[
            pl.BlockSpec((1, HG, T, MLA_QK_PAD), lambda b, h, i: (b, h, i, 0)),
            pl.BlockSpec((1, HG, T, MLA_QK_PAD), lambda b, h, i: (b, h, jnp.minimum(i + 1, nq - 1), 0)),
            pl.BlockSpec((1, HG, SEQ, MLA_QK_PAD), lambda b, h, i: (b, h, 0, 0)),
            pl.BlockSpec((1, HG, nq, MLA_DV, T), lambda b, h, i: (b, h, 0, 0, 0)),
            pl.BlockSpec((1, T, HG * MLA_DV), lambda b, h, i: (b, i, h)),
        ],
        out_specs=pl.BlockSpec((1, T, HG * MLA_DV), lambda b, h, i: (b, i, h)),
        out_shape=jax.ShapeDtypeStruct((BATCH, SEQ, MLA_V), BF16),
        scratch_shapes=[
            pltpu.VMEM((HG, 1, T), F32),
            pltpu.VMEM((HG, 1, T), F32),
            pltpu.VMEM((HG, MLA_DV, T), F32),
            pltpu.VMEM((HG, T, T), F32),
        ],
        compiler_params=pltpu.CompilerParams(
            dimension_semantics=("arbitrary", "arbitrary", "arbitrary"), vmem_limit_bytes=VMEM_LIMIT),
        name="odd_attn",
    )(q_cat, q_cat, k_cat, vt, sg.reshape(BATCH, SEQ, MLA_V))


ODD_OUT_R = 1024


def _odd_out_kernel(og_ref, h_ref, w_ref, pn_ref, out_ref):
    y = jnp.dot(og_ref[...], w_ref[...], preferred_element_type=F32)
    out_ref[...] = h_ref[...] + _rms(y, pn_ref[...])


def _odd_out(og, h, w, post_g):
    R = ODD_OUT_R
    row_map = lambda i: (i, 0)
    const2 = lambda i: (0, 0)
    return pl.pallas_call(
        _odd_out_kernel,
        grid=(ROWS // R,),
        in_specs=[
            pl.BlockSpec((R, MLA_V), row_map),
            pl.BlockSpec((R, D_MODEL), row_map),
            pl.BlockSpec((MLA_V, D_MODEL), const2),
            pl.BlockSpec((1, D_MODEL), const2),
        ],
        out_specs=pl.BlockSpec((R, D_MODEL), row_map),
        out_shape=jax.ShapeDtypeStruct((ROWS, D_MODEL), F32),
        compiler_params=pltpu.CompilerParams(
            dimension_semantics=("arbitrary",), vmem_limit_bytes=VMEM_LIMIT),
        name="odd_out",
    )(og, h, w, post_g)


def _prep_even_w_in(w):
    qk = w[:, :_V0].reshape(D_MODEL, 2 * RET_PAIRS, 2, 2, ROPE_HALF)
    qk = qk.transpose(0, 1, 3, 2, 4).reshape(D_MODEL, _V0)
    conv = w[:, _CB0:].reshape(D_MODEL, 4, CONV_W // LANES, LANES).transpose(0, 2, 1, 3).reshape(D_MODEL, 4 * CONV_W)
    return jnp.concatenate([qk, w[:, _V0:_CB0], conv], axis=1).astype(BF16)


def _rope_cols(w):
    z = jnp.zeros(w.shape[:-1] + (ROPE_HALF,), w.dtype)
    return jnp.concatenate([w[..., :ROPE_HALF], z, w[..., ROPE_HALF:], z], axis=-1)


def _prep_odd_weights(w_in, w_qb, w_kvb):
    lat = Q_LORA + KV_LORA
    wa = jnp.concatenate([w_in[:, :lat], _rope_cols(w_in[:, lat:lat + MLA_ROPE])], axis=1).astype(BF16)
    wg = w_in[:, lat + MLA_ROPE:].astype(BF16)
    wq3 = w_qb.reshape(Q_LORA, MLA_HEADS, MLA_QK)
    wq_nope = wq3[:, :, :MLA_NOPE].reshape(Q_LORA, MLA_HEADS * MLA_NOPE)
    wq_rope = _rope_cols(wq3[:, :, MLA_NOPE:]).reshape(Q_LORA, MLA_HEADS * LANES)
    wq = jnp.concatenate([wq_nope, wq_rope], axis=1).astype(BF16)
    wkv3 = w_kvb.reshape(KV_LORA, MLA_HEADS, MLA_NOPE + MLA_DV)
    wkv = jnp.concatenate(
        [wkv3[:, :, :MLA_NOPE].reshape(KV_LORA, MLA_HEADS * MLA_NOPE),
         wkv3[:, :, MLA_NOPE:].reshape(KV_LORA, MLA_V)], axis=1).astype(BF16)
    return wa, wg, wq, wkv


def kernel(x, positions, pre_norm, post_norm, even_w_in, even_conv_w, even_conv_b, ret_gn, even_w_out,
           odd_w_in, q_a_norm, w_qb, kv_a_norm, w_kvb, odd_w_out):
    cos_t, sin_t = _rope_tables(positions)
    tables = _retention_tables()
    h = x.reshape(ROWS, D_MODEL)
    for layer in range(DEPTH):
        pre_g = pre_norm[layer].reshape(1, D_MODEL)
        post_g = post_norm[layer].reshape(1, D_MODEL)
        if layer % 2 == 0:
            e = layer // 2
            qkvg, cmix = _even_in(h, pre_g, _prep_even_w_in(even_w_in[e]), cos_t, sin_t,
                                  even_conv_w[e], even_conv_b[e].reshape(1, CONV_W))
            h = _even_mix(qkvg, cmix, h, even_w_out[e].astype(BF16), ret_gn[e].reshape(1, RET_V),
                          post_g, tables)
        else:
            o = layer // 2
            wa, wg, wq, wkv = _prep_odd_weights(odd_w_in[o], w_qb[o], w_kvb[o])
            q_cat, k_cat, vt, sg = _odd_in(h, pre_g, wa, wg, q_a_norm[o].reshape(1, Q_LORA), wq,
                                           kv_a_norm[o].reshape(1, KV_LORA), wkv, cos_t, sin_t)
            og = _attn(q_cat, k_cat, vt, sg)
            h = _odd_out(og.reshape(ROWS, MLA_V), h, odd_w_out[o].astype(BF16), post_g)
    return h.reshape(BATCH, SEQ, D_MODEL)
```

```python
import math

import jax
import jax.numpy as jnp
import numpy as np
from jax import lax
from jax.experimental import pallas as pl
from jax.experimental.pallas import tpu as pltpu

D_MODEL = 1024
BATCH = 4
SEQ = 4096
DEPTH = 4
ROWS = BATCH * SEQ

RET_HEADS = 8
RET_DK = 64
RET_DV = 128
RET_CHUNK = 128
RET_PAIRS = RET_HEADS // 2
CONV_W = 1024
CONV_K = 3
MLA_HEADS = 8
MLA_NOPE = 128
MLA_ROPE = 64
MLA_DV = 128
Q_LORA = 384
KV_LORA = 256
ROPE_BASE = 10000.0
EPS = 1e-6

RET_QK = RET_HEADS * RET_DK
RET_V = RET_HEADS * RET_DV
EVEN_IN = 2 * RET_QK + 2 * RET_V + 4 * CONV_W
EVEN_MIX = RET_V + CONV_W
MLA_QK = MLA_NOPE + MLA_ROPE
MLA_V = MLA_HEADS * MLA_DV
MLA_SCALE = MLA_QK ** -0.5
LOG2E = math.log2(math.e)
MLA_QK_PAD = 256
ODD_IN = Q_LORA + KV_LORA + MLA_ROPE + MLA_V
ODD_A = Q_LORA + KV_LORA + 128

_Q0, _K0, _V0, _GR0, _CB0, _CC0, _CX0, _GC0 = 0, 512, 1024, 2048, 3072, 4096, 5120, 6144

LANES = 128
SUBLANES = 8
ROPE_HALF = 32
HALF_TILE = LANES // 2

VMEM_LIMIT = 56 * 1024 * 1024

BF16 = jnp.bfloat16
F32 = jnp.float32


def _silu(x):
    return (0.5 * x) * (1.0 + jnp.tanh(0.5 * x))


def _rms(x, g):
    ms = jnp.mean(x * x, axis=-1, keepdims=True)
    return x * lax.rsqrt(ms + EPS) * g


def _rope_tile(x, cos, sin_signed):
    return x * cos + pltpu.roll(x, HALF_TILE, 1) * sin_signed


ROPE_GROUPS = LANES // ROPE_HALF
ROPE_BLK = 512


def _rope_table_kernel(pos_ref, inv_ref, cos_ref, sin_ref):
    ang = pos_ref[...].astype(F32) * inv_ref[...]
    c = jnp.cos(ang)
    s = jnp.sin(ang)
    lane = lax.broadcasted_iota(jnp.int32, (ROPE_BLK, LANES), 1)
    low = lane < ROPE_HALF
    for g in range(ROPE_GROUPS):
        shift = (LANES - g * ROPE_HALF) % LANES
        cg = jnp.where(low, c if shift == 0 else pltpu.roll(c, shift, 1), 0.0)
        sg = jnp.where(low, s if shift == 0 else pltpu.roll(s, shift, 1), 0.0)
        cg = cg + pltpu.roll(cg, ROPE_HALF, 1)
        sg = sg + pltpu.roll(sg, ROPE_HALF, 1)
        cos_ref[g] = cg + pltpu.roll(cg, HALF_TILE, 1)
        sin_ref[g] = pltpu.roll(sg, HALF_TILE, 1) - sg


def _rope_tables(positions):
    inv = ROPE_BASE ** (-jnp.arange(0, 2 * ROPE_HALF, 2, dtype=F32) / (2 * ROPE_HALF))
    dense_rows = ROWS // ROPE_GROUPS
    pos_dense = jnp.repeat(positions.reshape(ROPE_GROUPS, dense_rows).T, ROPE_HALF, axis=1)
    inv_dense = jnp.tile(inv, ROPE_GROUPS).reshape(1, LANES)
    out_spec = pl.BlockSpec((ROPE_GROUPS, ROPE_BLK, LANES), lambda i: (0, i, 0))
    out = jax.ShapeDtypeStruct((ROPE_GROUPS, dense_rows, LANES), F32)
    cos_t, sin_t = pl.pallas_call(
        _rope_table_kernel,
        grid=(dense_rows // ROPE_BLK,),
        in_specs=[pl.BlockSpec((ROPE_BLK, LANES), lambda i: (i, 0)),
                  pl.BlockSpec((1, LANES), lambda i: (0, 0))],
        out_specs=[out_spec, out_spec],
        out_shape=[out, out],
        name="rope_table",
    )(pos_dense, inv_dense)
    return cos_t.reshape(ROWS, LANES), sin_t.reshape(ROWS, LANES)


EVEN_IN_TM = 1024
EVEN_IN_TN = 1024
EVEN_QKVG = 3 * EVEN_IN_TN
CONV_BLK = EVEN_IN_TN // 4
CONV_BLOCKS = CONV_W // CONV_BLK
EVEN_IN_STRIP = 256
CONV_STRIP = 128


def _even_in_kernel(x_ref, g_ref, w_ref, cos_ref, sin_ref, cw_ref, cb_ref,
                    qkvg_ref, cmix_ref, u_ref, tail_ref):
    i = pl.program_id(0)
    j = pl.program_id(1)
    strips = [slice(r * EVEN_IN_STRIP, (r + 1) * EVEN_IN_STRIP) for r in range(EVEN_IN_TM // EVEN_IN_STRIP)]

    halves = [slice(0, EVEN_IN_TN // 2), slice(EVEN_IN_TN // 2, EVEN_IN_TN)]

    @pl.when(j == 0)
    def _():
        for rs in strips:
            u = _rms(x_ref[rs, :], g_ref[...]).astype(BF16)
            u_ref[rs, :] = u
            cos = cos_ref[rs, :]
            sin = sin_ref[rs, :]
            for ns in halves:
                acc = jnp.dot(u, w_ref[:, ns], preferred_element_type=F32)
                for c in range((ns.stop - ns.start) // LANES):
                    y = _rope_tile(acc[:, c * LANES:(c + 1) * LANES], cos, sin)
                    if ns.start >= _K0:
                        y = y * (RET_DK ** -0.5)
                    qkvg_ref[rs, ns.start + c * LANES:ns.start + (c + 1) * LANES] = y.astype(BF16)

    @pl.when(j == 1)
    def _():
        qkvg_ref[...] = jnp.dot(u_ref[...], w_ref[...], preferred_element_type=F32).astype(BF16)

    @pl.when(j == 2)
    def _():
        for rs in strips:
            for ns in halves:
                acc = jnp.dot(u_ref[rs, :], w_ref[:, ns], preferred_element_type=F32)
                qkvg_ref[rs, ns] = _silu(acc).astype(BF16)

    @pl.when(j >= 3)
    def _():
        slot = j - 3
        sequence_start = (i % (SEQ // EVEN_IN_TM)) == 0
        lane_tiles = [slice(ct * LANES, (ct + 1) * LANES) for ct in range(CONV_BLK // LANES)]
        halos = [jnp.where(sequence_start, 0.0, tail_ref[slot, :, cs]) for cs in lane_tiles]
        for rs in strips:
            for ct, cs in enumerate(lane_tiles):
                acc = jnp.dot(u_ref[rs, :], w_ref[:, ct * 4 * LANES:(ct + 1) * 4 * LANES],
                              preferred_element_type=F32)
                for r in range(EVEN_IN_STRIP // CONV_STRIP):
                    rr = slice(r * CONV_STRIP, (r + 1) * CONV_STRIP)
                    part = lambda k: acc[rr, k * LANES:(k + 1) * LANES]
                    prod = part(1) * part(2)
                    ext = jnp.concatenate([halos[ct], prod], axis=0)
                    conv = cb_ref[:, cs] + prod * cw_ref[CONV_K - 1:CONV_K, cs]
                    for tap in range(CONV_K - 1):
                        off = SUBLANES - (CONV_K - 1) + tap
                        conv = conv + ext[off:off + CONV_STRIP, :] * cw_ref[tap:tap + 1, cs]
                    halos[ct] = prod[CONV_STRIP - SUBLANES:, :]
                    out_rows = slice(rs.start + rr.start, rs.start + rr.stop)
                    cmix_ref[out_rows, cs] = (part(0) * conv * _silu(part(3))).astype(BF16)
        for ct, cs in enumerate(lane_tiles):
            tail_ref[slot, :, cs] = halos[ct]


def _even_in(h, pre_norm, w, cos_t, sin_t, conv_w, conv_b, layer):
    tm, tn = EVEN_IN_TM, EVEN_IN_TN
    e = layer // 2
    qkvg_blocks = EVEN_QKVG // tn
    conv_map = lambda i, j: (e, 0, jnp.maximum(j - qkvg_blocks, 0))
    return pl.pallas_call(
        _even_in_kernel,
        grid=(ROWS // tm, EVEN_IN // tn),
        in_specs=[
            pl.BlockSpec((tm, D_MODEL), lambda i, j: (i, 0)),
            pl.BlockSpec((None, 1, D_MODEL), lambda i, j: (layer, 0, 0)),
            pl.BlockSpec((None, D_MODEL, tn), lambda i, j: (e, 0, j)),
            pl.BlockSpec((tm, LANES), lambda i, j: (i, 0)),
            pl.BlockSpec((tm, LANES), lambda i, j: (i, 0)),
            pl.BlockSpec((None, CONV_K, CONV_BLK), conv_map),
            pl.BlockSpec((None, 1, CONV_BLK), conv_map),
        ],
        out_specs=[
            pl.BlockSpec((tm, tn), lambda i, j: (i, jnp.minimum(j, qkvg_blocks - 1))),
            pl.BlockSpec((tm, CONV_BLK), lambda i, j: (i, jnp.maximum(j - qkvg_blocks, 0))),
        ],
        out_shape=[
            jax.ShapeDtypeStruct((ROWS, EVEN_QKVG), BF16),
            jax.ShapeDtypeStruct((ROWS, CONV_W), BF16),
        ],
        scratch_shapes=[
            pltpu.VMEM((tm, D_MODEL), BF16),
            pltpu.VMEM((CONV_BLOCKS, SUBLANES, CONV_BLK), F32),
        ],
        compiler_params=pltpu.CompilerParams(
            dimension_semantics=("arbitrary", "arbitrary"), vmem_limit_bytes=VMEM_LIMIT),
        name="even_in",
    )(h, pre_norm, w, cos_t, sin_t, conv_w, conv_b)


EVEN_MIX_R = 512
EVEN_MIX_STEPS = ROWS // EVEN_MIX_R
PROJ_PIECE = 256


def _even_mix_block(p_ref, cmix_ref, h_ref, wout_ref, gn_ref, pn_ref,
                    intra_ref, xi_ref, zeta_ref, gam_ref, bm_ref,
                    o_ref, state_ref, mix_prev_ref, mix_ref):
    C = RET_CHUNK

    lane_k = lax.broadcasted_iota(jnp.int32, (C, LANES), 1)
    first_head = (lane_k % HALF_TILE) < ROPE_HALF
    lane_v = lax.broadcasted_iota(jnp.int32, (C, 2 * RET_DV), 1)
    zero_k = jnp.zeros((C, LANES), BF16)
    zero_v = jnp.zeros((C, 2 * RET_DV), BF16)

    def chunk_body(c, carry):
        rows = pl.ds(pl.multiple_of(c * C, C), C)

        mix_prev = mix_prev_ref[rows, :]
        cmix_prev = cmix_ref[rows, :]
        y_prev = []

        def project_piece():
            ns = slice(len(y_prev) * PROJ_PIECE, (len(y_prev) + 1) * PROJ_PIECE)
            y_prev.append(jnp.dot(mix_prev, wout_ref[:RET_V, ns], preferred_element_type=F32)
                          + jnp.dot(cmix_prev, wout_ref[RET_V:, ns], preferred_element_type=F32))

        for j in range(RET_PAIRS):
            project_piece()
            qp = p_ref[rows, _Q0 + j * LANES:_Q0 + (j + 1) * LANES]
            kp = p_ref[rows, _K0 + j * LANES:_K0 + (j + 1) * LANES]
            vp = p_ref[rows, _V0 + j * 2 * RET_DV:_V0 + (j + 1) * 2 * RET_DV]
            kstack = jnp.concatenate(
                [jnp.where(first_head, kp, zero_k), jnp.where(first_head, zero_k, kp)], axis=0)
            sc = lax.dot_general(qp, kstack, (((1,), (1,)), ((), ())), preferred_element_type=F32)
            pm = (sc * intra_ref[j]).astype(BF16)
            vblk = jnp.concatenate(
                [jnp.where(lane_v < RET_DV, vp, zero_v), jnp.where(lane_v >= RET_DV, vp, zero_v)], axis=0)
            st = state_ref[j]
            r = (jnp.dot(pm, vblk, preferred_element_type=F32)
                 + jnp.dot(qp, st.astype(BF16), preferred_element_type=F32) * xi_ref[j])
            kz = (kp.astype(F32) * zeta_ref[j]).astype(BF16)
            kv = lax.dot_general(kz, vp, (((0,), (0,)), ((), ())), preferred_element_type=F32)
            state_ref[j] = st * gam_ref[j] + kv * bm_ref[...]
            for hh in range(2):
                hd = 2 * j + hh
                cols = slice(hd * RET_DV, (hd + 1) * RET_DV)
                y = _rms(r[:, hh * RET_DV:(hh + 1) * RET_DV], gn_ref[:, cols])
                gate = p_ref[rows, _GR0 + hd * RET_DV:_GR0 + (hd + 1) * RET_DV].astype(F32)
                mix_ref[rows, cols] = (y * gate).astype(BF16)

        o_ref[rows, :] = h_ref[rows, :] + _rms(jnp.concatenate(y_prev, axis=1), pn_ref[...])
        return carry

    lax.fori_loop(0, EVEN_MIX_R // C, chunk_body, 0)


def _even_mix_kernel(p_ref, cmix_ref, h_ref, wout_ref, gn_ref, pn_ref,
                     intra_ref, xi_ref, zeta_ref, gam_ref, bm_ref,
                     o_ref, state_ref, mix_a_ref, mix_b_ref):
    t = pl.program_id(0)
    args = (p_ref, cmix_ref, h_ref, wout_ref, gn_ref, pn_ref,
            intra_ref, xi_ref, zeta_ref, gam_ref, bm_ref, o_ref, state_ref)

    @pl.when(t % (SEQ // EVEN_MIX_R) == 0)
    def _():
        state_ref[...] = jnp.zeros_like(state_ref)

    @pl.when(t == 0)
    def _():
        mix_b_ref[...] = jnp.zeros_like(mix_b_ref)

    @pl.when(t % 2 == 0)
    def _():
        _even_mix_block(*args, mix_b_ref, mix_a_ref)

    @pl.when(t % 2 == 1)
    def _():
        _even_mix_block(*args, mix_a_ref, mix_b_ref)


def _retention_tables():
    h, c = RET_HEADS, RET_CHUNK
    log_gamma = np.log1p(-np.exp2(-5.0 - np.arange(h, dtype=np.float64)))
    i = np.arange(c, dtype=np.float64)
    rel = i[:, None] - i[None, :]
    intra = np.where(rel >= 0, np.exp(log_gamma[:, None, None] * np.maximum(rel, 0.0)), 0.0)
    xi = np.exp(log_gamma[:, None] * (i + 1.0))
    zeta = np.exp(log_gamma[:, None] * (c - 1.0 - i))
    gamma_c = np.exp(log_gamma * c)
    pair = np.arange(RET_PAIRS)
    intra_p = intra.reshape(RET_PAIRS, 2, c, c).transpose(0, 2, 1, 3).reshape(RET_PAIRS, c, 2 * c)
    xi_p = np.repeat(xi.reshape(RET_PAIRS, 2, c).transpose(0, 2, 1), RET_DV, axis=2)
    second = ((np.arange(2 * RET_DK) % HALF_TILE) >= ROPE_HALF).astype(np.int64)
    zeta_p = zeta.reshape(RET_PAIRS, 2, c)[pair[:, None, None], second[None, None, :], np.arange(c)[None, :, None]]
    gam_rows = gamma_c.reshape(RET_PAIRS, 2)[:, second]
    gam_p = np.broadcast_to(gam_rows[:, :, None], (RET_PAIRS, 2 * RET_DK, 2 * RET_DV))
    col_head = np.arange(2 * RET_DV) // RET_DV
    bm = (second[:, None] == col_head[None, :])
    as_f32 = lambda a: jnp.asarray(np.ascontiguousarray(a, dtype=np.float32))
    return as_f32(intra_p), as_f32(xi_p), as_f32(zeta_p), as_f32(gam_p), as_f32(bm)


def _even_mix(qkvg, cmix, h, w_out, gn, post_norm, tables, layer):
    R = EVEN_MIX_R
    nt = EVEN_MIX_STEPS
    e = layer // 2
    intra_p, xi_p, zeta_p, gam_p, bm = tables
    mix_map = lambda t: (jnp.minimum(t, nt - 1), 0)
    row_map = lambda t: (jnp.maximum(t - 1, 0), 0)
    const2 = lambda t: (0, 0)
    const3 = lambda t: (0, 0, 0)
    return pl.pallas_call(
        _even_mix_kernel,
        grid=(nt + 1,),
        in_specs=[
            pl.BlockSpec((R, EVEN_QKVG), mix_map),
            pl.BlockSpec((R, CONV_W), row_map),
            pl.BlockSpec((R, D_MODEL), row_map),
            pl.BlockSpec((None, EVEN_MIX, D_MODEL), lambda t: (e, 0, 0)),
            pl.BlockSpec((None, 1, RET_V), lambda t: (e, 0, 0)),
            pl.BlockSpec((None, 1, D_MODEL), lambda t: (layer, 0, 0)),
            pl.BlockSpec((RET_PAIRS, RET_CHUNK, 2 * RET_CHUNK), const3),
            pl.BlockSpec((RET_PAIRS, RET_CHUNK, 2 * RET_DV), const3),
            pl.BlockSpec((RET_PAIRS, RET_CHUNK, 2 * RET_DK), const3),
            pl.BlockSpec((RET_PAIRS, 2 * RET_DK, 2 * RET_DV), const3),
            pl.BlockSpec((2 * RET_DK, 2 * RET_DV), const2),
        ],
        out_specs=pl.BlockSpec((R, D_MODEL), row_map),
        out_shape=jax.ShapeDtypeStruct((ROWS, D_MODEL), F32),
        scratch_shapes=[
            pltpu.VMEM((RET_PAIRS, 2 * RET_DK, 2 * RET_DV), F32),
            pltpu.VMEM((R, RET_V), BF16),
            pltpu.VMEM((R, RET_V), BF16),
        ],
        compiler_params=pltpu.CompilerParams(
            dimension_semantics=("arbitrary",), vmem_limit_bytes=VMEM_LIMIT),
        name="even_mix",
    )(qkvg, cmix, h, w_out, gn, post_norm, intra_p, xi_p, zeta_p, gam_p, bm)


ODD_R = 512


def _odd_in_kernel(x_ref, g_ref, wa_ref, wg_ref, qn_ref, wq_ref, kvn_ref, wkv_ref, cos_ref, sin_ref,
                   q_ref, k_ref, vt_ref, sg_ref):
    u = _rms(x_ref[...], g_ref[...]).astype(BF16)
    a = jnp.dot(u, wa_ref[...], preferred_element_type=F32)
    cq = a[:, :Q_LORA]
    ckv = a[:, Q_LORA:Q_LORA + KV_LORA]
    kr = a[:, Q_LORA + KV_LORA:]
    q = jnp.dot(_rms(cq, qn_ref[...]).astype(BF16), wq_ref[...], preferred_element_type=F32) * (MLA_SCALE * LOG2E)
    kv = jnp.dot(_rms(ckv, kvn_ref[...]).astype(BF16), wkv_ref[...], preferred_element_type=F32)
    cos = cos_ref[...]
    sin = sin_ref[...]
    kr_rot = _rope_tile(kr, cos, sin).astype(BF16)
    for hd in range(MLA_HEADS):
        lo, hi = hd * LANES, (hd + 1) * LANES
        q_ref[0, hd, :, 0:LANES] = q[:, lo:hi].astype(BF16)
        q_ref[0, hd, :, LANES:2 * LANES] = _rope_tile(q[:, MLA_V + lo:MLA_V + hi], cos, sin).astype(BF16)
        k_ref[0, hd, :, 0:LANES] = kv[:, lo:hi].astype(BF16)
        k_ref[0, hd, :, LANES:2 * LANES] = kr_rot
        vt_ref[0, hd, 0] = kv[:, MLA_V + lo:MLA_V + hi].T.astype(BF16)
    gate = jnp.dot(u, wg_ref[...], preferred_element_type=F32)
    sg_ref[...] = _silu(gate).astype(BF16)


def _odd_in(h, pre_norm, wa, wg, qn, wq, kvn, wkv, cos_t, sin_t, layer):
    R = ODD_R
    nb = SEQ // R
    o = layer // 2
    row_map = lambda b, i: (b * nb + i, 0)
    odd_map = lambda b, i: (o, 0, 0)
    return pl.pallas_call(
        _odd_in_kernel,
        grid=(BATCH, nb),
        in_specs=[
            pl.BlockSpec((R, D_MODEL), row_map),
            pl.BlockSpec((None, 1, D_MODEL), lambda b, i: (layer, 0, 0)),
            pl.BlockSpec((None, D_MODEL, ODD_A), odd_map),
            pl.BlockSpec((None, D_MODEL, MLA_V), odd_map),
            pl.BlockSpec((None, 1, Q_LORA), odd_map),
            pl.BlockSpec((None, Q_LORA, 2 * MLA_V), odd_map),
            pl.BlockSpec((None, 1, KV_LORA), odd_map),
            pl.BlockSpec((None, KV_LORA, 2 * MLA_V), odd_map),
            pl.BlockSpec((R, LANES), row_map),
            pl.BlockSpec((R, LANES), row_map),
        ],
        out_specs=[
            pl.BlockSpec((1, MLA_HEADS, R, MLA_QK_PAD), lambda b, i: (b, 0, i, 0)),
            pl.BlockSpec((1, MLA_HEADS, R, MLA_QK_PAD), lambda b, i: (b, 0, i, 0)),
            pl.BlockSpec((1, MLA_HEADS, 1, MLA_DV, R), lambda b, i: (b, 0, i, 0, 0)),
            pl.BlockSpec((R, MLA_V), row_map),
        ],
        out_shape=[
            jax.ShapeDtypeStruct((BATCH, MLA_HEADS, SEQ, MLA_QK_PAD), BF16),
            jax.ShapeDtypeStruct((BATCH, MLA_HEADS, SEQ, MLA_QK_PAD), BF16),
            jax.ShapeDtypeStruct((BATCH, MLA_HEADS, nb, MLA_DV, R), BF16),
            jax.ShapeDtypeStruct((ROWS, MLA_V), BF16),
        ],
        compiler_params=pltpu.CompilerParams(
            dimension_semantics=("arbitrary", "arbitrary"), vmem_limit_bytes=VMEM_LIMIT),
        name="odd_in",
    )(h, pre_norm, wa, wg, qn, wq, kvn, wkv, cos_t, sin_t)


ATTN_T = ODD_R
ATTN_HG = 4


def _attn_kernel(q_ref, qn_ref, k_ref, vt_ref, sg_ref, o_ref, m_ref, l_ref, acc_ref, st_ref):
    T = ATTN_T
    qi = pl.program_id(2)
    m_ref[...] = jnp.full(m_ref.shape, -jnp.inf, F32)
    l_ref[...] = jnp.zeros(l_ref.shape, F32)
    acc_ref[...] = jnp.zeros(acc_ref.shape, F32)

    def scores(hh, j, queries=q_ref):
        k = k_ref[0, hh, pl.ds(pl.multiple_of(j * T, T), T), :]
        return lax.dot_general(k, queries[0, hh], (((1,), (1,)), ((), ())), preferred_element_type=F32)

    def consume(hh, j, st):
        m_old = m_ref[hh]
        m_new = jnp.maximum(m_old, jnp.max(st, axis=0, keepdims=True))
        alpha = jnp.exp2(m_old - m_new)
        p = jnp.exp2(st - m_new)
        l_ref[hh] = alpha * l_ref[hh] + jnp.sum(p, axis=0, keepdims=True)
        acc_ref[hh] = alpha * acc_ref[hh] + jnp.dot(
            vt_ref[0, hh, j], p.astype(BF16), preferred_element_type=F32)
        m_ref[hh] = m_new

    @pl.when(qi == 0)
    def _():
        for hh in range(ATTN_HG):
            st_ref[hh] = scores(hh, 0)

    def body(j, carry):
        for hh in range(ATTN_HG):
            st = st_ref[hh]
            st_ref[hh] = scores(hh, j + 1)
            consume(hh, j, st)
        return carry

    lax.fori_loop(0, qi, body, 0)
    kpos = lax.broadcasted_iota(jnp.int32, (T, T), 0)
    qpos = lax.broadcasted_iota(jnp.int32, (T, T), 1)
    for hh in range(ATTN_HG):
        st = st_ref[hh]
        st_ref[hh] = scores(hh, 0, qn_ref)
        consume(hh, qi, jnp.where(kpos <= qpos, st, -jnp.inf))
    for hh in range(ATTN_HG):
        cols = slice(hh * MLA_DV, (hh + 1) * MLA_DV)
        o = (acc_ref[hh] * (1.0 / l_ref[hh])).T
        o_ref[0, :, cols] = (o * sg_ref[0, :, cols].astype(F32)).astype(BF16)


def _attn(q_cat, k_cat, vt, sg):
    T = ATTN_T
    HG = ATTN_HG
    nq = SEQ // T
    return pl.pallas_call(
        _attn_kernel,
        grid=(BATCH, MLA_HEADS // HG, nq),
        in_specs=[
            pl.BlockSpec((1, HG, T, MLA_QK_PAD), lambda b, h, i: (b, h, i, 0)),
            pl.BlockSpec((1, HG, T, MLA_QK_PAD), lambda b, h, i: (b, h, jnp.minimum(i + 1, nq - 1), 0)),
            pl.BlockSpec((1, HG, SEQ, MLA_QK_PAD), lambda b, h, i: (b, h, 0, 0)),
            pl.BlockSpec((1, HG, nq, MLA_DV, T), lambda b, h, i: (b, h, 0, 0, 0)),
            pl.BlockSpec((1, T, HG * MLA_DV), lambda b, h, i: (b, i, h)),
        ],
        out_specs=pl.BlockSpec((1, T, HG * MLA_DV), lambda b, h, i: (b, i, h)),
        out_shape=jax.ShapeDtypeStruct((BATCH, SEQ, MLA_V), BF16),
        scratch_shapes=[
            pltpu.VMEM((HG, 1, T), F32),
            pltpu.VMEM((HG, 1, T), F32),
            pltpu.VMEM((HG, MLA_DV, T), F32),
            pltpu.VMEM((HG, T, T), F32),
        ],
        compiler_params=pltpu.CompilerParams(
            dimension_semantics=("arbitrary", "arbitrary", "arbitrary"), vmem_limit_bytes=VMEM_LIMIT),
        name="odd_attn",
    )(q_cat, q_cat, k_cat, vt, sg.reshape(BATCH, SEQ, MLA_V))


ODD_OUT_R = 1024


def _odd_out_kernel(og_ref, h_ref, w_ref, pn_ref, out_ref):
    y = jnp.dot(og_ref[...], w_ref[...], preferred_element_type=F32)
    out_ref[...] = h_ref[...] + _rms(y, pn_ref[...])


def _odd_out(og, h, w, post_norm, layer):
    R = ODD_OUT_R
    row_map = lambda i: (i, 0)
    return pl.pallas_call(
        _odd_out_kernel,
        grid=(ROWS // R,),
        in_specs=[
            pl.BlockSpec((R, MLA_V), row_map),
            pl.BlockSpec((R, D_MODEL), row_map),
            pl.BlockSpec((None, MLA_V, D_MODEL), lambda i: (layer // 2, 0, 0)),
            pl.BlockSpec((None, 1, D_MODEL), lambda i: (layer, 0, 0)),
        ],
        out_specs=pl.BlockSpec((R, D_MODEL), row_map),
        out_shape=jax.ShapeDtypeStruct((ROWS, D_MODEL), F32),
        compiler_params=pltpu.CompilerParams(
            dimension_semantics=("arbitrary",), vmem_limit_bytes=VMEM_LIMIT),
        name="odd_out",
    )(og, h, w, post_norm)


PREP_ROW_BLOCKS = 4


def _lane_iota(rows):
    return lax.broadcasted_iota(jnp.int32, (rows, LANES), 1)


def _rope_spread(tile, lane, src_lo):
    x1 = tile if src_lo == 0 else pltpu.roll(tile, LANES - src_lo, 1)
    x2 = pltpu.roll(tile, (HALF_TILE - ROPE_HALF - src_lo) % LANES, 1)
    return jnp.where(lane < ROPE_HALF, x1, jnp.where((lane >= HALF_TILE) & (lane < HALF_TILE + ROPE_HALF), x2, 0.0))


def _prep_even_kernel(win_ref, wout_ref, win_o_ref, wout_o_ref):
    rows = win_ref.shape[0]
    lane = _lane_iota(rows)
    keep = (lane < ROPE_HALF) | (lane >= LANES - ROPE_HALF)
    for t in range(_V0 // LANES):
        cs = slice(t * LANES, (t + 1) * LANES)
        x = win_ref[:, cs]
        moved = jnp.where(lane < HALF_TILE, pltpu.roll(x, LANES - ROPE_HALF, 1), pltpu.roll(x, ROPE_HALF, 1))
        win_o_ref[:, cs] = jnp.where(keep, x, moved).astype(BF16)
    win_o_ref[:, _V0:_CB0] = win_ref[:, _V0:_CB0].astype(BF16)
    for t in range(CONV_W // LANES):
        for k in range(4):
            src = _CB0 + k * CONV_W + t * LANES
            dst = _CB0 + (4 * t + k) * LANES
            win_o_ref[:, dst:dst + LANES] = win_ref[:, src:src + LANES].astype(BF16)
    wout_o_ref[...] = wout_ref[...].astype(BF16)


def _prep_even(even_w_in, even_w_out):
    n = even_w_in.shape[0]
    rb = PREP_ROW_BLOCKS
    spec = lambda rows, cols: pl.BlockSpec((None, rows // rb, cols), lambda e, r: (e, r, 0))
    return pl.pallas_call(
        _prep_even_kernel,
        grid=(n, rb),
        in_specs=[spec(D_MODEL, EVEN_IN), spec(EVEN_MIX, D_MODEL)],
        out_specs=[spec(D_MODEL, EVEN_IN), spec(EVEN_MIX, D_MODEL)],
        out_shape=[jax.ShapeDtypeStruct((n, D_MODEL, EVEN_IN), BF16),
                   jax.ShapeDtypeStruct((n, EVEN_MIX, D_MODEL), BF16)],
        compiler_params=pltpu.CompilerParams(
            dimension_semantics=("arbitrary", "arbitrary"), vmem_limit_bytes=VMEM_LIMIT),
        name="prep_even",
    )(even_w_in, even_w_out)


def _prep_odd_kernel(win_ref, wqb_ref, wkvb_ref, wout_ref, wa_ref, wg_ref, wq_ref, wkv_ref, wo_ref):
    lat = Q_LORA + KV_LORA
    lane = _lane_iota(win_ref.shape[0])
    wa_ref[:, :lat] = win_ref[:, :lat].astype(BF16)
    wa_ref[:, lat:] = _rope_spread(win_ref[:, lat:lat + LANES], lane, 0).astype(BF16)
    gate0 = lat + MLA_ROPE - HALF_TILE
    for t in range(MLA_V // LANES):
        a = pltpu.roll(win_ref[:, gate0 + t * LANES:gate0 + (t + 1) * LANES], HALF_TILE, 1)
        if t + 1 < MLA_V // LANES:
            b = pltpu.roll(win_ref[:, gate0 + (t + 1) * LANES:gate0 + (t + 2) * LANES], HALF_TILE, 1)
        else:
            edge = win_ref[:, gate0 + (t + 1) * LANES:]
            b = jnp.concatenate([edge, edge], axis=1)
        wg_ref[:, t * LANES:(t + 1) * LANES] = jnp.where(lane < HALF_TILE, a, b).astype(BF16)
    lane = _lane_iota(wqb_ref.shape[0])
    for hp in range(MLA_HEADS // 2):
        t0, t1, t2 = (wqb_ref[:, (3 * hp + k) * LANES:(3 * hp + k + 1) * LANES] for k in range(3))
        ha, hb = 2 * hp, 2 * hp + 1
        wq_ref[:, ha * LANES:(ha + 1) * LANES] = t0.astype(BF16)
        wq_ref[:, hb * LANES:(hb + 1) * LANES] = jnp.where(
            lane < HALF_TILE, pltpu.roll(t1, HALF_TILE, 1), pltpu.roll(t2, HALF_TILE, 1)).astype(BF16)
        wq_ref[:, MLA_V + ha * LANES:MLA_V + (ha + 1) * LANES] = _rope_spread(t1, lane, 0).astype(BF16)
        wq_ref[:, MLA_V + hb * LANES:MLA_V + (hb + 1) * LANES] = _rope_spread(t2, lane, HALF_TILE).astype(BF16)
    for hd in range(MLA_HEADS):
        wkv_ref[:, hd * LANES:(hd + 1) * LANES] = wkvb_ref[:, 2 * hd * LANES:(2 * hd + 1) * LANES].astype(BF16)
        wkv_ref[:, MLA_V + hd * LANES:MLA_V + (hd + 1) * LANES] = (
            wkvb_ref[:, (2 * hd + 1) * LANES:(2 * hd + 2) * LANES].astype(BF16))
    wo_ref[...] = wout_ref[...].astype(BF16)


def _prep_odd(odd_w_in, w_qb, w_kvb, odd_w_out):
    n = odd_w_in.shape[0]
    rb = PREP_ROW_BLOCKS
    spec = lambda rows, cols: pl.BlockSpec((None, rows // rb, cols), lambda o, r: (o, r, 0))
    shape = lambda rows, cols: jax.ShapeDtypeStruct((n, rows, cols), BF16)
    return pl.pallas_call(
        _prep_odd_kernel,
        grid=(n, rb),
        in_specs=[spec(D_MODEL, ODD_IN), spec(Q_LORA, MLA_HEADS * MLA_QK),
                  spec(KV_LORA, 2 * MLA_V), spec(MLA_V, D_MODEL)],
        out_specs=[spec(D_MODEL, ODD_A), spec(D_MODEL, MLA_V), spec(Q_LORA, 2 * MLA_V),
                   spec(KV_LORA, 2 * MLA_V), spec(MLA_V, D_MODEL)],
        out_shape=[shape(D_MODEL, ODD_A), shape(D_MODEL, MLA_V), shape(Q_LORA, 2 * MLA_V),
                   shape(KV_LORA, 2 * MLA_V), shape(MLA_V, D_MODEL)],
        compiler_params=pltpu.CompilerParams(
            dimension_semantics=("arbitrary", "arbitrary"), vmem_limit_bytes=VMEM_LIMIT),
        name="prep_odd",
    )(odd_w_in, w_qb, w_kvb, odd_w_out)


def _rows3(a):
    return a.reshape(a.shape[0], 1, a.shape[1])


def kernel(x, positions, pre_norm, post_norm, even_w_in, even_conv_w, even_conv_b, ret_gn, even_w_out,
           odd_w_in, q_a_norm, w_qb, kv_a_norm, w_kvb, odd_w_out):
    cos_t, sin_t = _rope_tables(positions)
    tables = _retention_tables()
    ew_in, ew_out = _prep_even(even_w_in, even_w_out)
    wa, wg, wq, wkv, wo = _prep_odd(odd_w_in, w_qb, w_kvb, odd_w_out)
    pre_norm, post_norm, ret_gn, even_conv_b = map(_rows3, (pre_norm, post_norm, ret_gn, even_conv_b))
    q_a_norm, kv_a_norm = _rows3(q_a_norm), _rows3(kv_a_norm)
    h = x.reshape(ROWS, D_MODEL)
    for layer in range(DEPTH):
        if layer % 2 == 0:
            qkvg, cmix = _even_in(h, pre_norm, ew_in, cos_t, sin_t, even_conv_w, even_conv_b, layer)
            h = _even_mix(qkvg, cmix, h, ew_out, ret_gn, post_norm, tables, layer)
        else:
            q_cat, k_cat, vt, sg = _odd_in(h, pre_norm, wa, wg, q_a_norm, wq, kv_a_norm, wkv, cos_t, sin_t, layer)
            og = _attn(q_cat, k_cat, vt, sg)
            h = _odd_out(og.reshape(ROWS, MLA_V), h, wo, post_norm, layer)
    return h.reshape(BATCH, SEQ, D_MODEL)
```

```python
import math

import jax
import jax.numpy as jnp
import numpy as np
from jax import lax
from jax.experimental import pallas as pl
from jax.experimental.pallas import tpu as pltpu

D_MODEL = 1024
BATCH = 4
SEQ = 4096
DEPTH = 4
ROWS = BATCH * SEQ

RET_HEADS = 8
RET_DK = 64
RET_DV = 128
RET_CHUNK = 128
RET_PAIRS = RET_HEADS // 2
CONV_W = 1024
CONV_K = 3
MLA_HEADS = 8
MLA_NOPE = 128
MLA_ROPE = 64
MLA_DV = 128
Q_LORA = 384
KV_LORA = 256
ROPE_BASE = 10000.0
EPS = 1e-6

RET_QK = RET_HEADS * RET_DK
RET_V = RET_HEADS * RET_DV
EVEN_IN = 2 * RET_QK + 2 * RET_V + 4 * CONV_W
EVEN_MIX = RET_V + CONV_W
MLA_QK = MLA_NOPE + MLA_ROPE
MLA_V = MLA_HEADS * MLA_DV
MLA_SCALE = MLA_QK ** -0.5
LOG2E = math.log2(math.e)
MLA_QK_PAD = 256
ODD_IN = Q_LORA + KV_LORA + MLA_ROPE + MLA_V
ODD_A = Q_LORA + KV_LORA + 128

_Q0, _K0, _V0, _GR0, _CB0, _CC0, _CX0, _GC0 = 0, 512, 1024, 2048, 3072, 4096, 5120, 6144

LANES = 128
SUBLANES = 8
ROPE_HALF = 32
HALF_TILE = LANES // 2

VMEM_LIMIT = 56 * 1024 * 1024

BF16 = jnp.bfloat16
F32 = jnp.float32


def _silu(x):
    return (0.5 * x) * (1.0 + jnp.tanh(0.5 * x))


def _rms(x, g):
    ms = jnp.mean(x * x, axis=-1, keepdims=True)
    return x * lax.rsqrt(ms + EPS) * g


def _rope_tile(x, cos, sin_signed):
    return x * cos + pltpu.roll(x, HALF_TILE, 1) * sin_signed


ROPE_GROUPS = LANES // ROPE_HALF
ROPE_BLK = 512


def _rope_table_kernel(pos_ref, inv_ref, cos_ref, sin_ref):
    ang = pos_ref[...].astype(F32) * inv_ref[...]
    c = jnp.cos(ang)
    s = jnp.sin(ang)
    lane = lax.broadcasted_iota(jnp.int32, (ROPE_BLK, LANES), 1)
    low = lane < ROPE_HALF
    for g in range(ROPE_GROUPS):
        shift = (LANES - g * ROPE_HALF) % LANES
        cg = jnp.where(low, c if shift == 0 else pltpu.roll(c, shift, 1), 0.0)
        sg = jnp.where(low, s if shift == 0 else pltpu.roll(s, shift, 1), 0.0)
        cg = cg + pltpu.roll(cg, ROPE_HALF, 1)
        sg = sg + pltpu.roll(sg, ROPE_HALF, 1)
        cos_ref[g] = cg + pltpu.roll(cg, HALF_TILE, 1)
        sin_ref[g] = pltpu.roll(sg, HALF_TILE, 1) - sg


def _rope_tables(positions):
    inv = ROPE_BASE ** (-jnp.arange(0, 2 * ROPE_HALF, 2, dtype=F32) / (2 * ROPE_HALF))
    dense_rows = ROWS // ROPE_GROUPS
    pos_dense = jnp.repeat(positions.reshape(ROPE_GROUPS, dense_rows).T, ROPE_HALF, axis=1)
    inv_dense = jnp.tile(inv, ROPE_GROUPS).reshape(1, LANES)
    out_spec = pl.BlockSpec((ROPE_GROUPS, ROPE_BLK, LANES), lambda i: (0, i, 0))
    out = jax.ShapeDtypeStruct((ROPE_GROUPS, dense_rows, LANES), F32)
    cos_t, sin_t = pl.pallas_call(
        _rope_table_kernel,
        grid=(dense_rows // ROPE_BLK,),
        in_specs=[pl.BlockSpec((ROPE_BLK, LANES), lambda i: (i, 0)),
                  pl.BlockSpec((1, LANES), lambda i: (0, 0))],
        out_specs=[out_spec, out_spec],
        out_shape=[out, out],
        name="rope_table",
    )(pos_dense, inv_dense)
    return cos_t.reshape(ROWS, LANES), sin_t.reshape(ROWS, LANES)


EVEN_IN_TM = 2048
EVEN_IN_TN = 1024
EVEN_QKVG = 3 * EVEN_IN_TN
CONV_BLK = EVEN_IN_TN // 4
CONV_BLOCKS = CONV_W // CONV_BLK
EVEN_IN_STRIP = 256
CONV_STRIP = 128


def _even_in_kernel(x_ref, g_ref, w_ref, cos_ref, sin_ref, cw_ref, cb_ref,
                    qkvg_ref, cmix_ref, u_ref, tail_ref):
    i = pl.program_id(0)
    j = pl.program_id(1)
    strips = [slice(r * EVEN_IN_STRIP, (r + 1) * EVEN_IN_STRIP) for r in range(EVEN_IN_TM // EVEN_IN_STRIP)]

    halves = [slice(0, EVEN_IN_TN // 2), slice(EVEN_IN_TN // 2, EVEN_IN_TN)]

    @pl.when(j == 0)
    def _():
        for rs in strips:
            u = _rms(x_ref[rs, :], g_ref[...]).astype(BF16)
            u_ref[rs, :] = u
            cos = cos_ref[rs, :]
            sin = sin_ref[rs, :]
            for ns in halves:
                acc = jnp.dot(u, w_ref[:, ns], preferred_element_type=F32)
                for c in range((ns.stop - ns.start) // LANES):
                    y = _rope_tile(acc[:, c * LANES:(c + 1) * LANES], cos, sin)
                    if ns.start >= _K0:
                        y = y * (RET_DK ** -0.5)
                    qkvg_ref[rs, ns.start + c * LANES:ns.start + (c + 1) * LANES] = y.astype(BF16)

    @pl.when(j == 1)
    def _():
        qkvg_ref[...] = jnp.dot(u_ref[...], w_ref[...], preferred_element_type=F32).astype(BF16)

    @pl.when(j == 2)
    def _():
        for rs in strips:
            for ns in halves:
                acc = jnp.dot(u_ref[rs, :], w_ref[:, ns], preferred_element_type=F32)
                qkvg_ref[rs, ns] = _silu(acc).astype(BF16)

    @pl.when(j >= 3)
    def _():
        slot = j - 3
        sequence_start = (i % (SEQ // EVEN_IN_TM)) == 0
        lane_tiles = [slice(ct * LANES, (ct + 1) * LANES) for ct in range(CONV_BLK // LANES)]
        halos = [jnp.where(sequence_start, 0.0, tail_ref[slot, :, cs]) for cs in lane_tiles]
        for rs in strips:
            for ct, cs in enumerate(lane_tiles):
                acc = jnp.dot(u_ref[rs, :], w_ref[:, ct * 4 * LANES:(ct + 1) * 4 * LANES],
                              preferred_element_type=F32)
                for r in range(EVEN_IN_STRIP // CONV_STRIP):
                    rr = slice(r * CONV_STRIP, (r + 1) * CONV_STRIP)
                    part = lambda k: acc[rr, k * LANES:(k + 1) * LANES]
                    prod = part(1) * part(2)
                    ext = jnp.concatenate([halos[ct], prod], axis=0)
                    conv = cb_ref[:, cs] + prod * cw_ref[CONV_K - 1:CONV_K, cs]
                    for tap in range(CONV_K - 1):
                        off = SUBLANES - (CONV_K - 1) + tap
                        conv = conv + ext[off:off + CONV_STRIP, :] * cw_ref[tap:tap + 1, cs]
                    halos[ct] = prod[CONV_STRIP - SUBLANES:, :]
                    out_rows = slice(rs.start + rr.start, rs.start + rr.stop)
                    cmix_ref[out_rows, cs] = (part(0) * conv * _silu(part(3))).astype(BF16)
        for ct, cs in enumerate(lane_tiles):
            tail_ref[slot, :, cs] = halos[ct]


def _even_in(h, pre_norm, w, cos_t, sin_t, conv_w, conv_b, layer):
    tm, tn = EVEN_IN_TM, EVEN_IN_TN
    e = layer // 2
    qkvg_blocks = EVEN_QKVG // tn
    conv_map = lambda i, j: (e, 0, jnp.maximum(j - qkvg_blocks, 0))
    return pl.pallas_call(
        _even_in_kernel,
        grid=(ROWS // tm, EVEN_IN // tn),
        in_specs=[
            pl.BlockSpec((tm, D_MODEL), lambda i, j: (i, 0)),
            pl.BlockSpec((None, 1, D_MODEL), lambda i, j: (layer, 0, 0)),
            pl.BlockSpec((None, D_MODEL, tn), lambda i, j: (e, 0, j)),
            pl.BlockSpec((tm, LANES), lambda i, j: (i, 0)),
            pl.BlockSpec((tm, LANES), lambda i, j: (i, 0)),
            pl.BlockSpec((None, CONV_K, CONV_BLK), conv_map),
            pl.BlockSpec((None, 1, CONV_BLK), conv_map),
        ],
        out_specs=[
            pl.BlockSpec((tm, tn), lambda i, j: (i, jnp.minimum(j, qkvg_blocks - 1))),
            pl.BlockSpec((tm, CONV_BLK), lambda i, j: (i, jnp.maximum(j - qkvg_blocks, 0))),
        ],
        out_shape=[
            jax.ShapeDtypeStruct((ROWS, EVEN_QKVG), BF16),
            jax.ShapeDtypeStruct((ROWS, CONV_W), BF16),
        ],
        scratch_shapes=[
            pltpu.VMEM((tm, D_MODEL), BF16),
            pltpu.VMEM((CONV_BLOCKS, SUBLANES, CONV_BLK), F32),
        ],
        compiler_params=pltpu.CompilerParams(
            dimension_semantics=("arbitrary", "arbitrary"), vmem_limit_bytes=VMEM_LIMIT),
        name="even_in",
    )(h, pre_norm, w, cos_t, sin_t, conv_w, conv_b)


EVEN_MIX_R = 512
EVEN_MIX_STEPS = ROWS // EVEN_MIX_R
PROJ_PIECE = 256
EVEN_MIX_CPI = 2


def _even_mix_block(p_ref, cmix_ref, h_ref, wout_ref, gn_ref, pn_ref,
                    intra_ref, xi_ref, zeta_ref, gam_ref, bm_ref,
                    o_ref, state_ref, mix_prev_ref, mix_ref):
    C = RET_CHUNK

    lane_k = lax.broadcasted_iota(jnp.int32, (C, LANES), 1)
    first_head = (lane_k % HALF_TILE) < ROPE_HALF
    lane_v = lax.broadcasted_iota(jnp.int32, (C, 2 * RET_DV), 1)
    zero_k = jnp.zeros((C, LANES), BF16)
    zero_v = jnp.zeros((C, 2 * RET_DV), BF16)

    def chunk_body(c2, carry):
        rows_all = pl.ds(pl.multiple_of(c2 * (EVEN_MIX_CPI * C), EVEN_MIX_CPI * C), EVEN_MIX_CPI * C)

        mix_prev = mix_prev_ref[rows_all, :]
        cmix_prev = cmix_ref[rows_all, :]
        y_prev = []

        def project_piece():
            ns = slice(len(y_prev) * PROJ_PIECE, (len(y_prev) + 1) * PROJ_PIECE)
            y_prev.append(jnp.dot(mix_prev, wout_ref[:RET_V, ns], preferred_element_type=F32)
                          + jnp.dot(cmix_prev, wout_ref[RET_V:, ns], preferred_element_type=F32))

        for sub, j in [(s, p) for s in range(EVEN_MIX_CPI) for p in range(RET_PAIRS)]:
            rows = pl.ds(pl.multiple_of(c2 * (EVEN_MIX_CPI * C) + sub * C, C), C)
            if (sub * RET_PAIRS + j) % (EVEN_MIX_CPI * RET_PAIRS * PROJ_PIECE // D_MODEL) == 0:
                project_piece()
            qp = p_ref[rows, _Q0 + j * LANES:_Q0 + (j + 1) * LANES]
            kp = p_ref[rows, _K0 + j * LANES:_K0 + (j + 1) * LANES]
            vp = p_ref[rows, _V0 + j * 2 * RET_DV:_V0 + (j + 1) * 2 * RET_DV]
            kstack = jnp.concatenate(
                [jnp.where(first_head, kp, zero_k), jnp.where(first_head, zero_k, kp)], axis=0)
            sc = lax.dot_general(qp, kstack, (((1,), (1,)), ((), ())), preferred_element_type=F32)
            pm = (sc * intra_ref[j]).astype(BF16)
            vblk = jnp.concatenate(
                [jnp.where(lane_v < RET_DV, vp, zero_v), jnp.where(lane_v >= RET_DV, vp, zero_v)], axis=0)
            st = state_ref[j]
            r = (jnp.dot(pm, vblk, preferred_element_type=F32)
                 + jnp.dot(qp, st.astype(BF16), preferred_element_type=F32) * xi_ref[j])
            kz = (kp.astype(F32) * zeta_ref[j]).astype(BF16)
            kv = lax.dot_general(kz, vp, (((0,), (0,)), ((), ())), preferred_element_type=F32)
            state_ref[j] = st * gam_ref[j] + kv * bm_ref[...]
            for hh in range(2):
                hd = 2 * j + hh
                cols = slice(hd * RET_DV, (hd + 1) * RET_DV)
                y = _rms(r[:, hh * RET_DV:(hh + 1) * RET_DV], gn_ref[:, cols])
                gate = p_ref[rows, _GR0 + hd * RET_DV:_GR0 + (hd + 1) * RET_DV].astype(F32)
                mix_ref[rows, cols] = (y * gate).astype(BF16)

        o_ref[rows_all, :] = h_ref[rows_all, :] + _rms(jnp.concatenate(y_prev, axis=1), pn_ref[...])
        return carry

    lax.fori_loop(0, EVEN_MIX_R // (EVEN_MIX_CPI * C), chunk_body, 0)


def _even_mix_kernel(p_ref, cmix_ref, h_ref, wout_ref, gn_ref, pn_ref,
                     intra_ref, xi_ref, zeta_ref, gam_ref, bm_ref,
                     o_ref, state_ref, mix_a_ref, mix_b_ref):
    t = pl.program_id(0)
    args = (p_ref, cmix_ref, h_ref, wout_ref, gn_ref, pn_ref,
            intra_ref, xi_ref, zeta_ref, gam_ref, bm_ref, o_ref, state_ref)

    @pl.when(t % (SEQ // EVEN_MIX_R) == 0)
    def _():
        state_ref[...] = jnp.zeros_like(state_ref)

    @pl.when(t == 0)
    def _():
        mix_b_ref[...] = jnp.zeros_like(mix_b_ref)

    @pl.when(t % 2 == 0)
    def _():
        _even_mix_block(*args, mix_b_ref, mix_a_ref)

    @pl.when(t % 2 == 1)
    def _():
        _even_mix_block(*args, mix_a_ref, mix_b_ref)


def _retention_tables():
    h, c = RET_HEADS, RET_CHUNK
    log_gamma = np.log1p(-np.exp2(-5.0 - np.arange(h, dtype=np.float64)))
    i = np.arange(c, dtype=np.float64)
    rel = i[:, None] - i[None, :]
    intra = np.where(rel >= 0, np.exp(log_gamma[:, None, None] * np.maximum(rel, 0.0)), 0.0)
    xi = np.exp(log_gamma[:, None] * (i + 1.0))
    zeta = np.exp(log_gamma[:, None] * (c - 1.0 - i))
    gamma_c = np.exp(log_gamma * c)
    pair = np.arange(RET_PAIRS)
    intra_p = intra.reshape(RET_PAIRS, 2, c, c).transpose(0, 2, 1, 3).reshape(RET_PAIRS, c, 2 * c)
    xi_p = np.repeat(xi.reshape(RET_PAIRS, 2, c).transpose(0, 2, 1), RET_DV, axis=2)
    second = ((np.arange(2 * RET_DK) % HALF_TILE) >= ROPE_HALF).astype(np.int64)
    zeta_p = zeta.reshape(RET_PAIRS, 2, c)[pair[:, None, None], second[None, None, :], np.arange(c)[None, :, None]]
    gam_rows = gamma_c.reshape(RET_PAIRS, 2)[:, second]
    gam_p = np.broadcast_to(gam_rows[:, :, None], (RET_PAIRS, 2 * RET_DK, 2 * RET_DV))
    col_head = np.arange(2 * RET_DV) // RET_DV
    bm = (second[:, None] == col_head[None, :])
    as_f32 = lambda a: jnp.asarray(np.ascontiguousarray(a, dtype=np.float32))
    return as_f32(intra_p), as_f32(xi_p), as_f32(zeta_p), as_f32(gam_p), as_f32(bm)


def _even_mix(qkvg, cmix, h, w_out, gn, post_norm, tables, layer):
    R = EVEN_MIX_R
    nt = EVEN_MIX_STEPS
    e = layer // 2
    intra_p, xi_p, zeta_p, gam_p, bm = tables
    mix_map = lambda t: (jnp.minimum(t, nt - 1), 0)
    row_map = lambda t: (jnp.maximum(t - 1, 0), 0)
    const2 = lambda t: (0, 0)
    const3 = lambda t: (0, 0, 0)
    return pl.pallas_call(
        _even_mix_kernel,
        grid=(nt + 1,),
        in_specs=[
            pl.BlockSpec((R, EVEN_QKVG), mix_map),
            pl.BlockSpec((R, CONV_W), row_map),
            pl.BlockSpec((R, D_MODEL), row_map),
            pl.BlockSpec((None, EVEN_MIX, D_MODEL), lambda t: (e, 0, 0)),
            pl.BlockSpec((None, 1, RET_V), lambda t: (e, 0, 0)),
            pl.BlockSpec((None, 1, D_MODEL), lambda t: (layer, 0, 0)),
            pl.BlockSpec((RET_PAIRS, RET_CHUNK, 2 * RET_CHUNK), const3),
            pl.BlockSpec((RET_PAIRS, RET_CHUNK, 2 * RET_DV), const3),
            pl.BlockSpec((RET_PAIRS, RET_CHUNK, 2 * RET_DK), const3),
            pl.BlockSpec((RET_PAIRS, 2 * RET_DK, 2 * RET_DV), const3),
            pl.BlockSpec((2 * RET_DK, 2 * RET_DV), const2),
        ],
        out_specs=pl.BlockSpec((R, D_MODEL), row_map),
        out_shape=jax.ShapeDtypeStruct((ROWS, D_MODEL), F32),
        scratch_shapes=[
            pltpu.VMEM((RET_PAIRS, 2 * RET_DK, 2 * RET_DV), F32),
            pltpu.VMEM((R, RET_V), BF16),
            pltpu.VMEM((R, RET_V), BF16),
        ],
        compiler_params=pltpu.CompilerParams(
            dimension_semantics=("arbitrary",), vmem_limit_bytes=VMEM_LIMIT),
        name="even_mix",
    )(qkvg, cmix, h, w_out, gn, post_norm, intra_p, xi_p, zeta_p, gam_p, bm)


ODD_R = 512


def _odd_in_kernel(x_ref, g_ref, wa_ref, wg_ref, qn_ref, wq_ref, kvn_ref, wkv_ref, cos_ref, sin_ref,
                   q_ref, k_ref, vt_ref, sg_ref):
    u = _rms(x_ref[...], g_ref[...]).astype(BF16)
    a = jnp.dot(u, wa_ref[...], preferred_element_type=F32)
    cq = a[:, :Q_LORA]
    ckv = a[:, Q_LORA:Q_LORA + KV_LORA]
    kr = a[:, Q_LORA + KV_LORA:]
    q = jnp.dot(_rms(cq, qn_ref[...]).astype(BF16), wq_ref[...], preferred_element_type=F32) * (MLA_SCALE * LOG2E)
    kv = jnp.dot(_rms(ckv, kvn_ref[...]).astype(BF16), wkv_ref[...], preferred_element_type=F32)
    cos = cos_ref[...]
    sin = sin_ref[...]
    kr_rot = _rope_tile(kr, cos, sin).astype(BF16)
    for hd in range(MLA_HEADS):
        lo, hi = hd * LANES, (hd + 1) * LANES
        q_ref[0, hd, :, 0:LANES] = q[:, lo:hi].astype(BF16)
        q_ref[0, hd, :, LANES:2 * LANES] = _rope_tile(q[:, MLA_V + lo:MLA_V + hi], cos, sin).astype(BF16)
        k_ref[0, hd, :, 0:LANES] = kv[:, lo:hi].astype(BF16)
        k_ref[0, hd, :, LANES:2 * LANES] = kr_rot
        vt_ref[0, hd, 0] = kv[:, MLA_V + lo:MLA_V + hi].T.astype(BF16)
    gate = jnp.dot(u, wg_ref[...], preferred_element_type=F32)
    sg_ref[...] = _silu(gate).astype(BF16)


def _odd_in(h, pre_norm, wa, wg, qn, wq, kvn, wkv, cos_t, sin_t, layer):
    R = ODD_R
    nb = SEQ // R
    o = layer // 2
    row_map = lambda b, i: (b * nb + i, 0)
    odd_map = lambda b, i: (o, 0, 0)
    return pl.pallas_call(
        _odd_in_kernel,
        grid=(BATCH, nb),
        in_specs=[
            pl.BlockSpec((R, D_MODEL), row_map),
            pl.BlockSpec((None, 1, D_MODEL), lambda b, i: (layer, 0, 0)),
            pl.BlockSpec((None, D_MODEL, ODD_A), odd_map),
            pl.BlockSpec((None, D_MODEL, MLA_V), odd_map),
            pl.BlockSpec((None, 1, Q_LORA), odd_map),
            pl.BlockSpec((None, Q_LORA, 2 * MLA_V), odd_map),
            pl.BlockSpec((None, 1, KV_LORA), odd_map),
            pl.BlockSpec((None, KV_LORA, 2 * MLA_V), odd_map),
            pl.BlockSpec((R, LANES), row_map),
            pl.BlockSpec((R, LANES), row_map),
        ],
        out_specs=[
            pl.BlockSpec((1, MLA_HEADS, R, MLA_QK_PAD), lambda b, i: (b, 0, i, 0)),
            pl.BlockSpec((1, MLA_HEADS, R, MLA_QK_PAD), lambda b, i: (b, 0, i, 0)),
            pl.BlockSpec((1, MLA_HEADS, 1, MLA_DV, R), lambda b, i: (b, 0, i, 0, 0)),
            pl.BlockSpec((R, MLA_V), row_map),
        ],
        out_shape=[
            jax.ShapeDtypeStruct((BATCH, MLA_HEADS, SEQ, MLA_QK_PAD), BF16),
            jax.ShapeDtypeStruct((BATCH, MLA_HEADS, SEQ, MLA_QK_PAD), BF16),
            jax.ShapeDtypeStruct((BATCH, MLA_HEADS, nb, MLA_DV, R), BF16),
            jax.ShapeDtypeStruct((ROWS, MLA_V), BF16),
        ],
        compiler_params=pltpu.CompilerParams(
            dimension_semantics=("arbitrary", "arbitrary"), vmem_limit_bytes=VMEM_LIMIT),
        name="odd_in",
    )(h, pre_norm, wa, wg, qn, wq, kvn, wkv, cos_t, sin_t)


ATTN_T = ODD_R
ATTN_HG = 4


def _attn_kernel(q_ref, qn_ref, k_ref, vt_ref, sg_ref, o_ref, m_ref, l_ref, acc_ref, st_ref):
    T = ATTN_T
    qi = pl.program_id(2)
    m_ref[...] = jnp.full(m_ref.shape, -jnp.inf, F32)
    l_ref[...] = jnp.zeros(l_ref.shape, F32)
    acc_ref[...] = jnp.zeros(acc_ref.shape, F32)

    def scores(hh, j, queries=q_ref):
        k = k_ref[0, hh, pl.ds(pl.multiple_of(j * T, T), T), :]
        return lax.dot_general(k, queries[0, hh], (((1,), (1,)), ((), ())), preferred_element_type=F32)

    def consume(hh, j, st):
        m_old = m_ref[hh]
        m_new = jnp.maximum(m_old, jnp.max(st, axis=0, keepdims=True))
        alpha = jnp.exp2(m_old - m_new)
        p = jnp.exp2(st - m_new)
        l_ref[hh] = alpha * l_ref[hh] + jnp.sum(p, axis=0, keepdims=True)
        acc_ref[hh] = alpha * acc_ref[hh] + jnp.dot(
            vt_ref[0, hh, j], p.astype(BF16), preferred_element_type=F32)
        m_ref[hh] = m_new

    @pl.when(qi == 0)
    def _():
        for hh in range(ATTN_HG):
            st_ref[hh] = scores(hh, 0)

    def body(j, carry):
        for hh in range(ATTN_HG):
            st = st_ref[hh]
            st_ref[hh] = scores(hh, j + 1)
            consume(hh, j, st)
        return carry

    lax.fori_loop(0, qi, body, 0)
    kpos = lax.broadcasted_iota(jnp.int32, (T, T), 0)
    qpos = lax.broadcasted_iota(jnp.int32, (T, T), 1)
    for hh in range(ATTN_HG):
        st = st_ref[hh]
        st_ref[hh] = scores(hh, 0, qn_ref)
        consume(hh, qi, jnp.where(kpos <= qpos, st, -jnp.inf))
    for hh in range(ATTN_HG):
        cols = slice(hh * MLA_DV, (hh + 1) * MLA_DV)
        o = (acc_ref[hh] * (1.0 / l_ref[hh])).T
        o_ref[0, :, cols] = (o * sg_ref[0, :, cols].astype(F32)).astype(BF16)


def _attn(q_cat, k_cat, vt, sg):
    T = ATTN_T
    HG = ATTN_HG
    nq = SEQ // T
    return pl.pallas_call(
        _attn_kernel,
        grid=(BATCH, MLA_HEADS // HG, nq),
        in_specs=[
            pl.BlockSpec((1, HG, T, MLA_QK_PAD), lambda b, h, i: (b, h, i, 0)),
            pl.BlockSpec((1, HG, T, MLA_QK_PAD), lambda b, h, i: (b, h, jnp.minimum(i + 1, nq - 1), 0)),
            pl.BlockSpec((1, HG, SEQ, MLA_QK_PAD), lambda b, h, i: (b, h, 0, 0)),
            pl.BlockSpec((1, HG, nq, MLA_DV, T), lambda b, h, i: (b, h, 0, 0, 0)),
            pl.BlockSpec((1, T, HG * MLA_DV), lambda b, h, i: (b, i, h)),
        ],
        out_specs=pl.BlockSpec((1, T, HG * MLA_DV), lambda b, h, i: (b, i, h)),
        out_shape=jax.ShapeDtypeStruct((BATCH, SEQ, MLA_V), BF16),
        scratch_shapes=[
            pltpu.VMEM((HG, 1, T), F32),
            pltpu.VMEM((HG, 1, T), F32),
            pltpu.VMEM((HG, MLA_DV, T), F32),
            pltpu.VMEM((HG, T, T), F32),
        ],
        compiler_params=pltpu.CompilerParams(
            dimension_semantics=("arbitrary", "arbitrary", "arbitrary"), vmem_limit_bytes=VMEM_LIMIT),
        name="odd_attn",
    )(q_cat, q_cat, k_cat, vt, sg.reshape(BATCH, SEQ, MLA_V))


ODD_OUT_R = 1024


def _odd_out_kernel(og_ref, h_ref, w_ref, pn_ref, out_ref):
    y = jnp.dot(og_ref[...], w_ref[...], preferred_element_type=F32)
    out_ref[...] = h_ref[...] + _rms(y, pn_ref[...])


def _odd_out(og, h, w, post_norm, layer):
    R = ODD_OUT_R
    row_map = lambda i: (i, 0)
    return pl.pallas_call(
        _odd_out_kernel,
        grid=(ROWS // R,),
        in_specs=[
            pl.BlockSpec((R, MLA_V), row_map),
            pl.BlockSpec((R, D_MODEL), row_map),
            pl.BlockSpec((None, MLA_V, D_MODEL), lambda i: (layer // 2, 0, 0)),
            pl.BlockSpec((None, 1, D_MODEL), lambda i: (layer, 0, 0)),
        ],
        out_specs=pl.BlockSpec((R, D_MODEL), row_map),
        out_shape=jax.ShapeDtypeStruct((ROWS, D_MODEL), F32),
        compiler_params=pltpu.CompilerParams(
            dimension_semantics=("arbitrary",), vmem_limit_bytes=VMEM_LIMIT),
        name="odd_out",
    )(og, h, w, post_norm)


PREP_ROW_BLOCKS = 4


def _lane_iota(rows):
    return lax.broadcasted_iota(jnp.int32, (rows, LANES), 1)


def _rope_spread(tile, lane, src_lo):
    x1 = tile if src_lo == 0 else pltpu.roll(tile, LANES - src_lo, 1)
    x2 = pltpu.roll(tile, (HALF_TILE - ROPE_HALF - src_lo) % LANES, 1)
    return jnp.where(lane < ROPE_HALF, x1, jnp.where((lane >= HALF_TILE) & (lane < HALF_TILE + ROPE_HALF), x2, 0.0))


def _prep_even_kernel(win_ref, wout_ref, win_o_ref, wout_o_ref):
    rows = win_ref.shape[0]
    lane = _lane_iota(rows)
    keep = (lane < ROPE_HALF) | (lane >= LANES - ROPE_HALF)
    for t in range(_V0 // LANES):
        cs = slice(t * LANES, (t + 1) * LANES)
        x = win_ref[:, cs]
        moved = jnp.where(lane < HALF_TILE, pltpu.roll(x, LANES - ROPE_HALF, 1), pltpu.roll(x, ROPE_HALF, 1))
        win_o_ref[:, cs] = jnp.where(keep, x, moved).astype(BF16)
    win_o_ref[:, _V0:_CB0] = win_ref[:, _V0:_CB0].astype(BF16)
    for t in range(CONV_W // LANES):
        for k in range(4):
            src = _CB0 + k * CONV_W + t * LANES
            dst = _CB0 + (4 * t + k) * LANES
            win_o_ref[:, dst:dst + LANES] = win_ref[:, src:src + LANES].astype(BF16)
    wout_o_ref[...] = wout_ref[...].astype(BF16)


def _prep_even(even_w_in, even_w_out):
    n = even_w_in.shape[0]
    rb = PREP_ROW_BLOCKS
    spec = lambda rows, cols: pl.BlockSpec((None, rows // rb, cols), lambda e, r: (e, r, 0))
    return pl.pallas_call(
        _prep_even_kernel,
        grid=(n, rb),
        in_specs=[spec(D_MODEL, EVEN_IN), spec(EVEN_MIX, D_MODEL)],
        out_specs=[spec(D_MODEL, EVEN_IN), spec(EVEN_MIX, D_MODEL)],
        out_shape=[jax.ShapeDtypeStruct((n, D_MODEL, EVEN_IN), BF16),
                   jax.ShapeDtypeStruct((n, EVEN_MIX, D_MODEL), BF16)],
        compiler_params=pltpu.CompilerParams(
            dimension_semantics=("arbitrary", "arbitrary"), vmem_limit_bytes=VMEM_LIMIT),
        name="prep_even",
    )(even_w_in, even_w_out)


def _prep_odd_kernel(win_ref, wqb_ref, wkvb_ref, wout_ref, wa_ref, wg_ref, wq_ref, wkv_ref, wo_ref):
    lat = Q_LORA + KV_LORA
    lane = _lane_iota(win_ref.shape[0])
    wa_ref[:, :lat] = win_ref[:, :lat].astype(BF16)
    wa_ref[:, lat:] = _rope_spread(win_ref[:, lat:lat + LANES], lane, 0).astype(BF16)
    gate0 = lat + MLA_ROPE - HALF_TILE
    for t in range(MLA_V // LANES):
        a = pltpu.roll(win_ref[:, gate0 + t * LANES:gate0 + (t + 1) * LANES], HALF_TILE, 1)
        if t + 1 < MLA_V // LANES:
            b = pltpu.roll(win_ref[:, gate0 + (t + 1) * LANES:gate0 + (t + 2) * LANES], HALF_TILE, 1)
        else:
            edge = win_ref[:, gate0 + (t + 1) * LANES:]
            b = jnp.concatenate([edge, edge], axis=1)
        wg_ref[:, t * LANES:(t + 1) * LANES] = jnp.where(lane < HALF_TILE, a, b).astype(BF16)
    lane = _lane_iota(wqb_ref.shape[0])
    for hp in range(MLA_HEADS // 2):
        t0, t1, t2 = (wqb_ref[:, (3 * hp + k) * LANES:(3 * hp + k + 1) * LANES] for k in range(3))
        ha, hb = 2 * hp, 2 * hp + 1
        wq_ref[:, ha * LANES:(ha + 1) * LANES] = t0.astype(BF16)
        wq_ref[:, hb * LANES:(hb + 1) * LANES] = jnp.where(
            lane < HALF_TILE, pltpu.roll(t1, HALF_TILE, 1), pltpu.roll(t2, HALF_TILE, 1)).astype(BF16)
        wq_ref[:, MLA_V + ha * LANES:MLA_V + (ha + 1) * LANES] = _rope_spread(t1, lane, 0).astype(BF16)
        wq_ref[:, MLA_V + hb * LANES:MLA_V + (hb + 1) * LANES] = _rope_spread(t2, lane, HALF_TILE).astype(BF16)
    for hd in range(MLA_HEADS):
        wkv_ref[:, hd * LANES:(hd + 1) * LANES] = wkvb_ref[:, 2 * hd * LANES:(2 * hd + 1) * LANES].astype(BF16)
        wkv_ref[:, MLA_V + hd * LANES:MLA_V + (hd + 1) * LANES] = (
            wkvb_ref[:, (2 * hd + 1) * LANES:(2 * hd + 2) * LANES].astype(BF16))
    wo_ref[...] = wout_ref[...].astype(BF16)


def _prep_odd(odd_w_in, w_qb, w_kvb, odd_w_out):
    n = odd_w_in.shape[0]
    rb = PREP_ROW_BLOCKS
    spec = lambda rows, cols: pl.BlockSpec((None, rows // rb, cols), lambda o, r: (o, r, 0))
    shape = lambda rows, cols: jax.ShapeDtypeStruct((n, rows, cols), BF16)
    return pl.pallas_call(
        _prep_odd_kernel,
        grid=(n, rb),
        in_specs=[spec(D_MODEL, ODD_IN), spec(Q_LORA, MLA_HEADS * MLA_QK),
                  spec(KV_LORA, 2 * MLA_V), spec(MLA_V, D_MODEL)],
        out_specs=[spec(D_MODEL, ODD_A), spec(D_MODEL, MLA_V), spec(Q_LORA, 2 * MLA_V),
                   spec(KV_LORA, 2 * MLA_V), spec(MLA_V, D_MODEL)],
        out_shape=[shape(D_MODEL, ODD_A), shape(D_MODEL, MLA_V), shape(Q_LORA, 2 * MLA_V),
                   shape(KV_LORA, 2 * MLA_V), shape(MLA_V, D_MODEL)],
        compiler_params=pltpu.CompilerParams(
            dimension_semantics=("arbitrary", "arbitrary"), vmem_limit_bytes=VMEM_LIMIT),
        name="prep_odd",
    )(odd_w_in, w_qb, w_kvb, odd_w_out)


def _rows3(a):
    return a.reshape(a.shape[0], 1, a.shape[1])


def kernel(x, positions, pre_norm, post_norm, even_w_in, even_conv_w, even_conv_b, ret_gn, even_w_out,
           odd_w_in, q_a_norm, w_qb, kv_a_norm, w_kvb, odd_w_out):
    cos_t, sin_t = _rope_tables(positions)
    tables = _retention_tables()
    ew_in, ew_out = _prep_even(even_w_in, even_w_out)
    wa, wg, wq, wkv, wo = _prep_odd(odd_w_in, w_qb, w_kvb, odd_w_out)
    pre_norm, post_norm, ret_gn, even_conv_b = map(_rows3, (pre_norm, post_norm, ret_gn, even_conv_b))
    q_a_norm, kv_a_norm = _rows3(q_a_norm), _rows3(kv_a_norm)
    h = x.reshape(ROWS, D_MODEL)
    for layer in range(DEPTH):
        if layer % 2 == 0:
            qkvg, cmix = _even_in(h, pre_norm, ew_in, cos_t, sin_t, even_conv_w, even_conv_b, layer)
            h = _even_mix(qkvg, cmix, h, ew_out, ret_gn, post_norm, tables, layer)
        else:
            q_cat, k_cat, vt, sg = _odd_in(h, pre_norm, wa, wg, q_a_norm, wq, kv_a_norm, wkv, cos_t, sin_t, layer)
            og = _attn(q_cat, k_cat, vt, sg)
            h = _odd_out(og.reshape(ROWS, MLA_V), h, wo, post_norm, layer)
    return h.reshape(BATCH, SEQ, D_MODEL)
```

```python
import math

import jax
import jax.numpy as jnp
import numpy as np
from jax import lax
from jax.experimental import pallas as pl
from jax.experimental.pallas import tpu as pltpu

D_MODEL = 1024
BATCH = 4
SEQ = 4096
DEPTH = 4
ROWS = BATCH * SEQ

RET_HEADS = 8
RET_DK = 64
RET_DV = 128
RET_CHUNK = 128
RET_PAIRS = RET_HEADS // 2
CONV_W = 1024
CONV_K = 3
MLA_HEADS = 8
MLA_NOPE = 128
MLA_ROPE = 64
MLA_DV = 128
Q_LORA = 384
KV_LORA = 256
ROPE_BASE = 10000.0
EPS = 1e-6

RET_QK = RET_HEADS * RET_DK
RET_V = RET_HEADS * RET_DV
EVEN_IN = 2 * RET_QK + 2 * RET_V + 4 * CONV_W
EVEN_MIX = RET_V + CONV_W
MLA_QK = MLA_NOPE + MLA_ROPE
MLA_V = MLA_HEADS * MLA_DV
MLA_SCALE = MLA_QK ** -0.5
LOG2E = math.log2(math.e)
MLA_QK_PAD = 256
ODD_IN = Q_LORA + KV_LORA + MLA_ROPE + MLA_V
ODD_A = Q_LORA + KV_LORA + 128

_Q0, _K0, _V0, _GR0, _CB0, _CC0, _CX0, _GC0 = 0, 512, 1024, 2048, 3072, 4096, 5120, 6144

LANES = 128
SUBLANES = 8
ROPE_HALF = 32
HALF_TILE = LANES // 2

VMEM_LIMIT = 56 * 1024 * 1024

BF16 = jnp.bfloat16
F32 = jnp.float32


def _silu(x):
    return (0.5 * x) * (1.0 + jnp.tanh(0.5 * x))


def _rms(x, g):
    ms = jnp.mean(x * x, axis=-1, keepdims=True)
    return x * lax.rsqrt(ms + EPS) * g


def _rope_tile(x, cos, sin_signed):
    return x * cos + pltpu.roll(x, HALF_TILE, 1) * sin_signed


ROPE_GROUPS = LANES // ROPE_HALF
ROPE_BLK = 512


def _rope_table_kernel(pos_ref, inv_ref, cos_ref, sin_ref):
    ang = pos_ref[...].astype(F32) * inv_ref[...]
    c = jnp.cos(ang)
    s = jnp.sin(ang)
    lane = lax.broadcasted_iota(jnp.int32, (ROPE_BLK, LANES), 1)
    low = lane < ROPE_HALF
    for g in range(ROPE_GROUPS):
        shift = (LANES - g * ROPE_HALF) % LANES
        cg = jnp.where(low, c if shift == 0 else pltpu.roll(c, shift, 1), 0.0)
        sg = jnp.where(low, s if shift == 0 else pltpu.roll(s, shift, 1), 0.0)
        cg = cg + pltpu.roll(cg, ROPE_HALF, 1)
        sg = sg + pltpu.roll(sg, ROPE_HALF, 1)
        cos_ref[g] = cg + pltpu.roll(cg, HALF_TILE, 1)
        sin_ref[g] = pltpu.roll(sg, HALF_TILE, 1) - sg


def _rope_tables(positions):
    inv = ROPE_BASE ** (-jnp.arange(0, 2 * ROPE_HALF, 2, dtype=F32) / (2 * ROPE_HALF))
    dense_rows = ROWS // ROPE_GROUPS
    pos_dense = jnp.repeat(positions.reshape(ROPE_GROUPS, dense_rows).T, ROPE_HALF, axis=1)
    inv_dense = jnp.tile(inv, ROPE_GROUPS).reshape(1, LANES)
    out_spec = pl.BlockSpec((ROPE_GROUPS, ROPE_BLK, LANES), lambda i: (0, i, 0))
    out = jax.ShapeDtypeStruct((ROPE_GROUPS, dense_rows, LANES), F32)
    cos_t, sin_t = pl.pallas_call(
        _rope_table_kernel,
        grid=(dense_rows // ROPE_BLK,),
        in_specs=[pl.BlockSpec((ROPE_BLK, LANES), lambda i: (i, 0)),
                  pl.BlockSpec((1, LANES), lambda i: (0, 0))],
        out_specs=[out_spec, out_spec],
        out_shape=[out, out],
        name="rope_table",
    )(pos_dense, inv_dense)
    return cos_t.reshape(ROWS, LANES), sin_t.reshape(ROWS, LANES)


EVEN_IN_TM = 2048
EVEN_IN_TN = 1024
EVEN_QKVG = 3 * EVEN_IN_TN
CONV_BLK = EVEN_IN_TN // 4
CONV_BLOCKS = CONV_W // CONV_BLK
EVEN_IN_STRIP = 256
CONV_STRIP = 128


def _even_in_kernel(x_ref, g_ref, w_ref, cos_ref, sin_ref, cw_ref, cb_ref,
                    qkvg_ref, cmix_ref, u_ref, tail_ref):
    i = pl.program_id(0)
    j = pl.program_id(1)
    strips = [slice(r * EVEN_IN_STRIP, (r + 1) * EVEN_IN_STRIP) for r in range(EVEN_IN_TM // EVEN_IN_STRIP)]

    halves = [slice(0, EVEN_IN_TN // 2), slice(EVEN_IN_TN // 2, EVEN_IN_TN)]

    @pl.when(j == 0)
    def _():
        for rs in strips:
            u = _rms(x_ref[rs, :], g_ref[...]).astype(BF16)
            u_ref[rs, :] = u
            cos = cos_ref[rs, :]
            sin = sin_ref[rs, :]
            for ns in halves:
                acc = jnp.dot(u, w_ref[:, ns], preferred_element_type=F32)
                for c in range((ns.stop - ns.start) // LANES):
                    y = _rope_tile(acc[:, c * LANES:(c + 1) * LANES], cos, sin)
                    if ns.start >= _K0:
                        y = y * (RET_DK ** -0.5)
                    qkvg_ref[rs, ns.start + c * LANES:ns.start + (c + 1) * LANES] = y.astype(BF16)

    @pl.when(j == 1)
    def _():
        qkvg_ref[...] = jnp.dot(u_ref[...], w_ref[...], preferred_element_type=F32).astype(BF16)

    @pl.when(j == 2)
    def _():
        for rs in strips:
            for ns in halves:
                acc = jnp.dot(u_ref[rs, :], w_ref[:, ns], preferred_element_type=F32)
                qkvg_ref[rs, ns] = _silu(acc).astype(BF16)

    @pl.when(j >= 3)
    def _():
        slot = j - 3
        sequence_start = (i % (SEQ // EVEN_IN_TM)) == 0
        lane_tiles = [slice(ct * LANES, (ct + 1) * LANES) for ct in range(CONV_BLK // LANES)]
        halos = [jnp.where(sequence_start, 0.0, tail_ref[slot, :, cs]) for cs in lane_tiles]
        for rs in strips:
            for ct, cs in enumerate(lane_tiles):
                acc = jnp.dot(u_ref[rs, :], w_ref[:, ct * 4 * LANES:(ct + 1) * 4 * LANES],
                              preferred_element_type=F32)
                for r in range(EVEN_IN_STRIP // CONV_STRIP):
                    rr = slice(r * CONV_STRIP, (r + 1) * CONV_STRIP)
                    part = lambda k: acc[rr, k * LANES:(k + 1) * LANES]
                    prod = part(1) * part(2)
                    ext = jnp.concatenate([halos[ct], prod], axis=0)
                    conv = cb_ref[:, cs] + prod * cw_ref[CONV_K - 1:CONV_K, cs]
                    for tap in range(CONV_K - 1):
                        off = SUBLANES - (CONV_K - 1) + tap
                        conv = conv + ext[off:off + CONV_STRIP, :] * cw_ref[tap:tap + 1, cs]
                    halos[ct] = prod[CONV_STRIP - SUBLANES:, :]
                    out_rows = slice(rs.start + rr.start, rs.start + rr.stop)
                    cmix_ref[out_rows, cs] = (part(0) * conv * _silu(part(3))).astype(BF16)
        for ct, cs in enumerate(lane_tiles):
            tail_ref[slot, :, cs] = halos[ct]


def _even_in(h, pre_norm, w, cos_t, sin_t, conv_w, conv_b, layer):
    tm, tn = EVEN_IN_TM, EVEN_IN_TN
    e = layer // 2
    qkvg_blocks = EVEN_QKVG // tn
    conv_map = lambda i, j: (e, 0, jnp.maximum(j - qkvg_blocks, 0))
    return pl.pallas_call(
        _even_in_kernel,
        grid=(ROWS // tm, EVEN_IN // tn),
        in_specs=[
            pl.BlockSpec((tm, D_MODEL), lambda i, j: (i, 0)),
            pl.BlockSpec((None, 1, D_MODEL), lambda i, j: (layer, 0, 0)),
            pl.BlockSpec((None, D_MODEL, tn), lambda i, j: (e, 0, j)),
            pl.BlockSpec((tm, LANES), lambda i, j: (i, 0)),
            pl.BlockSpec((tm, LANES), lambda i, j: (i, 0)),
            pl.BlockSpec((None, CONV_K, CONV_BLK), conv_map),
            pl.BlockSpec((None, 1, CONV_BLK), conv_map),
        ],
        out_specs=[
            pl.BlockSpec((tm, tn), lambda i, j: (i, jnp.minimum(j, qkvg_blocks - 1))),
            pl.BlockSpec((tm, CONV_BLK), lambda i, j: (i, jnp.maximum(j - qkvg_blocks, 0))),
        ],
        out_shape=[
            jax.ShapeDtypeStruct((ROWS, EVEN_QKVG), BF16),
            jax.ShapeDtypeStruct((ROWS, CONV_W), BF16),
        ],
        scratch_shapes=[
            pltpu.VMEM((tm, D_MODEL), BF16),
            pltpu.VMEM((CONV_BLOCKS, SUBLANES, CONV_BLK), F32),
        ],
        compiler_params=pltpu.CompilerParams(
            dimension_semantics=("arbitrary", "arbitrary"), vmem_limit_bytes=VMEM_LIMIT),
        name="even_in",
    )(h, pre_norm, w, cos_t, sin_t, conv_w, conv_b)


EVEN_MIX_R = 512
EVEN_MIX_STEPS = ROWS // EVEN_MIX_R
PROJ_PIECE = 256
EVEN_MIX_CPI = 4


def _even_mix_block(p_ref, cmix_ref, h_ref, wout_ref, gn_ref, pn_ref,
                    intra_ref, xi_ref, zeta_ref, gam_ref, bm_ref,
                    o_ref, state_ref, mix_prev_ref, mix_ref):
    C = RET_CHUNK

    lane_k = lax.broadcasted_iota(jnp.int32, (C, LANES), 1)
    first_head = (lane_k % HALF_TILE) < ROPE_HALF
    lane_v = lax.broadcasted_iota(jnp.int32, (C, 2 * RET_DV), 1)
    zero_k = jnp.zeros((C, LANES), BF16)
    zero_v = jnp.zeros((C, 2 * RET_DV), BF16)

    def chunk_body(c2, carry):
        rows_all = pl.ds(pl.multiple_of(c2 * (EVEN_MIX_CPI * C), EVEN_MIX_CPI * C), EVEN_MIX_CPI * C)

        mix_prev = mix_prev_ref[rows_all, :]
        cmix_prev = cmix_ref[rows_all, :]
        y_prev = []

        def project_piece():
            ns = slice(len(y_prev) * PROJ_PIECE, (len(y_prev) + 1) * PROJ_PIECE)
            y_prev.append(jnp.dot(mix_prev, wout_ref[:RET_V, ns], preferred_element_type=F32)
                          + jnp.dot(cmix_prev, wout_ref[RET_V:, ns], preferred_element_type=F32))

        for sub, j in [(s, p) for s in range(EVEN_MIX_CPI) for p in range(RET_PAIRS)]:
            rows = pl.ds(pl.multiple_of(c2 * (EVEN_MIX_CPI * C) + sub * C, C), C)
            if (sub * RET_PAIRS + j) % (EVEN_MIX_CPI * RET_PAIRS * PROJ_PIECE // D_MODEL) == 0:
                project_piece()
            qp = p_ref[rows, _Q0 + j * LANES:_Q0 + (j + 1) * LANES]
            kp = p_ref[rows, _K0 + j * LANES:_K0 + (j + 1) * LANES]
            vp = p_ref[rows, _V0 + j * 2 * RET_DV:_V0 + (j + 1) * 2 * RET_DV]
            kstack = jnp.concatenate(
                [jnp.where(first_head, kp, zero_k), jnp.where(first_head, zero_k, kp)], axis=0)
            sc = lax.dot_general(qp, kstack, (((1,), (1,)), ((), ())), preferred_element_type=F32)
            pm = (sc * intra_ref[j]).astype(BF16)
            vblk = jnp.concatenate(
                [jnp.where(lane_v < RET_DV, vp, zero_v), jnp.where(lane_v >= RET_DV, vp, zero_v)], axis=0)
            st = state_ref[j]
            r = (jnp.dot(pm, vblk, preferred_element_type=F32)
                 + jnp.dot(qp, st.astype(BF16), preferred_element_type=F32) * xi_ref[j])
            kz = (kp.astype(F32) * zeta_ref[j]).astype(BF16)
            kv = lax.dot_general(kz, vp, (((0,), (0,)), ((), ())), preferred_element_type=F32)
            state_ref[j] = st * gam_ref[j] + kv * bm_ref[...]
            for hh in range(2):
                hd = 2 * j + hh
                cols = slice(hd * RET_DV, (hd + 1) * RET_DV)
                y = _rms(r[:, hh * RET_DV:(hh + 1) * RET_DV], gn_ref[:, cols])
                gate = p_ref[rows, _GR0 + hd * RET_DV:_GR0 + (hd + 1) * RET_DV].astype(F32)
                mix_ref[rows, cols] = (y * gate).astype(BF16)

        o_ref[rows_all, :] = h_ref[rows_all, :] + _rms(jnp.concatenate(y_prev, axis=1), pn_ref[...])
        return carry

    lax.fori_loop(0, EVEN_MIX_R // (EVEN_MIX_CPI * C), chunk_body, 0)


def _even_mix_kernel(p_ref, cmix_ref, h_ref, wout_ref, gn_ref, pn_ref,
                     intra_ref, xi_ref, zeta_ref, gam_ref, bm_ref,
                     o_ref, state_ref, mix_a_ref, mix_b_ref):
    t = pl.program_id(0)
    args = (p_ref, cmix_ref, h_ref, wout_ref, gn_ref, pn_ref,
            intra_ref, xi_ref, zeta_ref, gam_ref, bm_ref, o_ref, state_ref)

    @pl.when(t % (SEQ // EVEN_MIX_R) == 0)
    def _():
        state_ref[...] = jnp.zeros_like(state_ref)

    @pl.when(t == 0)
    def _():
        mix_b_ref[...] = jnp.zeros_like(mix_b_ref)

    @pl.when(t % 2 == 0)
    def _():
        _even_mix_block(*args, mix_b_ref, mix_a_ref)

    @pl.when(t % 2 == 1)
    def _():
        _even_mix_block(*args, mix_a_ref, mix_b_ref)


def _retention_tables():
    h, c = RET_HEADS, RET_CHUNK
    log_gamma = np.log1p(-np.exp2(-5.0 - np.arange(h, dtype=np.float64)))
    i = np.arange(c, dtype=np.float64)
    rel = i[:, None] - i[None, :]
    intra = np.where(rel >= 0, np.exp(log_gamma[:, None, None] * np.maximum(rel, 0.0)), 0.0)
    xi = np.exp(log_gamma[:, None] * (i + 1.0))
    zeta = np.exp(log_gamma[:, None] * (c - 1.0 - i))
    gamma_c = np.exp(log_gamma * c)
    pair = np.arange(RET_PAIRS)
    intra_p = intra.reshape(RET_PAIRS, 2, c, c).transpose(0, 2, 1, 3).reshape(RET_PAIRS, c, 2 * c)
    xi_p = np.repeat(xi.reshape(RET_PAIRS, 2, c).transpose(0, 2, 1), RET_DV, axis=2)
    second = ((np.arange(2 * RET_DK) % HALF_TILE) >= ROPE_HALF).astype(np.int64)
    zeta_p = zeta.reshape(RET_PAIRS, 2, c)[pair[:, None, None], second[None, None, :], np.arange(c)[None, :, None]]
    gam_rows = gamma_c.reshape(RET_PAIRS, 2)[:, second]
    gam_p = np.broadcast_to(gam_rows[:, :, None], (RET_PAIRS, 2 * RET_DK, 2 * RET_DV))
    col_head = np.arange(2 * RET_DV) // RET_DV
    bm = (second[:, None] == col_head[None, :])
    as_f32 = lambda a: jnp.asarray(np.ascontiguousarray(a, dtype=np.float32))
    return as_f32(intra_p), as_f32(xi_p), as_f32(zeta_p), as_f32(gam_p), as_f32(bm)


def _even_mix(qkvg, cmix, h, w_out, gn, post_norm, tables, layer):
    R = EVEN_MIX_R
    nt = EVEN_MIX_STEPS
    e = layer // 2
    intra_p, xi_p, zeta_p, gam_p, bm = tables
    mix_map = lambda t: (jnp.minimum(t, nt - 1), 0)
    row_map = lambda t: (jnp.maximum(t - 1, 0), 0)
    const2 = lambda t: (0, 0)
    const3 = lambda t: (0, 0, 0)
    return pl.pallas_call(
        _even_mix_kernel,
        grid=(nt + 1,),
        in_specs=[
            pl.BlockSpec((R, EVEN_QKVG), mix_map),
            pl.BlockSpec((R, CONV_W), row_map),
            pl.BlockSpec((R, D_MODEL), row_map),
            pl.BlockSpec((None, EVEN_MIX, D_MODEL), lambda t: (e, 0, 0)),
            pl.BlockSpec((None, 1, RET_V), lambda t: (e, 0, 0)),
            pl.BlockSpec((None, 1, D_MODEL), lambda t: (layer, 0, 0)),
            pl.BlockSpec((RET_PAIRS, RET_CHUNK, 2 * RET_CHUNK), const3),
            pl.BlockSpec((RET_PAIRS, RET_CHUNK, 2 * RET_DV), const3),
            pl.BlockSpec((RET_PAIRS, RET_CHUNK, 2 * RET_DK), const3),
            pl.BlockSpec((RET_PAIRS, 2 * RET_DK, 2 * RET_DV), const3),
            pl.BlockSpec((2 * RET_DK, 2 * RET_DV), const2),
        ],
        out_specs=pl.BlockSpec((R, D_MODEL), row_map),
        out_shape=jax.ShapeDtypeStruct((ROWS, D_MODEL), F32),
        scratch_shapes=[
            pltpu.VMEM((RET_PAIRS, 2 * RET_DK, 2 * RET_DV), F32),
            pltpu.VMEM((R, RET_V), BF16),
            pltpu.VMEM((R, RET_V), BF16),
        ],
        compiler_params=pltpu.CompilerParams(
            dimension_semantics=("arbitrary",), vmem_limit_bytes=VMEM_LIMIT),
        name="even_mix",
    )(qkvg, cmix, h, w_out, gn, post_norm, intra_p, xi_p, zeta_p, gam_p, bm)


ODD_R = 512


def _odd_in_kernel(x_ref, g_ref, wa_ref, wg_ref, qn_ref, wq_ref, kvn_ref, wkv_ref, cos_ref, sin_ref,
                   q_ref, k_ref, vt_ref, sg_ref):
    u = _rms(x_ref[...], g_ref[...]).astype(BF16)
    a = jnp.dot(u, wa_ref[...], preferred_element_type=F32)
    cq = a[:, :Q_LORA]
    ckv = a[:, Q_LORA:Q_LORA + KV_LORA]
    kr = a[:, Q_LORA + KV_LORA:]
    q = jnp.dot(_rms(cq, qn_ref[...]).astype(BF16), wq_ref[...], preferred_element_type=F32) * (MLA_SCALE * LOG2E)
    kv = jnp.dot(_rms(ckv, kvn_ref[...]).astype(BF16), wkv_ref[...], preferred_element_type=F32)
    cos = cos_ref[...]
    sin = sin_ref[...]
    kr_rot = _rope_tile(kr, cos, sin).astype(BF16)
    for hd in range(MLA_HEADS):
        lo, hi = hd * LANES, (hd + 1) * LANES
        q_ref[0, hd, 0:LANES, :] = q[:, lo:hi].T.astype(BF16)
        q_ref[0, hd, LANES:2 * LANES, :] = _rope_tile(q[:, MLA_V + lo:MLA_V + hi], cos, sin).T.astype(BF16)
        k_ref[0, hd, :, 0:LANES] = kv[:, lo:hi].astype(BF16)
        k_ref[0, hd, :, LANES:2 * LANES] = kr_rot
        vt_ref[0, hd, 0] = kv[:, MLA_V + lo:MLA_V + hi].T.astype(BF16)
    gate = jnp.dot(u, wg_ref[...], preferred_element_type=F32)
    sg_ref[...] = _silu(gate).astype(BF16)


def _odd_in(h, pre_norm, wa, wg, qn, wq, kvn, wkv, cos_t, sin_t, layer):
    R = ODD_R
    nb = SEQ // R
    o = layer // 2
    row_map = lambda b, i: (b * nb + i, 0)
    odd_map = lambda b, i: (o, 0, 0)
    return pl.pallas_call(
        _odd_in_kernel,
        grid=(BATCH, nb),
        in_specs=[
            pl.BlockSpec((R, D_MODEL), row_map),
            pl.BlockSpec((None, 1, D_MODEL), lambda b, i: (layer, 0, 0)),
            pl.BlockSpec((None, D_MODEL, ODD_A), odd_map),
            pl.BlockSpec((None, D_MODEL, MLA_V), odd_map),
            pl.BlockSpec((None, 1, Q_LORA), odd_map),
            pl.BlockSpec((None, Q_LORA, 2 * MLA_V), odd_map),
            pl.BlockSpec((None, 1, KV_LORA), odd_map),
            pl.BlockSpec((None, KV_LORA, 2 * MLA_V), odd_map),
            pl.BlockSpec((R, LANES), row_map),
            pl.BlockSpec((R, LANES), row_map),
        ],
        out_specs=[
            pl.BlockSpec((1, MLA_HEADS, MLA_QK_PAD, R), lambda b, i: (b, 0, 0, i)),
            pl.BlockSpec((1, MLA_HEADS, R, MLA_QK_PAD), lambda b, i: (b, 0, i, 0)),
            pl.BlockSpec((1, MLA_HEADS, 1, MLA_DV, R), lambda b, i: (b, 0, i, 0, 0)),
            pl.BlockSpec((R, MLA_V), row_map),
        ],
        out_shape=[
            jax.ShapeDtypeStruct((BATCH, MLA_HEADS, MLA_QK_PAD, SEQ), BF16),
            jax.ShapeDtypeStruct((BATCH, MLA_HEADS, SEQ, MLA_QK_PAD), BF16),
            jax.ShapeDtypeStruct((BATCH, MLA_HEADS, nb, MLA_DV, R), BF16),
            jax.ShapeDtypeStruct((ROWS, MLA_V), BF16),
        ],
        compiler_params=pltpu.CompilerParams(
            dimension_semantics=("arbitrary", "arbitrary"), vmem_limit_bytes=VMEM_LIMIT),
        name="odd_in",
    )(h, pre_norm, wa, wg, qn, wq, kvn, wkv, cos_t, sin_t)


ATTN_T = ODD_R
ATTN_HG = 4


def _attn_kernel(q_ref, qn_ref, k_ref, vt_ref, sg_ref, o_ref, m_ref, l_ref, acc_ref, st_ref):
    T = ATTN_T
    qi = pl.program_id(2)
    m_ref[...] = jnp.full(m_ref.shape, -jnp.inf, F32)
    l_ref[...] = jnp.zeros(l_ref.shape, F32)
    acc_ref[...] = jnp.zeros(acc_ref.shape, F32)

    def scores(hh, j, queries=q_ref):
        k = k_ref[0, hh, pl.ds(pl.multiple_of(j * T, T), T), :]
        return jnp.dot(k, queries[0, hh], preferred_element_type=F32)

    def consume(hh, j, st):
        m_old = m_ref[hh]
        m_new = jnp.maximum(m_old, jnp.max(st, axis=0, keepdims=True))
        alpha = jnp.exp2(m_old - m_new)
        p = jnp.exp2(st - m_new)
        l_ref[hh] = alpha * l_ref[hh] + jnp.sum(p, axis=0, keepdims=True)
        acc_ref[hh] = alpha * acc_ref[hh] + jnp.dot(
            vt_ref[0, hh, j], p.astype(BF16), preferred_element_type=F32)
        m_ref[hh] = m_new

    @pl.when(qi == 0)
    def _():
        for hh in range(ATTN_HG):
            st_ref[hh] = scores(hh, 0)

    def body(j, carry):
        for hh in range(ATTN_HG):
            st = st_ref[hh]
            st_ref[hh] = scores(hh, j + 1)
            consume(hh, j, st)
        return carry

    lax.fori_loop(0, qi, body, 0)

    H = T // 2
    causal = lax.broadcasted_iota(jnp.int32, (H, H), 0) <= lax.broadcasted_iota(jnp.int32, (H, H), 1)

    def consume_diagonal(hh, st):
        s_ee = jnp.where(causal, st[:H, :H], -jnp.inf)
        s_el = st[:H, H:]
        s_ll = jnp.where(causal, st[H:, H:], -jnp.inf)
        m_old = m_ref[hh]
        col_max = jnp.concatenate(
            [jnp.max(s_ee, axis=0, keepdims=True),
             jnp.maximum(jnp.max(s_el, axis=0, keepdims=True), jnp.max(s_ll, axis=0, keepdims=True))], axis=1)
        m_new = jnp.maximum(m_old, col_max)
        alpha = jnp.exp2(m_old - m_new)
        p_ee = jnp.exp2(s_ee - m_new[:, :H])
        p_el = jnp.exp2(s_el - m_new[:, H:])
        p_ll = jnp.exp2(s_ll - m_new[:, H:])
        col_sum = jnp.concatenate(
            [jnp.sum(p_ee, axis=0, keepdims=True),
             jnp.sum(p_el, axis=0, keepdims=True) + jnp.sum(p_ll, axis=0, keepdims=True)], axis=1)
        l_ref[hh] = alpha * l_ref[hh] + col_sum
        vt = vt_ref[0, hh, qi]
        pv = jnp.concatenate(
            [jnp.dot(vt[:, :H], p_ee.astype(BF16), preferred_element_type=F32),
             jnp.dot(vt, jnp.concatenate([p_el, p_ll], axis=0).astype(BF16), preferred_element_type=F32)], axis=1)
        acc_ref[hh] = alpha * acc_ref[hh] + pv
        m_ref[hh] = m_new

    for hh in range(ATTN_HG):
        st = st_ref[hh]
        st_ref[hh] = scores(hh, 0, qn_ref)
        consume_diagonal(hh, st)
    for hh in range(ATTN_HG):
        cols = slice(hh * MLA_DV, (hh + 1) * MLA_DV)
        o = (acc_ref[hh] * (1.0 / l_ref[hh])).T
        o_ref[0, :, cols] = (o * sg_ref[0, :, cols].astype(F32)).astype(BF16)


def _attn(q_cat, k_cat, vt, sg):
    T = ATTN_T
    HG = ATTN_HG
    nq = SEQ // T
    return pl.pallas_call(
        _attn_kernel,
        grid=(BATCH, MLA_HEADS // HG, nq),
        in_specs=[
            pl.BlockSpec((1, HG, MLA_QK_PAD, T), lambda b, h, i: (b, h, 0, i)),
            pl.BlockSpec((1, HG, MLA_QK_PAD, T), lambda b, h, i: (b, h, 0, jnp.minimum(i + 1, nq - 1))),
            pl.BlockSpec((1, HG, SEQ, MLA_QK_PAD), lambda b, h, i: (b, h, 0, 0)),
            pl.BlockSpec((1, HG, nq, MLA_DV, T), lambda b, h, i: (b, h, 0, 0, 0)),
            pl.BlockSpec((1, T, HG * MLA_DV), lambda b, h, i: (b, i, h)),
        ],
        out_specs=pl.BlockSpec((1, T, HG * MLA_DV), lambda b, h, i: (b, i, h)),
        out_shape=jax.ShapeDtypeStruct((BATCH, SEQ, MLA_V), BF16),
        scratch_shapes=[
            pltpu.VMEM((HG, 1, T), F32),
            pltpu.VMEM((HG, 1, T), F32),
            pltpu.VMEM((HG, MLA_DV, T), F32),
            pltpu.VMEM((HG, T, T), F32),
        ],
        compiler_params=pltpu.CompilerParams(
            dimension_semantics=("arbitrary", "arbitrary", "arbitrary"), vmem_limit_bytes=VMEM_LIMIT),
        name="odd_attn",
    )(q_cat, q_cat, k_cat, vt, sg.reshape(BATCH, SEQ, MLA_V))


ODD_OUT_R = 1024


def _odd_out_kernel(og_ref, h_ref, w_ref, pn_ref, out_ref):
    y = jnp.dot(og_ref[...], w_ref[...], preferred_element_type=F32)
    out_ref[...] = h_ref[...] + _rms(y, pn_ref[...])


def _odd_out(og, h, w, post_norm, layer):
    R = ODD_OUT_R
    row_map = lambda i: (i, 0)
    return pl.pallas_call(
        _odd_out_kernel,
        grid=(ROWS // R,),
        in_specs=[
            pl.BlockSpec((R, MLA_V), row_map),
            pl.BlockSpec((R, D_MODEL), row_map),
            pl.BlockSpec((None, MLA_V, D_MODEL), lambda i: (layer // 2, 0, 0)),
            pl.BlockSpec((None, 1, D_MODEL), lambda i: (layer, 0, 0)),
        ],
        out_specs=pl.BlockSpec((R, D_MODEL), row_map),
        out_shape=jax.ShapeDtypeStruct((ROWS, D_MODEL), F32),
        compiler_params=pltpu.CompilerParams(
            dimension_semantics=("arbitrary",), vmem_limit_bytes=VMEM_LIMIT),
        name="odd_out",
    )(og, h, w, post_norm)


PREP_ROW_BLOCKS = 4


def _lane_iota(rows):
    return lax.broadcasted_iota(jnp.int32, (rows, LANES), 1)


def _rope_spread(tile, lane, src_lo):
    x1 = tile if src_lo == 0 else pltpu.roll(tile, LANES - src_lo, 1)
    x2 = pltpu.roll(tile, (HALF_TILE - ROPE_HALF - src_lo) % LANES, 1)
    return jnp.where(lane < ROPE_HALF, x1, jnp.where((lane >= HALF_TILE) & (lane < HALF_TILE + ROPE_HALF), x2, 0.0))


def _prep_even_kernel(win_ref, wout_ref, win_o_ref, wout_o_ref):
    rows = win_ref.shape[0]
    lane = _lane_iota(rows)
    keep = (lane < ROPE_HALF) | (lane >= LANES - ROPE_HALF)
    for t in range(_V0 // LANES):
        cs = slice(t * LANES, (t + 1) * LANES)
        x = win_ref[:, cs]
        moved = jnp.where(lane < HALF_TILE, pltpu.roll(x, LANES - ROPE_HALF, 1), pltpu.roll(x, ROPE_HALF, 1))
        win_o_ref[:, cs] = jnp.where(keep, x, moved).astype(BF16)
    win_o_ref[:, _V0:_CB0] = win_ref[:, _V0:_CB0].astype(BF16)
    for t in range(CONV_W // LANES):
        for k in range(4):
            src = _CB0 + k * CONV_W + t * LANES
            dst = _CB0 + (4 * t + k) * LANES
            win_o_ref[:, dst:dst + LANES] = win_ref[:, src:src + LANES].astype(BF16)
    wout_o_ref[...] = wout_ref[...].astype(BF16)


def _prep_even(even_w_in, even_w_out):
    n = even_w_in.shape[0]
    rb = PREP_ROW_BLOCKS
    spec = lambda rows, cols: pl.BlockSpec((None, rows // rb, cols), lambda e, r: (e, r, 0))
    return pl.pallas_call(
        _prep_even_kernel,
        grid=(n, rb),
        in_specs=[spec(D_MODEL, EVEN_IN), spec(EVEN_MIX, D_MODEL)],
        out_specs=[spec(D_MODEL, EVEN_IN), spec(EVEN_MIX, D_MODEL)],
        out_shape=[jax.ShapeDtypeStruct((n, D_MODEL, EVEN_IN), BF16),
                   jax.ShapeDtypeStruct((n, EVEN_MIX, D_MODEL), BF16)],
        compiler_params=pltpu.CompilerParams(
            dimension_semantics=("arbitrary", "arbitrary"), vmem_limit_bytes=VMEM_LIMIT),
        name="prep_even",
    )(even_w_in, even_w_out)


def _prep_odd_kernel(win_ref, wqb_ref, wkvb_ref, wout_ref, wa_ref, wg_ref, wq_ref, wkv_ref, wo_ref):
    lat = Q_LORA + KV_LORA
    lane = _lane_iota(win_ref.shape[0])
    wa_ref[:, :lat] = win_ref[:, :lat].astype(BF16)
    wa_ref[:, lat:] = _rope_spread(win_ref[:, lat:lat + LANES], lane, 0).astype(BF16)
    gate0 = lat + MLA_ROPE - HALF_TILE
    for t in range(MLA_V // LANES):
        a = pltpu.roll(win_ref[:, gate0 + t * LANES:gate0 + (t + 1) * LANES], HALF_TILE, 1)
        if t + 1 < MLA_V // LANES:
            b = pltpu.roll(win_ref[:, gate0 + (t + 1) * LANES:gate0 + (t + 2) * LANES], HALF_TILE, 1)
        else:
            edge = win_ref[:, gate0 + (t + 1) * LANES:]
            b = jnp.concatenate([edge, edge], axis=1)
        wg_ref[:, t * LANES:(t + 1) * LANES] = jnp.where(lane < HALF_TILE, a, b).astype(BF16)
    lane = _lane_iota(wqb_ref.shape[0])
    for hp in range(MLA_HEADS // 2):
        t0, t1, t2 = (wqb_ref[:, (3 * hp + k) * LANES:(3 * hp + k + 1) * LANES] for k in range(3))
        ha, hb = 2 * hp, 2 * hp + 1
        wq_ref[:, ha * LANES:(ha + 1) * LANES] = t0.astype(BF16)
        wq_ref[:, hb * LANES:(hb + 1) * LANES] = jnp.where(
            lane < HALF_TILE, pltpu.roll(t1, HALF_TILE, 1), pltpu.roll(t2, HALF_TILE, 1)).astype(BF16)
        wq_ref[:, MLA_V + ha * LANES:MLA_V + (ha + 1) * LANES] = _rope_spread(t1, lane, 0).astype(BF16)
        wq_ref[:, MLA_V + hb * LANES:MLA_V + (hb + 1) * LANES] = _rope_spread(t2, lane, HALF_TILE).astype(BF16)
    for hd in range(MLA_HEADS):
        wkv_ref[:, hd * LANES:(hd + 1) * LANES] = wkvb_ref[:, 2 * hd * LANES:(2 * hd + 1) * LANES].astype(BF16)
        wkv_ref[:, MLA_V + hd * LANES:MLA_V + (hd + 1) * LANES] = (
            wkvb_ref[:, (2 * hd + 1) * LANES:(2 * hd + 2) * LANES].astype(BF16))
    wo_ref[...] = wout_ref[...].astype(BF16)


def _prep_odd(odd_w_in, w_qb, w_kvb, odd_w_out):
    n = odd_w_in.shape[0]
    rb = PREP_ROW_BLOCKS
    spec = lambda rows, cols: pl.BlockSpec((None, rows // rb, cols), lambda o, r: (o, r, 0))
    shape = lambda rows, cols: jax.ShapeDtypeStruct((n, rows, cols), BF16)
    return pl.pallas_call(
        _prep_odd_kernel,
        grid=(n, rb),
        in_specs=[spec(D_MODEL, ODD_IN), spec(Q_LORA, MLA_HEADS * MLA_QK),
                  spec(KV_LORA, 2 * MLA_V), spec(MLA_V, D_MODEL)],
        out_specs=[spec(D_MODEL, ODD_A), spec(D_MODEL, MLA_V), spec(Q_LORA, 2 * MLA_V),
                   spec(KV_LORA, 2 * MLA_V), spec(MLA_V, D_MODEL)],
        out_shape=[shape(D_MODEL, ODD_A), shape(D_MODEL, MLA_V), shape(Q_LORA, 2 * MLA_V),
                   shape(KV_LORA, 2 * MLA_V), shape(MLA_V, D_MODEL)],
        compiler_params=pltpu.CompilerParams(
            dimension_semantics=("arbitrary", "arbitrary"), vmem_limit_bytes=VMEM_LIMIT),
        name="prep_odd",
    )(odd_w_in, w_qb, w_kvb, odd_w_out)


def _rows3(a):
    return a.reshape(a.shape[0], 1, a.shape[1])


def kernel(x, positions, pre_norm, post_norm, even_w_in, even_conv_w, even_conv_b, ret_gn, even_w_out,
           odd_w_in, q_a_norm, w_qb, kv_a_norm, w_kvb, odd_w_out):
    cos_t, sin_t = _rope_tables(positions)
    tables = _retention_tables()
    ew_in, ew_out = _prep_even(even_w_in, even_w_out)
    wa, wg, wq, wkv, wo = _prep_odd(odd_w_in, w_qb, w_kvb, odd_w_out)
    pre_norm, post_norm, ret_gn, even_conv_b = map(_rows3, (pre_norm, post_norm, ret_gn, even_conv_b))
    q_a_norm, kv_a_norm = _rows3(q_a_norm), _rows3(kv_a_norm)
    h = x.reshape(ROWS, D_MODEL)
    for layer in range(DEPTH):
        if layer % 2 == 0:
            qkvg, cmix = _even_in(h, pre_norm, ew_in, cos_t, sin_t, even_conv_w, even_conv_b, layer)
            h = _even_mix(qkvg, cmix, h, ew_out, ret_gn, post_norm, tables, layer)
        else:
            q_cat, k_cat, vt, sg = _odd_in(h, pre_norm, wa, wg, q_a_norm, wq, kv_a_norm, wkv, cos_t, sin_t, layer)
            og = _attn(q_cat, k_cat, vt, sg)
            h = _odd_out(og.reshape(ROWS, MLA_V), h, wo, post_norm, layer)
    return h.reshape(BATCH, SEQ, D_MODEL)
```

```python
import math

import jax
import jax.numpy as jnp
import numpy as np
from jax import lax
from jax.experimental import pallas as pl
from jax.experimental.pallas import tpu as pltpu

D_MODEL = 1024
BATCH = 4
SEQ = 4096
DEPTH = 4
ROWS = BATCH * SEQ

RET_HEADS = 8
RET_DK = 64
RET_DV = 128
RET_CHUNK = 128
RET_PAIRS = RET_HEADS // 2
CONV_W = 1024
CONV_K = 3
MLA_HEADS = 8
MLA_NOPE = 128
MLA_ROPE = 64
MLA_DV = 128
Q_LORA = 384
KV_LORA = 256
ROPE_BASE = 10000.0
EPS = 1e-6

RET_QK = RET_HEADS * RET_DK
RET_V = RET_HEADS * RET_DV
EVEN_IN = 2 * RET_QK + 2 * RET_V + 4 * CONV_W
EVEN_MIX = RET_V + CONV_W
MLA_QK = MLA_NOPE + MLA_ROPE
MLA_V = MLA_HEADS * MLA_DV
MLA_SCALE = MLA_QK ** -0.5
LOG2E = math.log2(math.e)
MLA_QK_PAD = 256
ODD_IN = Q_LORA + KV_LORA + MLA_ROPE + MLA_V
ODD_A = Q_LORA + KV_LORA + 128

_Q0, _K0, _V0, _GR0, _CB0, _CC0, _CX0, _GC0 = 0, 512, 1024, 2048, 3072, 4096, 5120, 6144

LANES = 128
SUBLANES = 8
ROPE_HALF = 32
HALF_TILE = LANES // 2

VMEM_LIMIT = 56 * 1024 * 1024

BF16 = jnp.bfloat16
F32 = jnp.float32


def _silu(x):
    return (0.5 * x) * (1.0 + jnp.tanh(0.5 * x))


def _rms(x, g):
    ms = jnp.mean(x * x, axis=-1, keepdims=True)
    return x * lax.rsqrt(ms + EPS) * g


def _rope_tile(x, cos, sin_signed):
    return x * cos + pltpu.roll(x, HALF_TILE, 1) * sin_signed


ROPE_GROUPS = LANES // ROPE_HALF
ROPE_BLK = 512


def _rope_table_kernel(pos_ref, inv_ref, cos_ref, sin_ref):
    ang = pos_ref[...].astype(F32) * inv_ref[...]
    c = jnp.cos(ang)
    s = jnp.sin(ang)
    lane = lax.broadcasted_iota(jnp.int32, (ROPE_BLK, LANES), 1)
    low = lane < ROPE_HALF
    for g in range(ROPE_GROUPS):
        shift = (LANES - g * ROPE_HALF) % LANES
        cg = jnp.where(low, c if shift == 0 else pltpu.roll(c, shift, 1), 0.0)
        sg = jnp.where(low, s if shift == 0 else pltpu.roll(s, shift, 1), 0.0)
        cg = cg + pltpu.roll(cg, ROPE_HALF, 1)
        sg = sg + pltpu.roll(sg, ROPE_HALF, 1)
        cos_ref[g] = cg + pltpu.roll(cg, HALF_TILE, 1)
        sin_ref[g] = pltpu.roll(sg, HALF_TILE, 1) - sg


def _rope_tables(positions):
    inv = ROPE_BASE ** (-jnp.arange(0, 2 * ROPE_HALF, 2, dtype=F32) / (2 * ROPE_HALF))
    dense_rows = ROWS // ROPE_GROUPS
    pos_dense = jnp.repeat(positions.reshape(ROPE_GROUPS, dense_rows).T, ROPE_HALF, axis=1)
    inv_dense = jnp.tile(inv, ROPE_GROUPS).reshape(1, LANES)
    out_spec = pl.BlockSpec((ROPE_GROUPS, ROPE_BLK, LANES), lambda i: (0, i, 0))
    out = jax.ShapeDtypeStruct((ROPE_GROUPS, dense_rows, LANES), F32)
    cos_t, sin_t = pl.pallas_call(
        _rope_table_kernel,
        grid=(dense_rows // ROPE_BLK,),
        in_specs=[pl.BlockSpec((ROPE_BLK, LANES), lambda i: (i, 0)),
                  pl.BlockSpec((1, LANES), lambda i: (0, 0))],
        out_specs=[out_spec, out_spec],
        out_shape=[out, out],
        name="rope_table",
    )(pos_dense, inv_dense)
    return cos_t.reshape(ROWS, LANES), sin_t.reshape(ROWS, LANES)


EVEN_IN_TM = 2048
EVEN_IN_TN = 1024
EVEN_QKVG = 3 * EVEN_IN_TN
CONV_BLK = EVEN_IN_TN // 4
CONV_BLOCKS = CONV_W // CONV_BLK
EVEN_IN_STRIP = 256
CONV_STRIP = 128


def _even_in_kernel(x_ref, g_ref, w_ref, cos_ref, sin_ref, cw_ref, cb_ref,
                    q_ref, kt_ref, vg_ref, cmix_ref, u_ref, tail_ref):
    i = pl.program_id(0)
    j = pl.program_id(1)
    strips = [slice(r * EVEN_IN_STRIP, (r + 1) * EVEN_IN_STRIP) for r in range(EVEN_IN_TM // EVEN_IN_STRIP)]

    halves = [slice(0, EVEN_IN_TN // 2), slice(EVEN_IN_TN // 2, EVEN_IN_TN)]

    @pl.when(j == 0)
    def _():
        for rs in strips:
            u = _rms(x_ref[rs, :], g_ref[...]).astype(BF16)
            u_ref[rs, :] = u
            cos = cos_ref[rs, :]
            sin = sin_ref[rs, :]
            for ns in halves:
                acc = jnp.dot(u, w_ref[:, ns], preferred_element_type=F32)
                for c in range((ns.stop - ns.start) // LANES):
                    cs = slice(c * LANES, (c + 1) * LANES)
                    y = _rope_tile(acc[:, cs], cos, sin)
                    if ns.start >= _K0:
                        kt_ref[cs, rs] = (y * (RET_DK ** -0.5)).T.astype(BF16)
                    else:
                        q_ref[rs, cs] = y.astype(BF16)

    @pl.when(j == 1)
    def _():
        vg_ref[...] = jnp.dot(u_ref[...], w_ref[...], preferred_element_type=F32).astype(BF16)

    @pl.when(j == 2)
    def _():
        for rs in strips:
            for ns in halves:
                acc = jnp.dot(u_ref[rs, :], w_ref[:, ns], preferred_element_type=F32)
                vg_ref[rs, ns] = _silu(acc).astype(BF16)

    @pl.when(j >= 3)
    def _():
        slot = j - 3
        sequence_start = (i % (SEQ // EVEN_IN_TM)) == 0
        lane_tiles = [slice(ct * LANES, (ct + 1) * LANES) for ct in range(CONV_BLK // LANES)]
        halos = [jnp.where(sequence_start, 0.0, tail_ref[slot, :, cs]) for cs in lane_tiles]
        for rs in strips:
            for ct, cs in enumerate(lane_tiles):
                acc = jnp.dot(u_ref[rs, :], w_ref[:, ct * 4 * LANES:(ct + 1) * 4 * LANES],
                              preferred_element_type=F32)
                for r in range(EVEN_IN_STRIP // CONV_STRIP):
                    rr = slice(r * CONV_STRIP, (r + 1) * CONV_STRIP)
                    part = lambda k: acc[rr, k * LANES:(k + 1) * LANES]
                    prod = part(1) * part(2)
                    ext = jnp.concatenate([halos[ct], prod], axis=0)
                    conv = cb_ref[:, cs] + prod * cw_ref[CONV_K - 1:CONV_K, cs]
                    for tap in range(CONV_K - 1):
                        off = SUBLANES - (CONV_K - 1) + tap
                        conv = conv + ext[off:off + CONV_STRIP, :] * cw_ref[tap:tap + 1, cs]
                    halos[ct] = prod[CONV_STRIP - SUBLANES:, :]
                    out_rows = slice(rs.start + rr.start, rs.start + rr.stop)
                    cmix_ref[out_rows, cs] = (part(0) * conv * _silu(part(3))).astype(BF16)
        for ct, cs in enumerate(lane_tiles):
            tail_ref[slot, :, cs] = halos[ct]


def _even_in(h, pre_norm, w, cos_t, sin_t, conv_w, conv_b, layer):
    tm, tn = EVEN_IN_TM, EVEN_IN_TN
    e = layer // 2
    qkvg_blocks = EVEN_QKVG // tn
    vg_blocks = qkvg_blocks - 1
    conv_map = lambda i, j: (e, 0, jnp.maximum(j - qkvg_blocks, 0))
    return pl.pallas_call(
        _even_in_kernel,
        grid=(ROWS // tm, EVEN_IN // tn),
        in_specs=[
            pl.BlockSpec((tm, D_MODEL), lambda i, j: (i, 0)),
            pl.BlockSpec((None, 1, D_MODEL), lambda i, j: (layer, 0, 0)),
            pl.BlockSpec((None, D_MODEL, tn), lambda i, j: (e, 0, j)),
            pl.BlockSpec((tm, LANES), lambda i, j: (i, 0)),
            pl.BlockSpec((tm, LANES), lambda i, j: (i, 0)),
            pl.BlockSpec((None, CONV_K, CONV_BLK), conv_map),
            pl.BlockSpec((None, 1, CONV_BLK), conv_map),
        ],
        out_specs=[
            pl.BlockSpec((tm, RET_QK), lambda i, j: (i, 0)),
            pl.BlockSpec((RET_QK, tm), lambda i, j: (0, i)),
            pl.BlockSpec((tm, tn), lambda i, j: (i, jnp.clip(j - 1, 0, vg_blocks - 1))),
            pl.BlockSpec((tm, CONV_BLK), lambda i, j: (i, jnp.maximum(j - qkvg_blocks, 0))),
        ],
        out_shape=[
            jax.ShapeDtypeStruct((ROWS, RET_QK), BF16),
            jax.ShapeDtypeStruct((RET_QK, ROWS), BF16),
            jax.ShapeDtypeStruct((ROWS, vg_blocks * tn), BF16),
            jax.ShapeDtypeStruct((ROWS, CONV_W), BF16),
        ],
        scratch_shapes=[
            pltpu.VMEM((tm, D_MODEL), BF16),
            pltpu.VMEM((CONV_BLOCKS, SUBLANES, CONV_BLK), F32),
        ],
        compiler_params=pltpu.CompilerParams(
            dimension_semantics=("arbitrary", "arbitrary"), vmem_limit_bytes=VMEM_LIMIT),
        name="even_in",
    )(h, pre_norm, w, cos_t, sin_t, conv_w, conv_b)


EVEN_MIX_R = 512
EVEN_MIX_STEPS = ROWS // EVEN_MIX_R
PROJ_PIECE = 256


def _even_mix_block(q_ref, kt_ref, vg_ref, cmix_ref, h_ref, wout_ref, gn_ref, pn_ref,
                    intra_ref, xi_ref, zeta_ref, gam_ref, bm_ref,
                    o_ref, state_ref, mix_prev_ref, mix_ref):
    C = RET_CHUNK
    chunks = EVEN_MIX_R // C

    row_k = lax.broadcasted_iota(jnp.int32, (LANES, C), 0)
    first_head = (row_k % HALF_TILE) < ROPE_HALF
    lane_v = lax.broadcasted_iota(jnp.int32, (C, 2 * RET_DV), 1)
    zero_k = jnp.zeros((LANES, C), BF16)
    zero_v = jnp.zeros((C, 2 * RET_DV), BF16)

    y_prev = []

    def project_piece():
        ns = slice(len(y_prev) * PROJ_PIECE, (len(y_prev) + 1) * PROJ_PIECE)
        y_prev.append(jnp.dot(mix_prev_ref[...], wout_ref[:RET_V, ns], preferred_element_type=F32)
                      + jnp.dot(cmix_ref[...], wout_ref[RET_V:, ns], preferred_element_type=F32))

    stages_per_piece = chunks * RET_PAIRS * PROJ_PIECE // D_MODEL
    for c, j in [(c, p) for c in range(chunks) for p in range(RET_PAIRS)]:
        rows = slice(c * C, (c + 1) * C)
        if (c * RET_PAIRS + j) % stages_per_piece == 0:
            project_piece()
        qp = q_ref[rows, j * LANES:(j + 1) * LANES]
        kt = kt_ref[j * LANES:(j + 1) * LANES, rows]
        vp = vg_ref[rows, j * 2 * RET_DV:(j + 1) * 2 * RET_DV]
        kt_heads = jnp.concatenate(
            [jnp.where(first_head, kt, zero_k), jnp.where(first_head, zero_k, kt)], axis=1)
        sc = jnp.dot(qp, kt_heads, preferred_element_type=F32)
        pm = (sc * intra_ref[j]).astype(BF16)
        vblk = jnp.concatenate(
            [jnp.where(lane_v < RET_DV, vp, zero_v), jnp.where(lane_v >= RET_DV, vp, zero_v)], axis=0)
        st = state_ref[j]
        r = (jnp.dot(pm, vblk, preferred_element_type=F32)
             + jnp.dot(qp, st.astype(BF16), preferred_element_type=F32) * xi_ref[j])
        kz = (kt.astype(F32) * zeta_ref[j]).astype(BF16)
        kv = jnp.dot(kz, vp, preferred_element_type=F32)
        state_ref[j] = st * gam_ref[j] + kv * bm_ref[...]
        for hh in range(2):
            hd = 2 * j + hh
            cols = slice(hd * RET_DV, (hd + 1) * RET_DV)
            y = _rms(r[:, hh * RET_DV:(hh + 1) * RET_DV], gn_ref[:, cols])
            gate = vg_ref[rows, RET_V + hd * RET_DV:RET_V + (hd + 1) * RET_DV].astype(F32)
            mix_ref[rows, cols] = (y * gate).astype(BF16)

    o_ref[...] = h_ref[...] + _rms(jnp.concatenate(y_prev, axis=1), pn_ref[...])


def _even_mix_kernel(q_ref, kt_ref, vg_ref, cmix_ref, h_ref, wout_ref, gn_ref, pn_ref,
                     intra_ref, xi_ref, zeta_ref, gam_ref, bm_ref,
                     o_ref, state_ref, mix_a_ref, mix_b_ref):
    t = pl.program_id(0)
    args = (q_ref, kt_ref, vg_ref, cmix_ref, h_ref, wout_ref, gn_ref, pn_ref,
            intra_ref, xi_ref, zeta_ref, gam_ref, bm_ref, o_ref, state_ref)

    @pl.when(t % (SEQ // EVEN_MIX_R) == 0)
    def _():
        state_ref[...] = jnp.zeros_like(state_ref)

    @pl.when(t == 0)
    def _():
        mix_b_ref[...] = jnp.zeros_like(mix_b_ref)

    @pl.when(t % 2 == 0)
    def _():
        _even_mix_block(*args, mix_b_ref, mix_a_ref)

    @pl.when(t % 2 == 1)
    def _():
        _even_mix_block(*args, mix_a_ref, mix_b_ref)


def _retention_tables():
    h, c = RET_HEADS, RET_CHUNK
    log_gamma = np.log1p(-np.exp2(-5.0 - np.arange(h, dtype=np.float64)))
    i = np.arange(c, dtype=np.float64)
    rel = i[:, None] - i[None, :]
    intra = np.where(rel >= 0, np.exp(log_gamma[:, None, None] * np.maximum(rel, 0.0)), 0.0)
    xi = np.exp(log_gamma[:, None] * (i + 1.0))
    zeta = np.exp(log_gamma[:, None] * (c - 1.0 - i))
    gamma_c = np.exp(log_gamma * c)
    pair = np.arange(RET_PAIRS)
    intra_p = intra.reshape(RET_PAIRS, 2, c, c).transpose(0, 2, 1, 3).reshape(RET_PAIRS, c, 2 * c)
    xi_p = np.repeat(xi.reshape(RET_PAIRS, 2, c).transpose(0, 2, 1), RET_DV, axis=2)
    second = ((np.arange(2 * RET_DK) % HALF_TILE) >= ROPE_HALF).astype(np.int64)
    zeta_p = zeta.reshape(RET_PAIRS, 2, c)[pair[:, None, None], second[None, :, None], np.arange(c)[None, None, :]]
    gam_rows = gamma_c.reshape(RET_PAIRS, 2)[:, second]
    gam_p = np.broadcast_to(gam_rows[:, :, None], (RET_PAIRS, 2 * RET_DK, 2 * RET_DV))
    col_head = np.arange(2 * RET_DV) // RET_DV
    bm = (second[:, None] == col_head[None, :])
    as_f32 = lambda a: jnp.asarray(np.ascontiguousarray(a, dtype=np.float32))
    return as_f32(intra_p), as_f32(xi_p), as_f32(zeta_p), as_f32(gam_p), as_f32(bm)


def _even_mix(q, kt, vg, cmix, h, w_out, gn, post_norm, tables, layer):
    R = EVEN_MIX_R
    nt = EVEN_MIX_STEPS
    e = layer // 2
    intra_p, xi_p, zeta_p, gam_p, bm = tables
    mix_map = lambda t: (jnp.minimum(t, nt - 1), 0)
    row_map = lambda t: (jnp.maximum(t - 1, 0), 0)
    const2 = lambda t: (0, 0)
    const3 = lambda t: (0, 0, 0)
    return pl.pallas_call(
        _even_mix_kernel,
        grid=(nt + 1,),
        in_specs=[
            pl.BlockSpec((R, RET_QK), mix_map),
            pl.BlockSpec((RET_QK, R), lambda t: (0, jnp.minimum(t, nt - 1))),
            pl.BlockSpec((R, 2 * RET_V), mix_map),
            pl.BlockSpec((R, CONV_W), row_map),
            pl.BlockSpec((R, D_MODEL), row_map),
            pl.BlockSpec((None, EVEN_MIX, D_MODEL), lambda t: (e, 0, 0)),
            pl.BlockSpec((None, 1, RET_V), lambda t: (e, 0, 0)),
            pl.BlockSpec((None, 1, D_MODEL), lambda t: (layer, 0, 0)),
            pl.BlockSpec((RET_PAIRS, RET_CHUNK, 2 * RET_CHUNK), const3),
            pl.BlockSpec((RET_PAIRS, RET_CHUNK, 2 * RET_DV), const3),
            pl.BlockSpec((RET_PAIRS, RET_CHUNK, 2 * RET_DK), const3),
            pl.BlockSpec((RET_PAIRS, 2 * RET_DK, 2 * RET_DV), const3),
            pl.BlockSpec((2 * RET_DK, 2 * RET_DV), const2),
        ],
        out_specs=pl.BlockSpec((R, D_MODEL), row_map),
        out_shape=jax.ShapeDtypeStruct((ROWS, D_MODEL), F32),
        scratch_shapes=[
            pltpu.VMEM((RET_PAIRS, 2 * RET_DK, 2 * RET_DV), F32),
            pltpu.VMEM((R, RET_V), BF16),
            pltpu.VMEM((R, RET_V), BF16),
        ],
        compiler_params=pltpu.CompilerParams(
            dimension_semantics=("arbitrary",), vmem_limit_bytes=VMEM_LIMIT),
        name="even_mix",
    )(q, kt, vg, cmix, h, w_out, gn, post_norm, intra_p, xi_p, zeta_p, gam_p, bm)


ODD_R = 512


def _odd_in_kernel(x_ref, g_ref, wa_ref, wg_ref, qn_ref, wq_ref, kvn_ref, wkv_ref, cos_ref, sin_ref,
                   q_ref, k_ref, vt_ref, sg_ref):
    u = _rms(x_ref[...], g_ref[...]).astype(BF16)
    a = jnp.dot(u, wa_ref[...], preferred_element_type=F32)
    cq = a[:, :Q_LORA]
    ckv = a[:, Q_LORA:Q_LORA + KV_LORA]
    kr = a[:, Q_LORA + KV_LORA:]
    q = jnp.dot(_rms(cq, qn_ref[...]).astype(BF16), wq_ref[...], preferred_element_type=F32) * (MLA_SCALE * LOG2E)
    kv = jnp.dot(_rms(ckv, kvn_ref[...]).astype(BF16), wkv_ref[...], preferred_element_type=F32)
    cos = cos_ref[...]
    sin = sin_ref[...]
    kr_rot = _rope_tile(kr, cos, sin).astype(BF16)
    for hd in range(MLA_HEADS):
        lo, hi = hd * LANES, (hd + 1) * LANES
        q_ref[0, hd, 0:LANES, :] = q[:, lo:hi].T.astype(BF16)
        q_ref[0, hd, LANES:2 * LANES, :] = _rope_tile(q[:, MLA_V + lo:MLA_V + hi], cos, sin).T.astype(BF16)
        k_ref[0, hd, :, 0:LANES] = kv[:, lo:hi].astype(BF16)
        k_ref[0, hd, :, LANES:2 * LANES] = kr_rot
        vt_ref[0, hd, 0] = kv[:, MLA_V + lo:MLA_V + hi].T.astype(BF16)
    gate = jnp.dot(u, wg_ref[...], preferred_element_type=F32)
    sg_ref[...] = _silu(gate).astype(BF16)


def _odd_in(h, pre_norm, wa, wg, qn, wq, kvn, wkv, cos_t, sin_t, layer):
    R = ODD_R
    nb = SEQ // R
    o = layer // 2
    row_map = lambda b, i: (b * nb + i, 0)
    odd_map = lambda b, i: (o, 0, 0)
    return pl.pallas_call(
        _odd_in_kernel,
        grid=(BATCH, nb),
        in_specs=[
            pl.BlockSpec((R, D_MODEL), row_map),
            pl.BlockSpec((None, 1, D_MODEL), lambda b, i: (layer, 0, 0)),
            pl.BlockSpec((None, D_MODEL, ODD_A), odd_map),
            pl.BlockSpec((None, D_MODEL, MLA_V), odd_map),
            pl.BlockSpec((None, 1, Q_LORA), odd_map),
            pl.BlockSpec((None, Q_LORA, 2 * MLA_V), odd_map),
            pl.BlockSpec((None, 1, KV_LORA), odd_map),
            pl.BlockSpec((None, KV_LORA, 2 * MLA_V), odd_map),
            pl.BlockSpec((R, LANES), row_map),
            pl.BlockSpec((R, LANES), row_map),
        ],
        out_specs=[
            pl.BlockSpec((1, MLA_HEADS, MLA_QK_PAD, R), lambda b, i: (b, 0, 0, i)),
            pl.BlockSpec((1, MLA_HEADS, R, MLA_QK_PAD), lambda b, i: (b, 0, i, 0)),
            pl.BlockSpec((1, MLA_HEADS, 1, MLA_DV, R), lambda b, i: (b, 0, i, 0, 0)),
            pl.BlockSpec((R, MLA_V), row_map),
        ],
        out_shape=[
            jax.ShapeDtypeStruct((BATCH, MLA_HEADS, MLA_QK_PAD, SEQ), BF16),
            jax.ShapeDtypeStruct((BATCH, MLA_HEADS, SEQ, MLA_QK_PAD), BF16),
            jax.ShapeDtypeStruct((BATCH, MLA_HEADS, nb, MLA_DV, R), BF16),
            jax.ShapeDtypeStruct((ROWS, MLA_V), BF16),
        ],
        compiler_params=pltpu.CompilerParams(
            dimension_semantics=("arbitrary", "arbitrary"), vmem_limit_bytes=VMEM_LIMIT),
        name="odd_in",
    )(h, pre_norm, wa, wg, qn, wq, kvn, wkv, cos_t, sin_t)


ATTN_T = ODD_R
ATTN_HG = 4


def _attn_kernel(q_ref, qn_ref, k_ref, vt_ref, sg_ref, o_ref, m_ref, l_ref, acc_ref, st_ref):
    T = ATTN_T
    qi = pl.program_id(2)
    m_ref[...] = jnp.full(m_ref.shape, -jnp.inf, F32)
    l_ref[...] = jnp.zeros(l_ref.shape, F32)
    acc_ref[...] = jnp.zeros(acc_ref.shape, F32)

    def scores(hh, j, queries=q_ref):
        k = k_ref[0, hh, pl.ds(pl.multiple_of(j * T, T), T), :]
        return jnp.dot(k, queries[0, hh], preferred_element_type=F32)

    def consume(hh, j, st):
        m_old = m_ref[hh]
        m_new = jnp.maximum(m_old, jnp.max(st, axis=0, keepdims=True))
        alpha = jnp.exp2(m_old - m_new)
        p = jnp.exp2(st - m_new)
        l_ref[hh] = alpha * l_ref[hh] + jnp.sum(p, axis=0, keepdims=True)
        acc_ref[hh] = alpha * acc_ref[hh] + jnp.dot(
            vt_ref[0, hh, j], p.astype(BF16), preferred_element_type=F32)
        m_ref[hh] = m_new

    @pl.when(qi == 0)
    def _():
        for hh in range(ATTN_HG):
            st_ref[hh] = scores(hh, 0)

    def body(j, carry):
        for hh in range(ATTN_HG):
            st = st_ref[hh]
            st_ref[hh] = scores(hh, j + 1)
            consume(hh, j, st)
        return carry

    lax.fori_loop(0, qi, body, 0)

    H = T // 2
    causal = lax.broadcasted_iota(jnp.int32, (H, H), 0) <= lax.broadcasted_iota(jnp.int32, (H, H), 1)

    def consume_diagonal(hh, st):
        s_ee = jnp.where(causal, st[:H, :H], -jnp.inf)
        s_el = st[:H, H:]
        s_ll = jnp.where(causal, st[H:, H:], -jnp.inf)
        m_old = m_ref[hh]
        col_max = jnp.concatenate(
            [jnp.max(s_ee, axis=0, keepdims=True),
             jnp.maximum(jnp.max(s_el, axis=0, keepdims=True), jnp.max(s_ll, axis=0, keepdims=True))], axis=1)
        m_new = jnp.maximum(m_old, col_max)
        alpha = jnp.exp2(m_old - m_new)
        p_ee = jnp.exp2(s_ee - m_new[:, :H])
        p_el = jnp.exp2(s_el - m_new[:, H:])
        p_ll = jnp.exp2(s_ll - m_new[:, H:])
        col_sum = jnp.concatenate(
            [jnp.sum(p_ee, axis=0, keepdims=True),
             jnp.sum(p_el, axis=0, keepdims=True) + jnp.sum(p_ll, axis=0, keepdims=True)], axis=1)
        l_ref[hh] = alpha * l_ref[hh] + col_sum
        vt = vt_ref[0, hh, qi]
        pv = jnp.concatenate(
            [jnp.dot(vt[:, :H], p_ee.astype(BF16), preferred_element_type=F32),
             jnp.dot(vt, jnp.concatenate([p_el, p_ll], axis=0).astype(BF16), preferred_element_type=F32)], axis=1)
        acc_ref[hh] = alpha * acc_ref[hh] + pv
        m_ref[hh] = m_new

    for hh in range(ATTN_HG):
        st = st_ref[hh]
        st_ref[hh] = scores(hh, 0, qn_ref)
        consume_diagonal(hh, st)
    for hh in range(ATTN_HG):
        cols = slice(hh * MLA_DV, (hh + 1) * MLA_DV)
        o = (acc_ref[hh] * (1.0 / l_ref[hh])).T
        o_ref[0, :, cols] = (o * sg_ref[0, :, cols].astype(F32)).astype(BF16)


def _attn(q_cat, k_cat, vt, sg):
    T = ATTN_T
    HG = ATTN_HG
    nq = SEQ // T
    return pl.pallas_call(
        _attn_kernel,
        grid=(BATCH, MLA_HEADS // HG, nq),
        in_specs=[
            pl.BlockSpec((1, HG, MLA_QK_PAD, T), lambda b, h, i: (b, h, 0, i)),
            pl.BlockSpec((1, HG, MLA_QK_PAD, T), lambda b, h, i: (b, h, 0, jnp.minimum(i + 1, nq - 1))),
            pl.BlockSpec((1, HG, SEQ, MLA_QK_PAD), lambda b, h, i: (b, h, 0, 0)),
            pl.BlockSpec((1, HG, nq, MLA_DV, T), lambda b, h, i: (b, h, 0, 0, 0)),
            pl.BlockSpec((1, T, HG * MLA_DV), lambda b, h, i: (b, i, h)),
        ],
        out_specs=pl.BlockSpec((1, T, HG * MLA_DV), lambda b, h, i: (b, i, h)),
        out_shape=jax.ShapeDtypeStruct((BATCH, SEQ, MLA_V), BF16),
        scratch_shapes=[
            pltpu.VMEM((HG, 1, T), F32),
            pltpu.VMEM((HG, 1, T), F32),
            pltpu.VMEM((HG, MLA_DV, T), F32),
            pltpu.VMEM((HG, T, T), F32),
        ],
        compiler_params=pltpu.CompilerParams(
            dimension_semantics=("arbitrary", "arbitrary", "arbitrary"), vmem_limit_bytes=VMEM_LIMIT),
        name="odd_attn",
    )(q_cat, q_cat, k_cat, vt, sg.reshape(BATCH, SEQ, MLA_V))


ODD_OUT_R = 2048


def _odd_out_kernel(og_ref, h_ref, w_ref, pn_ref, out_ref):
    y = jnp.dot(og_ref[...], w_ref[...], preferred_element_type=F32)
    out_ref[...] = h_ref[...] + _rms(y, pn_ref[...])


def _odd_out(og, h, w, post_norm, layer):
    R = ODD_OUT_R
    row_map = lambda i: (i, 0)
    return pl.pallas_call(
        _odd_out_kernel,
        grid=(ROWS // R,),
        in_specs=[
            pl.BlockSpec((R, MLA_V), row_map),
            pl.BlockSpec((R, D_MODEL), row_map),
            pl.BlockSpec((None, MLA_V, D_MODEL), lambda i: (layer // 2, 0, 0)),
            pl.BlockSpec((None, 1, D_MODEL), lambda i: (layer, 0, 0)),
        ],
        out_specs=pl.BlockSpec((R, D_MODEL), row_map),
        out_shape=jax.ShapeDtypeStruct((ROWS, D_MODEL), F32),
        compiler_params=pltpu.CompilerParams(
            dimension_semantics=("arbitrary",), vmem_limit_bytes=VMEM_LIMIT),
        name="odd_out",
    )(og, h, w, post_norm)


PREP_ROW_BLOCKS = 4


def _lane_iota(rows):
    return lax.broadcasted_iota(jnp.int32, (rows, LANES), 1)


def _rope_spread(tile, lane, src_lo):
    x1 = tile if src_lo == 0 else pltpu.roll(tile, LANES - src_lo, 1)
    x2 = pltpu.roll(tile, (HALF_TILE - ROPE_HALF - src_lo) % LANES, 1)
    return jnp.where(lane < ROPE_HALF, x1, jnp.where((lane >= HALF_TILE) & (lane < HALF_TILE + ROPE_HALF), x2, 0.0))


def _prep_even_kernel(win_ref, wout_ref, win_o_ref, wout_o_ref):
    rows = win_ref.shape[0]
    lane = _lane_iota(rows)
    keep = (lane < ROPE_HALF) | (lane >= LANES - ROPE_HALF)
    for t in range(_V0 // LANES):
        cs = slice(t * LANES, (t + 1) * LANES)
        x = win_ref[:, cs]
        moved = jnp.where(lane < HALF_TILE, pltpu.roll(x, LANES - ROPE_HALF, 1), pltpu.roll(x, ROPE_HALF, 1))
        win_o_ref[:, cs] = jnp.where(keep, x, moved).astype(BF16)
    win_o_ref[:, _V0:_CB0] = win_ref[:, _V0:_CB0].astype(BF16)
    for t in range(CONV_W // LANES):
        for k in range(4):
            src = _CB0 + k * CONV_W + t * LANES
            dst = _CB0 + (4 * t + k) * LANES
            win_o_ref[:, dst:dst + LANES] = win_ref[:, src:src + LANES].astype(BF16)
    wout_o_ref[...] = wout_ref[...].astype(BF16)


def _prep_even(even_w_in, even_w_out):
    n = even_w_in.shape[0]
    rb = PREP_ROW_BLOCKS
    spec = lambda rows, cols: pl.BlockSpec((None, rows // rb, cols), lambda e, r: (e, r, 0))
    return pl.pallas_call(
        _prep_even_kernel,
        grid=(n, rb),
        in_specs=[spec(D_MODEL, EVEN_IN), spec(EVEN_MIX, D_MODEL)],
        out_specs=[spec(D_MODEL, EVEN_IN), spec(EVEN_MIX, D_MODEL)],
        out_shape=[jax.ShapeDtypeStruct((n, D_MODEL, EVEN_IN), BF16),
                   jax.ShapeDtypeStruct((n, EVEN_MIX, D_MODEL), BF16)],
        compiler_params=pltpu.CompilerParams(
            dimension_semantics=("arbitrary", "arbitrary"), vmem_limit_bytes=VMEM_LIMIT),
        name="prep_even",
    )(even_w_in, even_w_out)


def _prep_odd_kernel(win_ref, wqb_ref, wkvb_ref, wout_ref, wa_ref, wg_ref, wq_ref, wkv_ref, wo_ref):
    lat = Q_LORA + KV_LORA
    lane = _lane_iota(win_ref.shape[0])
    wa_ref[:, :lat] = win_ref[:, :lat].astype(BF16)
    wa_ref[:, lat:] = _rope_spread(win_ref[:, lat:lat + LANES], lane, 0).astype(BF16)
    gate0 = lat + MLA_ROPE - HALF_TILE
    for t in range(MLA_V // LANES):
        a = pltpu.roll(win_ref[:, gate0 + t * LANES:gate0 + (t + 1) * LANES], HALF_TILE, 1)
        if t + 1 < MLA_V // LANES:
            b = pltpu.roll(win_ref[:, gate0 + (t + 1) * LANES:gate0 + (t + 2) * LANES], HALF_TILE, 1)
        else:
            edge = win_ref[:, gate0 + (t + 1) * LANES:]
            b = jnp.concatenate([edge, edge], axis=1)
        wg_ref[:, t * LANES:(t + 1) * LANES] = jnp.where(lane < HALF_TILE, a, b).astype(BF16)
    lane = _lane_iota(wqb_ref.shape[0])
    for hp in range(MLA_HEADS // 2):
        t0, t1, t2 = (wqb_ref[:, (3 * hp + k) * LANES:(3 * hp + k + 1) * LANES] for k in range(3))
        ha, hb = 2 * hp, 2 * hp + 1
        wq_ref[:, ha * LANES:(ha + 1) * LANES] = t0.astype(BF16)
        wq_ref[:, hb * LANES:(hb + 1) * LANES] = jnp.where(
            lane < HALF_TILE, pltpu.roll(t1, HALF_TILE, 1), pltpu.roll(t2, HALF_TILE, 1)).astype(BF16)
        wq_ref[:, MLA_V + ha * LANES:MLA_V + (ha + 1) * LANES] = _rope_spread(t1, lane, 0).astype(BF16)
        wq_ref[:, MLA_V + hb * LANES:MLA_V + (hb + 1) * LANES] = _rope_spread(t2, lane, HALF_TILE).astype(BF16)
    for hd in range(MLA_HEADS):
        wkv_ref[:, hd * LANES:(hd + 1) * LANES] = wkvb_ref[:, 2 * hd * LANES:(2 * hd + 1) * LANES].astype(BF16)
        wkv_ref[:, MLA_V + hd * LANES:MLA_V + (hd + 1) * LANES] = (
            wkvb_ref[:, (2 * hd + 1) * LANES:(2 * hd + 2) * LANES].astype(BF16))
    wo_ref[...] = wout_ref[...].astype(BF16)


def _prep_odd(odd_w_in, w_qb, w_kvb, odd_w_out):
    n = odd_w_in.shape[0]
    rb = PREP_ROW_BLOCKS
    spec = lambda rows, cols: pl.BlockSpec((None, rows // rb, cols), lambda o, r: (o, r, 0))
    shape = lambda rows, cols: jax.ShapeDtypeStruct((n, rows, cols), BF16)
    return pl.pallas_call(
        _prep_odd_kernel,
        grid=(n, rb),
        in_specs=[spec(D_MODEL, ODD_IN), spec(Q_LORA, MLA_HEADS * MLA_QK),
                  spec(KV_LORA, 2 * MLA_V), spec(MLA_V, D_MODEL)],
        out_specs=[spec(D_MODEL, ODD_A), spec(D_MODEL, MLA_V), spec(Q_LORA, 2 * MLA_V),
                   spec(KV_LORA, 2 * MLA_V), spec(MLA_V, D_MODEL)],
        out_shape=[shape(D_MODEL, ODD_A), shape(D_MODEL, MLA_V), shape(Q_LORA, 2 * MLA_V),
                   shape(KV_LORA, 2 * MLA_V), shape(MLA_V, D_MODEL)],
        compiler_params=pltpu.CompilerParams(
            dimension_semantics=("arbitrary", "arbitrary"), vmem_limit_bytes=VMEM_LIMIT),
        name="prep_odd",
    )(odd_w_in, w_qb, w_kvb, odd_w_out)


def _rows3(a):
    return a.reshape(a.shape[0], 1, a.shape[1])


def kernel(x, positions, pre_norm, post_norm, even_w_in, even_conv_w, even_conv_b, ret_gn, even_w_out,
           odd_w_in, q_a_norm, w_qb, kv_a_norm, w_kvb, odd_w_out):
    cos_t, sin_t = _rope_tables(positions)
    tables = _retention_tables()
    ew_in, ew_out = _prep_even(even_w_in, even_w_out)
    wa, wg, wq, wkv, wo = _prep_odd(odd_w_in, w_qb, w_kvb, odd_w_out)
    pre_norm, post_norm, ret_gn, even_conv_b = map(_rows3, (pre_norm, post_norm, ret_gn, even_conv_b))
    q_a_norm, kv_a_norm = _rows3(q_a_norm), _rows3(kv_a_norm)
    h = x.reshape(ROWS, D_MODEL)
    for layer in range(DEPTH):
        if layer % 2 == 0:
            q, kt, vg, cmix = _even_in(h, pre_norm, ew_in, cos_t, sin_t, even_conv_w, even_conv_b, layer)
            h = _even_mix(q, kt, vg, cmix, h, ew_out, ret_gn, post_norm, tables, layer)
        else:
            q_cat, k_cat, vt, sg = _odd_in(h, pre_norm, wa, wg, q_a_norm, wq, kv_a_norm, wkv, cos_t, sin_t, layer)
            og = _attn(q_cat, k_cat, vt, sg)
            h = _odd_out(og.reshape(ROWS, MLA_V), h, wo, post_norm, layer)
    return h.reshape(BATCH, SEQ, D_MODEL)
```

```python
import math

import jax
import jax.numpy as jnp
import numpy as np
from jax import lax
from jax.experimental import pallas as pl
from jax.experimental.pallas import tpu as pltpu

D_MODEL = 1024
BATCH = 4
SEQ = 4096
DEPTH = 4
ROWS = BATCH * SEQ

RET_HEADS = 8
RET_DK = 64
RET_DV = 128
RET_CHUNK = 128
RET_PAIRS = RET_HEADS // 2
CONV_W = 1024
CONV_K = 3
MLA_HEADS = 8
MLA_NOPE = 128
MLA_ROPE = 64
MLA_DV = 128
Q_LORA = 384
KV_LORA = 256
ROPE_BASE = 10000.0
EPS = 1e-6

RET_QK = RET_HEADS * RET_DK
RET_V = RET_HEADS * RET_DV
EVEN_IN = 2 * RET_QK + 2 * RET_V + 4 * CONV_W
EVEN_MIX = RET_V + CONV_W
MLA_QK = MLA_NOPE + MLA_ROPE
MLA_V = MLA_HEADS * MLA_DV
MLA_SCALE = MLA_QK ** -0.5
LOG2E = math.log2(math.e)
MLA_QK_PAD = 256
ODD_IN = Q_LORA + KV_LORA + MLA_ROPE + MLA_V
ODD_A = Q_LORA + KV_LORA + 128
ODD_WQ = MLA_V + (MLA_HEADS // 2) * 128

_Q0, _K0, _V0, _GR0, _CB0, _CC0, _CX0, _GC0 = 0, 512, 1024, 2048, 3072, 4096, 5120, 6144

LANES = 128
SUBLANES = 8
ROPE_HALF = 32
HALF_TILE = LANES // 2

VMEM_LIMIT = 56 * 1024 * 1024

BF16 = jnp.bfloat16
F32 = jnp.float32


def _silu(x):
    return (0.5 * x) * (1.0 + jnp.tanh(0.5 * x))


def _rms(x, g):
    ms = jnp.mean(x * x, axis=-1, keepdims=True)
    return x * lax.rsqrt(ms + EPS) * g


def _rope_tile(x, cos, sin_signed):
    return x * cos + pltpu.roll(x, HALF_TILE, 1) * sin_signed


ROPE_GROUPS = LANES // ROPE_HALF
ROPE_BLK = 512


def _rope_table_kernel(pos_ref, inv_ref, cos_ref, sin_ref):
    ang = pos_ref[...].astype(F32) * inv_ref[...]
    c = jnp.cos(ang)
    s = jnp.sin(ang)
    lane = lax.broadcasted_iota(jnp.int32, (ROPE_BLK, LANES), 1)
    low = lane < ROPE_HALF
    for g in range(ROPE_GROUPS):
        shift = (LANES - g * ROPE_HALF) % LANES
        cg = jnp.where(low, c if shift == 0 else pltpu.roll(c, shift, 1), 0.0)
        sg = jnp.where(low, s if shift == 0 else pltpu.roll(s, shift, 1), 0.0)
        cg = cg + pltpu.roll(cg, ROPE_HALF, 1)
        sg = sg + pltpu.roll(sg, ROPE_HALF, 1)
        cos_ref[g] = cg + pltpu.roll(cg, HALF_TILE, 1)
        sin_ref[g] = pltpu.roll(sg, HALF_TILE, 1) - sg


def _rope_tables(positions):
    inv = ROPE_BASE ** (-jnp.arange(0, 2 * ROPE_HALF, 2, dtype=F32) / (2 * ROPE_HALF))
    dense_rows = ROWS // ROPE_GROUPS
    pos_dense = jnp.repeat(positions.reshape(ROPE_GROUPS, dense_rows).T, ROPE_HALF, axis=1)
    inv_dense = jnp.tile(inv, ROPE_GROUPS).reshape(1, LANES)
    out_spec = pl.BlockSpec((ROPE_GROUPS, ROPE_BLK, LANES), lambda i: (0, i, 0))
    out = jax.ShapeDtypeStruct((ROPE_GROUPS, dense_rows, LANES), F32)
    cos_t, sin_t = pl.pallas_call(
        _rope_table_kernel,
        grid=(dense_rows // ROPE_BLK,),
        in_specs=[pl.BlockSpec((ROPE_BLK, LANES), lambda i: (i, 0)),
                  pl.BlockSpec((1, LANES), lambda i: (0, 0))],
        out_specs=[out_spec, out_spec],
        out_shape=[out, out],
        name="rope_table",
    )(pos_dense, inv_dense)
    return cos_t.reshape(ROWS, LANES), sin_t.reshape(ROWS, LANES)


EVEN_IN_TM = 2048
EVEN_IN_TN = 1024
EVEN_QKVG = 3 * EVEN_IN_TN
CONV_BLK = EVEN_IN_TN // 4
CONV_BLOCKS = CONV_W // CONV_BLK
EVEN_IN_STRIP = 256
CONV_STRIP = 128


def _even_in_kernel(x_ref, g_ref, w_ref, cos_ref, sin_ref, cw_ref, cb_ref,
                    q_ref, kt_ref, vg_ref, cmix_ref, u_ref, tail_ref):
    i = pl.program_id(0)
    j = pl.program_id(1)
    strips = [slice(r * EVEN_IN_STRIP, (r + 1) * EVEN_IN_STRIP) for r in range(EVEN_IN_TM // EVEN_IN_STRIP)]

    halves = [slice(0, EVEN_IN_TN // 2), slice(EVEN_IN_TN // 2, EVEN_IN_TN)]

    @pl.when(j == 0)
    def _():
        for rs in strips:
            u = _rms(x_ref[rs, :], g_ref[...]).astype(BF16)
            u_ref[rs, :] = u
            cos = cos_ref[rs, :]
            sin = sin_ref[rs, :]
            for ns in halves:
                acc = jnp.dot(u, w_ref[:, ns], preferred_element_type=F32)
                for c in range((ns.stop - ns.start) // LANES):
                    cs = slice(c * LANES, (c + 1) * LANES)
                    y = _rope_tile(acc[:, cs], cos, sin)
                    if ns.start >= _K0:
                        kt_ref[cs, rs] = (y * (RET_DK ** -0.5)).T.astype(BF16)
                    else:
                        q_ref[rs, cs] = y.astype(BF16)

    @pl.when(j == 1)
    def _():
        vg_ref[...] = jnp.dot(u_ref[...], w_ref[...], preferred_element_type=F32).astype(BF16)

    @pl.when(j == 2)
    def _():
        for rs in strips:
            for ns in halves:
                acc = jnp.dot(u_ref[rs, :], w_ref[:, ns], preferred_element_type=F32)
                vg_ref[rs, ns] = _silu(acc).astype(BF16)

    @pl.when(j >= 3)
    def _():
        slot = j - 3
        sequence_start = (i % (SEQ // EVEN_IN_TM)) == 0
        lane_tiles = [slice(ct * LANES, (ct + 1) * LANES) for ct in range(CONV_BLK // LANES)]
        halos = [jnp.where(sequence_start, 0.0, tail_ref[slot, :, cs]) for cs in lane_tiles]
        for rs in strips:
            for ct, cs in enumerate(lane_tiles):
                acc = jnp.dot(u_ref[rs, :], w_ref[:, ct * 4 * LANES:(ct + 1) * 4 * LANES],
                              preferred_element_type=F32)
                for r in range(EVEN_IN_STRIP // CONV_STRIP):
                    rr = slice(r * CONV_STRIP, (r + 1) * CONV_STRIP)
                    part = lambda k: acc[rr, k * LANES:(k + 1) * LANES]
                    prod = part(1) * part(2)
                    ext = jnp.concatenate([halos[ct], prod], axis=0)
                    conv = cb_ref[:, cs] + prod * cw_ref[CONV_K - 1:CONV_K, cs]
                    for tap in range(CONV_K - 1):
                        off = SUBLANES - (CONV_K - 1) + tap
                        conv = conv + ext[off:off + CONV_STRIP, :] * cw_ref[tap:tap + 1, cs]
                    halos[ct] = prod[CONV_STRIP - SUBLANES:, :]
                    out_rows = slice(rs.start + rr.start, rs.start + rr.stop)
                    cmix_ref[out_rows, cs] = (part(0) * conv * _silu(part(3))).astype(BF16)
        for ct, cs in enumerate(lane_tiles):
            tail_ref[slot, :, cs] = halos[ct]


def _even_in(h, pre_norm, w, cos_t, sin_t, conv_w, conv_b, layer):
    tm, tn = EVEN_IN_TM, EVEN_IN_TN
    e = layer // 2
    qkvg_blocks = EVEN_QKVG // tn
    vg_blocks = qkvg_blocks - 1
    conv_map = lambda i, j: (e, 0, jnp.maximum(j - qkvg_blocks, 0))
    return pl.pallas_call(
        _even_in_kernel,
        grid=(ROWS // tm, EVEN_IN // tn),
        in_specs=[
            pl.BlockSpec((tm, D_MODEL), lambda i, j: (i, 0)),
            pl.BlockSpec((None, 1, D_MODEL), lambda i, j: (layer, 0, 0)),
            pl.BlockSpec((None, D_MODEL, tn), lambda i, j: (e, 0, j)),
            pl.BlockSpec((tm, LANES), lambda i, j: (i, 0)),
            pl.BlockSpec((tm, LANES), lambda i, j: (i, 0)),
            pl.BlockSpec((None, CONV_K, CONV_BLK), conv_map),
            pl.BlockSpec((None, 1, CONV_BLK), conv_map),
        ],
        out_specs=[
            pl.BlockSpec((tm, RET_QK), lambda i, j: (i, 0)),
            pl.BlockSpec((RET_QK, tm), lambda i, j: (0, i)),
            pl.BlockSpec((tm, tn), lambda i, j: (i, jnp.clip(j - 1, 0, vg_blocks - 1))),
            pl.BlockSpec((tm, CONV_BLK), lambda i, j: (i, jnp.maximum(j - qkvg_blocks, 0))),
        ],
        out_shape=[
            jax.ShapeDtypeStruct((ROWS, RET_QK), BF16),
            jax.ShapeDtypeStruct((RET_QK, ROWS), BF16),
            jax.ShapeDtypeStruct((ROWS, vg_blocks * tn), BF16),
            jax.ShapeDtypeStruct((ROWS, CONV_W), BF16),
        ],
        scratch_shapes=[
            pltpu.VMEM((tm, D_MODEL), BF16),
            pltpu.VMEM((CONV_BLOCKS, SUBLANES, CONV_BLK), F32),
        ],
        compiler_params=pltpu.CompilerParams(
            dimension_semantics=("arbitrary", "arbitrary"), vmem_limit_bytes=VMEM_LIMIT),
        name="even_in",
    )(h, pre_norm, w, cos_t, sin_t, conv_w, conv_b)


EVEN_MIX_R = 512
EVEN_MIX_STEPS = ROWS // EVEN_MIX_R
PROJ_PIECE = 256


def _even_mix_block(q_ref, kt_ref, vg_ref, cmix_ref, h_ref, wout_ref, gn_ref, pn_ref,
                    intra_ref, xi_ref, zeta_ref, gam_ref, bm_ref,
                    o_ref, state_ref, mix_prev_ref, mix_ref):
    C = RET_CHUNK
    chunks = EVEN_MIX_R // C

    row_k = lax.broadcasted_iota(jnp.int32, (LANES, C), 0)
    first_head = (row_k % HALF_TILE) < ROPE_HALF
    lane_v = lax.broadcasted_iota(jnp.int32, (C, 2 * RET_DV), 1)
    zero_k = jnp.zeros((LANES, C), BF16)
    zero_v = jnp.zeros((C, 2 * RET_DV), BF16)

    y_prev = []

    def project_piece():
        ns = slice(len(y_prev) * PROJ_PIECE, (len(y_prev) + 1) * PROJ_PIECE)
        y_prev.append(jnp.dot(mix_prev_ref[...], wout_ref[:RET_V, ns], preferred_element_type=F32)
                      + jnp.dot(cmix_ref[...], wout_ref[RET_V:, ns], preferred_element_type=F32))

    stages_per_piece = chunks * RET_PAIRS * PROJ_PIECE // D_MODEL
    for c, j in [(c, p) for c in range(chunks) for p in range(RET_PAIRS)]:
        rows = slice(c * C, (c + 1) * C)
        if (c * RET_PAIRS + j) % stages_per_piece == 0:
            project_piece()
        qp = q_ref[rows, j * LANES:(j + 1) * LANES]
        kt = kt_ref[j * LANES:(j + 1) * LANES, rows]
        vp = vg_ref[rows, j * 2 * RET_DV:(j + 1) * 2 * RET_DV]
        kt_heads = jnp.concatenate(
            [jnp.where(first_head, kt, zero_k), jnp.where(first_head, zero_k, kt)], axis=1)
        sc = jnp.dot(qp, kt_heads, preferred_element_type=F32)
        pm = (sc * intra_ref[j]).astype(BF16)
        vblk = jnp.concatenate(
            [jnp.where(lane_v < RET_DV, vp, zero_v), jnp.where(lane_v >= RET_DV, vp, zero_v)], axis=0)
        st = state_ref[j]
        r = (jnp.dot(pm, vblk, preferred_element_type=F32)
             + jnp.dot(qp, st.astype(BF16), preferred_element_type=F32) * xi_ref[j])
        kz = (kt.astype(F32) * zeta_ref[j]).astype(BF16)
        kv = jnp.dot(kz, vp, preferred_element_type=F32)
        state_ref[j] = st * gam_ref[j] + kv * bm_ref[...]
        for hh in range(2):
            hd = 2 * j + hh
            cols = slice(hd * RET_DV, (hd + 1) * RET_DV)
            y = _rms(r[:, hh * RET_DV:(hh + 1) * RET_DV], gn_ref[:, cols])
            gate = vg_ref[rows, RET_V + hd * RET_DV:RET_V + (hd + 1) * RET_DV].astype(F32)
            mix_ref[rows, cols] = (y * gate).astype(BF16)

    o_ref[...] = h_ref[...] + _rms(jnp.concatenate(y_prev, axis=1), pn_ref[...])


def _even_mix_kernel(q_ref, kt_ref, vg_ref, cmix_ref, h_ref, wout_ref, gn_ref, pn_ref,
                     intra_ref, xi_ref, zeta_ref, gam_ref, bm_ref,
                     o_ref, state_ref, mix_a_ref, mix_b_ref):
    t = pl.program_id(0)
    args = (q_ref, kt_ref, vg_ref, cmix_ref, h_ref, wout_ref, gn_ref, pn_ref,
            intra_ref, xi_ref, zeta_ref, gam_ref, bm_ref, o_ref, state_ref)

    @pl.when(t % (SEQ // EVEN_MIX_R) == 0)
    def _():
        state_ref[...] = jnp.zeros_like(state_ref)

    @pl.when(t == 0)
    def _():
        mix_b_ref[...] = jnp.zeros_like(mix_b_ref)

    @pl.when(t % 2 == 0)
    def _():
        _even_mix_block(*args, mix_b_ref, mix_a_ref)

    @pl.when(t % 2 == 1)
    def _():
        _even_mix_block(*args, mix_a_ref, mix_b_ref)


def _retention_tables():
    h, c = RET_HEADS, RET_CHUNK
    log_gamma = np.log1p(-np.exp2(-5.0 - np.arange(h, dtype=np.float64)))
    i = np.arange(c, dtype=np.float64)
    rel = i[:, None] - i[None, :]
    intra = np.where(rel >= 0, np.exp(log_gamma[:, None, None] * np.maximum(rel, 0.0)), 0.0)
    xi = np.exp(log_gamma[:, None] * (i + 1.0))
    zeta = np.exp(log_gamma[:, None] * (c - 1.0 - i))
    gamma_c = np.exp(log_gamma * c)
    pair = np.arange(RET_PAIRS)
    intra_p = intra.reshape(RET_PAIRS, 2, c, c).transpose(0, 2, 1, 3).reshape(RET_PAIRS, c, 2 * c)
    xi_p = np.repeat(xi.reshape(RET_PAIRS, 2, c).transpose(0, 2, 1), RET_DV, axis=2)
    second = ((np.arange(2 * RET_DK) % HALF_TILE) >= ROPE_HALF).astype(np.int64)
    zeta_p = zeta.reshape(RET_PAIRS, 2, c)[pair[:, None, None], second[None, :, None], np.arange(c)[None, None, :]]
    gam_rows = gamma_c.reshape(RET_PAIRS, 2)[:, second]
    gam_p = np.broadcast_to(gam_rows[:, :, None], (RET_PAIRS, 2 * RET_DK, 2 * RET_DV))
    col_head = np.arange(2 * RET_DV) // RET_DV
    bm = (second[:, None] == col_head[None, :])
    as_f32 = lambda a: jnp.asarray(np.ascontiguousarray(a, dtype=np.float32))
    return as_f32(intra_p), as_f32(xi_p), as_f32(zeta_p), as_f32(gam_p), as_f32(bm)


def _even_mix(q, kt, vg, cmix, h, w_out, gn, post_norm, tables, layer):
    R = EVEN_MIX_R
    nt = EVEN_MIX_STEPS
    e = layer // 2
    intra_p, xi_p, zeta_p, gam_p, bm = tables
    mix_map = lambda t: (jnp.minimum(t, nt - 1), 0)
    row_map = lambda t: (jnp.maximum(t - 1, 0), 0)
    const2 = lambda t: (0, 0)
    const3 = lambda t: (0, 0, 0)
    return pl.pallas_call(
        _even_mix_kernel,
        grid=(nt + 1,),
        in_specs=[
            pl.BlockSpec((R, RET_QK), mix_map),
            pl.BlockSpec((RET_QK, R), lambda t: (0, jnp.minimum(t, nt - 1))),
            pl.BlockSpec((R, 2 * RET_V), mix_map),
            pl.BlockSpec((R, CONV_W), row_map),
            pl.BlockSpec((R, D_MODEL), row_map),
            pl.BlockSpec((None, EVEN_MIX, D_MODEL), lambda t: (e, 0, 0)),
            pl.BlockSpec((None, 1, RET_V), lambda t: (e, 0, 0)),
            pl.BlockSpec((None, 1, D_MODEL), lambda t: (layer, 0, 0)),
            pl.BlockSpec((RET_PAIRS, RET_CHUNK, 2 * RET_CHUNK), const3),
            pl.BlockSpec((RET_PAIRS, RET_CHUNK, 2 * RET_DV), const3),
            pl.BlockSpec((RET_PAIRS, RET_CHUNK, 2 * RET_DK), const3),
            pl.BlockSpec((RET_PAIRS, 2 * RET_DK, 2 * RET_DV), const3),
            pl.BlockSpec((2 * RET_DK, 2 * RET_DV), const2),
        ],
        out_specs=pl.BlockSpec((R, D_MODEL), row_map),
        out_shape=jax.ShapeDtypeStruct((ROWS, D_MODEL), F32),
        scratch_shapes=[
            pltpu.VMEM((RET_PAIRS, 2 * RET_DK, 2 * RET_DV), F32),
            pltpu.VMEM((R, RET_V), BF16),
            pltpu.VMEM((R, RET_V), BF16),
        ],
        compiler_params=pltpu.CompilerParams(
            dimension_semantics=("arbitrary",), vmem_limit_bytes=VMEM_LIMIT),
        name="even_mix",
    )(q, kt, vg, cmix, h, w_out, gn, post_norm, intra_p, xi_p, zeta_p, gam_p, bm)


ODD_R = 512


def _odd_in_kernel(x_ref, g_ref, wa_ref, wg_ref, qn_ref, wq_ref, kvn_ref, wkv_ref, cos_ref, sin_ref,
                   q_ref, k_ref, vt_ref, sg_ref):
    u = _rms(x_ref[...], g_ref[...]).astype(BF16)
    a = jnp.dot(u, wa_ref[...], preferred_element_type=F32)
    cq = a[:, :Q_LORA]
    ckv = a[:, Q_LORA:Q_LORA + KV_LORA]
    kr = a[:, Q_LORA + KV_LORA:]
    q = jnp.dot(_rms(cq, qn_ref[...]).astype(BF16), wq_ref[...], preferred_element_type=F32) * (MLA_SCALE * LOG2E)
    kv = jnp.dot(_rms(ckv, kvn_ref[...]).astype(BF16), wkv_ref[...], preferred_element_type=F32)
    cos = cos_ref[...]
    sin = sin_ref[...]
    kr_rot = _rope_tile(kr, cos, sin)
    kr_heads = (kr_rot.astype(BF16), pltpu.roll(kr_rot, ROPE_HALF, 1).astype(BF16))
    for hd in range(MLA_HEADS):
        lo, hi = hd * LANES, (hd + 1) * LANES
        q_ref[0, hd, 0:LANES, :] = q[:, lo:hi].T.astype(BF16)
        if hd % 2 == 0:
            pair = MLA_V + (hd // 2) * LANES
            q_rot_t = _rope_tile(q[:, pair:pair + LANES], cos, sin).T.astype(BF16)
        q_ref[0, hd, LANES:2 * LANES, :] = q_rot_t
        k_ref[0, hd, :, 0:LANES] = kv[:, lo:hi].astype(BF16)
        k_ref[0, hd, :, LANES:2 * LANES] = kr_heads[hd % 2]
        vt_ref[0, hd, 0] = kv[:, MLA_V + lo:MLA_V + hi].T.astype(BF16)
    gate = jnp.dot(u, wg_ref[...], preferred_element_type=F32)
    sg_ref[...] = _silu(gate).astype(BF16)


def _odd_in(h, pre_norm, wa, wg, qn, wq, kvn, wkv, cos_t, sin_t, layer):
    R = ODD_R
    nb = SEQ // R
    o = layer // 2
    row_map = lambda b, i: (b * nb + i, 0)
    odd_map = lambda b, i: (o, 0, 0)
    return pl.pallas_call(
        _odd_in_kernel,
        grid=(BATCH, nb),
        in_specs=[
            pl.BlockSpec((R, D_MODEL), row_map),
            pl.BlockSpec((None, 1, D_MODEL), lambda b, i: (layer, 0, 0)),
            pl.BlockSpec((None, D_MODEL, ODD_A), odd_map),
            pl.BlockSpec((None, D_MODEL, MLA_V), odd_map),
            pl.BlockSpec((None, 1, Q_LORA), odd_map),
            pl.BlockSpec((None, Q_LORA, ODD_WQ), odd_map),
            pl.BlockSpec((None, 1, KV_LORA), odd_map),
            pl.BlockSpec((None, KV_LORA, 2 * MLA_V), odd_map),
            pl.BlockSpec((R, LANES), row_map),
            pl.BlockSpec((R, LANES), row_map),
        ],
        out_specs=[
            pl.BlockSpec((1, MLA_HEADS, MLA_QK_PAD, R), lambda b, i: (b, 0, 0, i)),
            pl.BlockSpec((1, MLA_HEADS, R, MLA_QK_PAD), lambda b, i: (b, 0, i, 0)),
            pl.BlockSpec((1, MLA_HEADS, 1, MLA_DV, R), lambda b, i: (b, 0, i, 0, 0)),
            pl.BlockSpec((R, MLA_V), row_map),
        ],
        out_shape=[
            jax.ShapeDtypeStruct((BATCH, MLA_HEADS, MLA_QK_PAD, SEQ), BF16),
            jax.ShapeDtypeStruct((BATCH, MLA_HEADS, SEQ, MLA_QK_PAD), BF16),
            jax.ShapeDtypeStruct((BATCH, MLA_HEADS, nb, MLA_DV, R), BF16),
            jax.ShapeDtypeStruct((ROWS, MLA_V), BF16),
        ],
        compiler_params=pltpu.CompilerParams(
            dimension_semantics=("arbitrary", "arbitrary"), vmem_limit_bytes=VMEM_LIMIT),
        name="odd_in",
    )(h, pre_norm, wa, wg, qn, wq, kvn, wkv, cos_t, sin_t)


ATTN_T = ODD_R
ATTN_HG = 4


def _attn_kernel(q_ref, qn_ref, k_ref, vt_ref, sg_ref, o_ref, m_ref, l_ref, acc_ref, st_ref):
    T = ATTN_T
    qi = pl.program_id(2)
    m_ref[...] = jnp.full(m_ref.shape, -jnp.inf, F32)
    l_ref[...] = jnp.zeros(l_ref.shape, F32)
    acc_ref[...] = jnp.zeros(acc_ref.shape, F32)

    def scores(hh, j, queries=q_ref):
        k = k_ref[0, hh, pl.ds(pl.multiple_of(j * T, T), T), :]
        return jnp.dot(k, queries[0, hh], preferred_element_type=F32)

    def consume(hh, j, st):
        m_old = m_ref[hh]
        m_new = jnp.maximum(m_old, jnp.max(st, axis=0, keepdims=True))
        alpha = jnp.exp2(m_old - m_new)
        p = jnp.exp2(st - m_new)
        l_ref[hh] = alpha * l_ref[hh] + jnp.sum(p, axis=0, keepdims=True)
        acc_ref[hh] = alpha * acc_ref[hh] + jnp.dot(
            vt_ref[0, hh, j], p.astype(BF16), preferred_element_type=F32)
        m_ref[hh] = m_new

    @pl.when(qi == 0)
    def _():
        for hh in range(ATTN_HG):
            st_ref[hh] = scores(hh, 0)

    def body(j, carry):
        for hh in range(ATTN_HG):
            st = st_ref[hh]
            st_ref[hh] = scores(hh, j + 1)
            consume(hh, j, st)
        return carry

    lax.fori_loop(0, qi, body, 0)

    H = T // 2
    causal = lax.broadcasted_iota(jnp.int32, (H, H), 0) <= lax.broadcasted_iota(jnp.int32, (H, H), 1)

    def consume_diagonal(hh, st):
        s_ee = jnp.where(causal, st[:H, :H], -jnp.inf)
        s_el = st[:H, H:]
        s_ll = jnp.where(causal, st[H:, H:], -jnp.inf)
        m_old = m_ref[hh]
        col_max = jnp.concatenate(
            [jnp.max(s_ee, axis=0, keepdims=True),
             jnp.maximum(jnp.max(s_el, axis=0, keepdims=True), jnp.max(s_ll, axis=0, keepdims=True))], axis=1)
        m_new = jnp.maximum(m_old, col_max)
        alpha = jnp.exp2(m_old - m_new)
        p_ee = jnp.exp2(s_ee - m_new[:, :H])
        p_el = jnp.exp2(s_el - m_new[:, H:])
        p_ll = jnp.exp2(s_ll - m_new[:, H:])
        col_sum = jnp.concatenate(
            [jnp.sum(p_ee, axis=0, keepdims=True),
             jnp.sum(p_el, axis=0, keepdims=True) + jnp.sum(p_ll, axis=0, keepdims=True)], axis=1)
        l_ref[hh] = alpha * l_ref[hh] + col_sum
        vt = vt_ref[0, hh, qi]
        pv = jnp.concatenate(
            [jnp.dot(vt[:, :H], p_ee.astype(BF16), preferred_element_type=F32),
             jnp.dot(vt, jnp.concatenate([p_el, p_ll], axis=0).astype(BF16), preferred_element_type=F32)], axis=1)
        acc_ref[hh] = alpha * acc_ref[hh] + pv
        m_ref[hh] = m_new

    for hh in range(ATTN_HG):
        st = st_ref[hh]
        st_ref[hh] = scores(hh, 0, qn_ref)
        consume_diagonal(hh, st)
    for hh in range(ATTN_HG):
        cols = slice(hh * MLA_DV, (hh + 1) * MLA_DV)
        o = (acc_ref[hh] * (1.0 / l_ref[hh])).T
        o_ref[0, :, cols] = (o * sg_ref[0, :, cols].astype(F32)).astype(BF16)


def _attn(q_cat, k_cat, vt, sg):
    T = ATTN_T
    HG = ATTN_HG
    nq = SEQ // T
    return pl.pallas_call(
        _attn_kernel,
        grid=(BATCH, MLA_HEADS // HG, nq),
        in_specs=[
            pl.BlockSpec((1, HG, MLA_QK_PAD, T), lambda b, h, i: (b, h, 0, i)),
            pl.BlockSpec((1, HG, MLA_QK_PAD, T), lambda b, h, i: (b, h, 0, jnp.minimum(i + 1, nq - 1))),
            pl.BlockSpec((1, HG, SEQ, MLA_QK_PAD), lambda b, h, i: (b, h, 0, 0)),
            pl.BlockSpec((1, HG, nq, MLA_DV, T), lambda b, h, i: (b, h, 0, 0, 0)),
            pl.BlockSpec((1, T, HG * MLA_DV), lambda b, h, i: (b, i, h)),
        ],
        out_specs=pl.BlockSpec((1, T, HG * MLA_DV), lambda b, h, i: (b, i, h)),
        out_shape=jax.ShapeDtypeStruct((BATCH, SEQ, MLA_V), BF16),
        scratch_shapes=[
            pltpu.VMEM((HG, 1, T), F32),
            pltpu.VMEM((HG, 1, T), F32),
            pltpu.VMEM((HG, MLA_DV, T), F32),
            pltpu.VMEM((HG, T, T), F32),
        ],
        compiler_params=pltpu.CompilerParams(
            dimension_semantics=("arbitrary", "arbitrary", "arbitrary"), vmem_limit_bytes=VMEM_LIMIT),
        name="odd_attn",
    )(q_cat, q_cat, k_cat, vt, sg.reshape(BATCH, SEQ, MLA_V))


ODD_OUT_R = 2048


def _odd_out_kernel(og_ref, h_ref, w_ref, pn_ref, out_ref):
    y = jnp.dot(og_ref[...], w_ref[...], preferred_element_type=F32)
    out_ref[...] = h_ref[...] + _rms(y, pn_ref[...])


def _odd_out(og, h, w, post_norm, layer):
    R = ODD_OUT_R
    row_map = lambda i: (i, 0)
    return pl.pallas_call(
        _odd_out_kernel,
        grid=(ROWS // R,),
        in_specs=[
            pl.BlockSpec((R, MLA_V), row_map),
            pl.BlockSpec((R, D_MODEL), row_map),
            pl.BlockSpec((None, MLA_V, D_MODEL), lambda i: (layer // 2, 0, 0)),
            pl.BlockSpec((None, 1, D_MODEL), lambda i: (layer, 0, 0)),
        ],
        out_specs=pl.BlockSpec((R, D_MODEL), row_map),
        out_shape=jax.ShapeDtypeStruct((ROWS, D_MODEL), F32),
        compiler_params=pltpu.CompilerParams(
            dimension_semantics=("arbitrary",), vmem_limit_bytes=VMEM_LIMIT),
        name="odd_out",
    )(og, h, w, post_norm)


PREP_ROW_BLOCKS = 4


def _lane_iota(rows):
    return lax.broadcasted_iota(jnp.int32, (rows, LANES), 1)


def _prep_even_kernel(win_ref, wout_ref, win_o_ref, wout_o_ref):
    rows = win_ref.shape[0]
    lane = _lane_iota(rows)
    keep = (lane < ROPE_HALF) | (lane >= LANES - ROPE_HALF)
    for t in range(_V0 // LANES):
        cs = slice(t * LANES, (t + 1) * LANES)
        x = win_ref[:, cs]
        moved = jnp.where(lane < HALF_TILE, pltpu.roll(x, LANES - ROPE_HALF, 1), pltpu.roll(x, ROPE_HALF, 1))
        win_o_ref[:, cs] = jnp.where(keep, x, moved).astype(BF16)
    win_o_ref[:, _V0:_CB0] = win_ref[:, _V0:_CB0].astype(BF16)
    for t in range(CONV_W // LANES):
        for k in range(4):
            src = _CB0 + k * CONV_W + t * LANES
            dst = _CB0 + (4 * t + k) * LANES
            win_o_ref[:, dst:dst + LANES] = win_ref[:, src:src + LANES].astype(BF16)
    wout_o_ref[...] = wout_ref[...].astype(BF16)


def _prep_even(even_w_in, even_w_out):
    n = even_w_in.shape[0]
    rb = PREP_ROW_BLOCKS
    spec = lambda rows, cols: pl.BlockSpec((None, rows // rb, cols), lambda e, r: (e, r, 0))
    return pl.pallas_call(
        _prep_even_kernel,
        grid=(n, rb),
        in_specs=[spec(D_MODEL, EVEN_IN), spec(EVEN_MIX, D_MODEL)],
        out_specs=[spec(D_MODEL, EVEN_IN), spec(EVEN_MIX, D_MODEL)],
        out_shape=[jax.ShapeDtypeStruct((n, D_MODEL, EVEN_IN), BF16),
                   jax.ShapeDtypeStruct((n, EVEN_MIX, D_MODEL), BF16)],
        compiler_params=pltpu.CompilerParams(
            dimension_semantics=("arbitrary", "arbitrary"), vmem_limit_bytes=VMEM_LIMIT),
        name="prep_even",
    )(even_w_in, even_w_out)


def _prep_odd_kernel(wint_ref, wqb_ref, wkvb_ref, wout_ref, wa_ref, wg_ref, wq_ref, wkv_ref, wo_ref):
    lat = Q_LORA + KV_LORA
    cols = wint_ref.shape[1]
    wa_ref[:, :lat] = wint_ref[:lat, :].T.astype(BF16)
    zeros = jnp.zeros((ROPE_HALF, cols), F32)
    rope_rows = jnp.concatenate(
        [wint_ref[lat:lat + ROPE_HALF, :], zeros, wint_ref[lat + ROPE_HALF:lat + MLA_ROPE, :], zeros], axis=0)
    wa_ref[:, lat:] = rope_rows.T.astype(BF16)
    wg_ref[...] = wint_ref[lat + MLA_ROPE:, :].T.astype(BF16)
    lane = _lane_iota(wqb_ref.shape[0])
    for hp in range(MLA_HEADS // 2):
        t0, t1, t2 = (wqb_ref[:, (3 * hp + k) * LANES:(3 * hp + k + 1) * LANES] for k in range(3))
        ha, hb = 2 * hp, 2 * hp + 1
        wq_ref[:, ha * LANES:(ha + 1) * LANES] = t0.astype(BF16)
        wq_ref[:, hb * LANES:(hb + 1) * LANES] = jnp.where(
            lane < HALF_TILE, pltpu.roll(t1, HALF_TILE, 1), pltpu.roll(t2, HALF_TILE, 1)).astype(BF16)
        pair = jnp.where(lane < ROPE_HALF, t1,
                         jnp.where(lane < HALF_TILE, pltpu.roll(t2, LANES - ROPE_HALF, 1),
                                   jnp.where(lane < HALF_TILE + ROPE_HALF, pltpu.roll(t1, ROPE_HALF, 1), t2)))
        wq_ref[:, MLA_V + hp * LANES:MLA_V + (hp + 1) * LANES] = pair.astype(BF16)
    for hd in range(MLA_HEADS):
        wkv_ref[:, hd * LANES:(hd + 1) * LANES] = wkvb_ref[:, 2 * hd * LANES:(2 * hd + 1) * LANES].astype(BF16)
        wkv_ref[:, MLA_V + hd * LANES:MLA_V + (hd + 1) * LANES] = (
            wkvb_ref[:, (2 * hd + 1) * LANES:(2 * hd + 2) * LANES].astype(BF16))
    wo_ref[...] = wout_ref[...].astype(BF16)


def _prep_odd(odd_w_in, w_qb, w_kvb, odd_w_out):
    n = odd_w_in.shape[0]
    rb = PREP_ROW_BLOCKS
    spec = lambda rows, cols: pl.BlockSpec((None, rows // rb, cols), lambda o, r: (o, r, 0))
    shape = lambda rows, cols: jax.ShapeDtypeStruct((n, rows, cols), BF16)
    odd_w_in_t = jnp.swapaxes(odd_w_in, 1, 2)
    return pl.pallas_call(
        _prep_odd_kernel,
        grid=(n, rb),
        in_specs=[pl.BlockSpec((None, ODD_IN, D_MODEL // rb), lambda o, r: (o, 0, r)),
                  spec(Q_LORA, MLA_HEADS * MLA_QK),
                  spec(KV_LORA, 2 * MLA_V), spec(MLA_V, D_MODEL)],
        out_specs=[spec(D_MODEL, ODD_A), spec(D_MODEL, MLA_V), spec(Q_LORA, ODD_WQ),
                   spec(KV_LORA, 2 * MLA_V), spec(MLA_V, D_MODEL)],
        out_shape=[shape(D_MODEL, ODD_A), shape(D_MODEL, MLA_V), shape(Q_LORA, ODD_WQ),
                   shape(KV_LORA, 2 * MLA_V), shape(MLA_V, D_MODEL)],
        compiler_params=pltpu.CompilerParams(
            dimension_semantics=("arbitrary", "arbitrary"), vmem_limit_bytes=VMEM_LIMIT),
        name="prep_odd",
    )(odd_w_in_t, w_qb, w_kvb, odd_w_out)


def _rows3(a):
    return a.reshape(a.shape[0], 1, a.shape[1])


def kernel(x, positions, pre_norm, post_norm, even_w_in, even_conv_w, even_conv_b, ret_gn, even_w_out,
           odd_w_in, q_a_norm, w_qb, kv_a_norm, w_kvb, odd_w_out):
    cos_t, sin_t = _rope_tables(positions)
    tables = _retention_tables()
    ew_in, ew_out = _prep_even(even_w_in, even_w_out)
    wa, wg, wq, wkv, wo = _prep_odd(odd_w_in, w_qb, w_kvb, odd_w_out)
    pre_norm, post_norm, ret_gn, even_conv_b = map(_rows3, (pre_norm, post_norm, ret_gn, even_conv_b))
    q_a_norm, kv_a_norm = _rows3(q_a_norm), _rows3(kv_a_norm)
    h = x.reshape(ROWS, D_MODEL)
    for layer in range(DEPTH):
        if layer % 2 == 0:
            q, kt, vg, cmix = _even_in(h, pre_norm, ew_in, cos_t, sin_t, even_conv_w, even_conv_b, layer)
            h = _even_mix(q, kt, vg, cmix, h, ew_out, ret_gn, post_norm, tables, layer)
        else:
            q_cat, k_cat, vt, sg = _odd_in(h, pre_norm, wa, wg, q_a_norm, wq, kv_a_norm, wkv, cos_t, sin_t, layer)
            og = _attn(q_cat, k_cat, vt, sg)
            h = _odd_out(og.reshape(ROWS, MLA_V), h, wo, post_norm, layer)
    return h.reshape(BATCH, SEQ, D_MODEL)
```

```python
import math

import jax
import jax.numpy as jnp
import numpy as np
from jax import lax
from jax.experimental import pallas as pl
from jax.experimental.pallas import tpu as pltpu

D_MODEL = 1024
BATCH = 4
SEQ = 4096
DEPTH = 4
ROWS = BATCH * SEQ

RET_HEADS = 8
RET_DK = 64
RET_DV = 128
RET_CHUNK = 128
RET_PAIRS = RET_HEADS // 2
CONV_W = 1024
CONV_K = 3
MLA_HEADS = 8
MLA_NOPE = 128
MLA_ROPE = 64
MLA_DV = 128
Q_LORA = 384
KV_LORA = 256
ROPE_BASE = 10000.0
EPS = 1e-6

RET_QK = RET_HEADS * RET_DK
RET_V = RET_HEADS * RET_DV
EVEN_IN = 2 * RET_QK + 2 * RET_V + 4 * CONV_W
EVEN_MIX = RET_V + CONV_W
MLA_QK = MLA_NOPE + MLA_ROPE
MLA_V = MLA_HEADS * MLA_DV
MLA_SCALE = MLA_QK ** -0.5
LOG2E = math.log2(math.e)
MLA_QK_PAD = 256
ODD_IN = Q_LORA + KV_LORA + MLA_ROPE + MLA_V
ODD_A = Q_LORA + KV_LORA + 128
ODD_WQ = MLA_V + (MLA_HEADS // 2) * 128

_Q0, _K0, _V0, _GR0, _CB0, _CC0, _CX0, _GC0 = 0, 512, 1024, 2048, 3072, 4096, 5120, 6144

LANES = 128
SUBLANES = 8
ROPE_HALF = 32
HALF_TILE = LANES // 2

VMEM_LIMIT = 56 * 1024 * 1024

BF16 = jnp.bfloat16
F32 = jnp.float32


def _silu(x):
    return (0.5 * x) * (1.0 + jnp.tanh(0.5 * x))


def _rms(x, g):
    ms = jnp.mean(x * x, axis=-1, keepdims=True)
    return x * lax.rsqrt(ms + EPS) * g


def _rope_tile(x, cos, sin_signed):
    return x * cos + pltpu.roll(x, HALF_TILE, 1) * sin_signed


ROPE_GROUPS = LANES // ROPE_HALF
ROPE_BLK = 512


def _rope_table_kernel(pos_ref, inv_ref, cos_ref, sin_ref):
    ang = pos_ref[...].astype(F32) * inv_ref[...]
    c = jnp.cos(ang)
    s = jnp.sin(ang)
    lane = lax.broadcasted_iota(jnp.int32, (ROPE_BLK, LANES), 1)
    low = lane < ROPE_HALF
    for g in range(ROPE_GROUPS):
        shift = (LANES - g * ROPE_HALF) % LANES
        cg = jnp.where(low, c if shift == 0 else pltpu.roll(c, shift, 1), 0.0)
        sg = jnp.where(low, s if shift == 0 else pltpu.roll(s, shift, 1), 0.0)
        cg = cg + pltpu.roll(cg, ROPE_HALF, 1)
        sg = sg + pltpu.roll(sg, ROPE_HALF, 1)
        cos_ref[g] = cg + pltpu.roll(cg, HALF_TILE, 1)
        sin_ref[g] = pltpu.roll(sg, HALF_TILE, 1) - sg


def _rope_tables(positions):
    inv = ROPE_BASE ** (-jnp.arange(0, 2 * ROPE_HALF, 2, dtype=F32) / (2 * ROPE_HALF))
    dense_rows = ROWS // ROPE_GROUPS
    pos_dense = jnp.repeat(positions.reshape(ROPE_GROUPS, dense_rows).T, ROPE_HALF, axis=1)
    inv_dense = jnp.tile(inv, ROPE_GROUPS).reshape(1, LANES)
    out_spec = pl.BlockSpec((ROPE_GROUPS, ROPE_BLK, LANES), lambda i: (0, i, 0))
    out = jax.ShapeDtypeStruct((ROPE_GROUPS, dense_rows, LANES), F32)
    cos_t, sin_t = pl.pallas_call(
        _rope_table_kernel,
        grid=(dense_rows // ROPE_BLK,),
        in_specs=[pl.BlockSpec((ROPE_BLK, LANES), lambda i: (i, 0)),
                  pl.BlockSpec((1, LANES), lambda i: (0, 0))],
        out_specs=[out_spec, out_spec],
        out_shape=[out, out],
        name="rope_table",
    )(pos_dense, inv_dense)
    return cos_t.reshape(ROWS, LANES), sin_t.reshape(ROWS, LANES)


EVEN_IN_TM = 2048
EVEN_IN_TN = 1024
EVEN_QKVG = 3 * EVEN_IN_TN
CONV_BLK = EVEN_IN_TN // 4
CONV_BLOCKS = CONV_W // CONV_BLK
EVEN_IN_STRIP = 256
CONV_STRIP = 128


def _even_in_kernel(x_ref, g_ref, w_ref, cos_ref, sin_ref, cw_ref, cb_ref,
                    q_ref, kt_ref, vg_ref, cmix_ref, u_ref, tail_ref):
    i = pl.program_id(0)
    j = pl.program_id(1)
    strips = [slice(r * EVEN_IN_STRIP, (r + 1) * EVEN_IN_STRIP) for r in range(EVEN_IN_TM // EVEN_IN_STRIP)]

    halves = [slice(0, EVEN_IN_TN // 2), slice(EVEN_IN_TN // 2, EVEN_IN_TN)]

    @pl.when(j == 0)
    def _():
        for rs in strips:
            u = _rms(x_ref[rs, :], g_ref[...]).astype(BF16)
            u_ref[rs, :] = u
            cos = cos_ref[rs, :]
            sin = sin_ref[rs, :]
            for ns in halves:
                acc = jnp.dot(u, w_ref[:, ns], preferred_element_type=F32)
                for c in range((ns.stop - ns.start) // LANES):
                    cs = slice(c * LANES, (c + 1) * LANES)
                    y = _rope_tile(acc[:, cs], cos, sin)
                    if ns.start >= _K0:
                        kt_ref[cs, rs] = y.T.astype(BF16)
                    else:
                        q_ref[rs, cs] = y.astype(BF16)

    @pl.when(j == 1)
    def _():
        vg_ref[...] = jnp.dot(u_ref[...], w_ref[...], preferred_element_type=F32).astype(BF16)

    @pl.when(j == 2)
    def _():
        for rs in strips:
            for ns in halves:
                acc = jnp.dot(u_ref[rs, :], w_ref[:, ns], preferred_element_type=F32)
                vg_ref[rs, ns] = _silu(acc).astype(BF16)

    @pl.when(j >= 3)
    def _():
        slot = j - 3
        sequence_start = (i % (SEQ // EVEN_IN_TM)) == 0
        lane_tiles = [slice(ct * LANES, (ct + 1) * LANES) for ct in range(CONV_BLK // LANES)]
        halos = [jnp.where(sequence_start, 0.0, tail_ref[slot, :, cs]) for cs in lane_tiles]
        for rs in strips:
            for ct, cs in enumerate(lane_tiles):
                acc = jnp.dot(u_ref[rs, :], w_ref[:, ct * 4 * LANES:(ct + 1) * 4 * LANES],
                              preferred_element_type=F32)
                for r in range(EVEN_IN_STRIP // CONV_STRIP):
                    rr = slice(r * CONV_STRIP, (r + 1) * CONV_STRIP)
                    part = lambda k: acc[rr, k * LANES:(k + 1) * LANES]
                    prod = part(1) * part(2)
                    ext = jnp.concatenate([halos[ct], prod], axis=0)
                    conv = cb_ref[:, cs] + prod * cw_ref[CONV_K - 1:CONV_K, cs]
                    for tap in range(CONV_K - 1):
                        off = SUBLANES - (CONV_K - 1) + tap
                        conv = conv + ext[off:off + CONV_STRIP, :] * cw_ref[tap:tap + 1, cs]
                    halos[ct] = prod[CONV_STRIP - SUBLANES:, :]
                    out_rows = slice(rs.start + rr.start, rs.start + rr.stop)
                    cmix_ref[out_rows, cs] = (part(0) * conv * _silu(part(3))).astype(BF16)
        for ct, cs in enumerate(lane_tiles):
            tail_ref[slot, :, cs] = halos[ct]


def _even_in(h, pre_norm, w, cos_t, sin_t, conv_w, conv_b, layer):
    tm, tn = EVEN_IN_TM, EVEN_IN_TN
    e = layer // 2
    qkvg_blocks = EVEN_QKVG // tn
    vg_blocks = qkvg_blocks - 1
    conv_map = lambda i, j: (e, 0, jnp.maximum(j - qkvg_blocks, 0))
    return pl.pallas_call(
        _even_in_kernel,
        grid=(ROWS // tm, EVEN_IN // tn),
        in_specs=[
            pl.BlockSpec((tm, D_MODEL), lambda i, j: (i, 0)),
            pl.BlockSpec((None, 1, D_MODEL), lambda i, j: (layer, 0, 0)),
            pl.BlockSpec((None, D_MODEL, tn), lambda i, j: (e, 0, j)),
            pl.BlockSpec((tm, LANES), lambda i, j: (i, 0)),
            pl.BlockSpec((tm, LANES), lambda i, j: (i, 0)),
            pl.BlockSpec((None, CONV_K, CONV_BLK), conv_map),
            pl.BlockSpec((None, 1, CONV_BLK), conv_map),
        ],
        out_specs=[
            pl.BlockSpec((tm, RET_QK), lambda i, j: (i, 0)),
            pl.BlockSpec((RET_QK, tm), lambda i, j: (0, i)),
            pl.BlockSpec((tm, tn), lambda i, j: (i, jnp.clip(j - 1, 0, vg_blocks - 1))),
            pl.BlockSpec((tm, CONV_BLK), lambda i, j: (i, jnp.maximum(j - qkvg_blocks, 0))),
        ],
        out_shape=[
            jax.ShapeDtypeStruct((ROWS, RET_QK), BF16),
            jax.ShapeDtypeStruct((RET_QK, ROWS), BF16),
            jax.ShapeDtypeStruct((ROWS, vg_blocks * tn), BF16),
            jax.ShapeDtypeStruct((ROWS, CONV_W), BF16),
        ],
        scratch_shapes=[
            pltpu.VMEM((tm, D_MODEL), BF16),
            pltpu.VMEM((CONV_BLOCKS, SUBLANES, CONV_BLK), F32),
        ],
        compiler_params=pltpu.CompilerParams(
            dimension_semantics=("arbitrary", "arbitrary"), vmem_limit_bytes=VMEM_LIMIT),
        name="even_in",
    )(h, pre_norm, w, cos_t, sin_t, conv_w, conv_b)


EVEN_MIX_R = 512
EVEN_MIX_STEPS = ROWS // EVEN_MIX_R
PROJ_PIECE = 256


def _even_mix_block(q_ref, kt_ref, vg_ref, cmix_ref, h_ref, wout_ref, gn_ref, pn_ref,
                    intra_ref, xi_ref, zeta_ref, gam_ref, bm_ref,
                    o_ref, state_ref, mix_prev_ref, mix_ref):
    C = RET_CHUNK
    chunks = EVEN_MIX_R // C

    row_k = lax.broadcasted_iota(jnp.int32, (LANES, C), 0)
    first_head = (row_k % HALF_TILE) < ROPE_HALF
    lane_v = lax.broadcasted_iota(jnp.int32, (C, 2 * RET_DV), 1)
    zero_k = jnp.zeros((LANES, C), BF16)
    zero_v = jnp.zeros((C, 2 * RET_DV), BF16)

    y_prev = []

    def project_piece():
        ns = slice(len(y_prev) * PROJ_PIECE, (len(y_prev) + 1) * PROJ_PIECE)
        y_prev.append(jnp.dot(mix_prev_ref[...], wout_ref[:RET_V, ns], preferred_element_type=F32)
                      + jnp.dot(cmix_ref[...], wout_ref[RET_V:, ns], preferred_element_type=F32))

    stages_per_piece = chunks * RET_PAIRS * PROJ_PIECE // D_MODEL
    for c, j in [(c, p) for c in range(chunks) for p in range(RET_PAIRS)]:
        rows = slice(c * C, (c + 1) * C)
        if (c * RET_PAIRS + j) % stages_per_piece == 0:
            project_piece()
        qp = q_ref[rows, j * LANES:(j + 1) * LANES]
        kt = kt_ref[j * LANES:(j + 1) * LANES, rows]
        vp = vg_ref[rows, j * 2 * RET_DV:(j + 1) * 2 * RET_DV]
        kt_heads = jnp.concatenate(
            [jnp.where(first_head, kt, zero_k), jnp.where(first_head, zero_k, kt)], axis=1)
        sc = jnp.dot(qp, kt_heads, preferred_element_type=F32)
        pm = (sc * intra_ref[j]).astype(BF16)
        vblk = jnp.concatenate(
            [jnp.where(lane_v < RET_DV, vp, zero_v), jnp.where(lane_v >= RET_DV, vp, zero_v)], axis=0)
        st = state_ref[j]
        r = (jnp.dot(pm, vblk, preferred_element_type=F32)
             + jnp.dot(qp, st.astype(BF16), preferred_element_type=F32) * xi_ref[j])
        kz = (kt.astype(F32) * zeta_ref[j]).astype(BF16)
        kv = jnp.dot(kz, vp, preferred_element_type=F32)
        state_ref[j] = st * gam_ref[j] + kv * bm_ref[...]
        for hh in range(2):
            hd = 2 * j + hh
            cols = slice(hd * RET_DV, (hd + 1) * RET_DV)
            y = _rms(r[:, hh * RET_DV:(hh + 1) * RET_DV], gn_ref[:, cols])
            gate = vg_ref[rows, RET_V + hd * RET_DV:RET_V + (hd + 1) * RET_DV].astype(F32)
            mix_ref[rows, cols] = (y * gate).astype(BF16)

    o_ref[...] = h_ref[...] + _rms(jnp.concatenate(y_prev, axis=1), pn_ref[...])


def _even_mix_kernel(q_ref, kt_ref, vg_ref, cmix_ref, h_ref, wout_ref, gn_ref, pn_ref,
                     intra_ref, xi_ref, zeta_ref, gam_ref, bm_ref,
                     o_ref, state_ref, mix_a_ref, mix_b_ref):
    t = pl.program_id(0)
    args = (q_ref, kt_ref, vg_ref, cmix_ref, h_ref, wout_ref, gn_ref, pn_ref,
            intra_ref, xi_ref, zeta_ref, gam_ref, bm_ref, o_ref, state_ref)

    @pl.when(t % (SEQ // EVEN_MIX_R) == 0)
    def _():
        state_ref[...] = jnp.zeros_like(state_ref)

    @pl.when(t == 0)
    def _():
        mix_b_ref[...] = jnp.zeros_like(mix_b_ref)

    @pl.when(t % 2 == 0)
    def _():
        _even_mix_block(*args, mix_b_ref, mix_a_ref)

    @pl.when(t % 2 == 1)
    def _():
        _even_mix_block(*args, mix_a_ref, mix_b_ref)


def _retention_tables():
    h, c = RET_HEADS, RET_CHUNK
    log_gamma = np.log1p(-np.exp2(-5.0 - np.arange(h, dtype=np.float64)))
    i = np.arange(c, dtype=np.float64)
    rel = i[:, None] - i[None, :]
    intra = np.where(rel >= 0, np.exp(log_gamma[:, None, None] * np.maximum(rel, 0.0)), 0.0)
    xi = np.exp(log_gamma[:, None] * (i + 1.0))
    zeta = np.exp(log_gamma[:, None] * (c - 1.0 - i))
    gamma_c = np.exp(log_gamma * c)
    pair = np.arange(RET_PAIRS)
    intra_p = intra.reshape(RET_PAIRS, 2, c, c).transpose(0, 2, 1, 3).reshape(RET_PAIRS, c, 2 * c)
    xi_p = np.repeat(xi.reshape(RET_PAIRS, 2, c).transpose(0, 2, 1), RET_DV, axis=2)
    second = ((np.arange(2 * RET_DK) % HALF_TILE) >= ROPE_HALF).astype(np.int64)
    zeta_p = zeta.reshape(RET_PAIRS, 2, c)[pair[:, None, None], second[None, :, None], np.arange(c)[None, None, :]]
    gam_rows = gamma_c.reshape(RET_PAIRS, 2)[:, second]
    gam_p = np.broadcast_to(gam_rows[:, :, None], (RET_PAIRS, 2 * RET_DK, 2 * RET_DV))
    col_head = np.arange(2 * RET_DV) // RET_DV
    bm = (second[:, None] == col_head[None, :])
    as_f32 = lambda a: jnp.asarray(np.ascontiguousarray(a, dtype=np.float32))
    return as_f32(intra_p), as_f32(xi_p), as_f32(zeta_p), as_f32(gam_p), as_f32(bm)


def _even_mix(q, kt, vg, cmix, h, w_out, gn, post_norm, tables, layer):
    R = EVEN_MIX_R
    nt = EVEN_MIX_STEPS
    e = layer // 2
    intra_p, xi_p, zeta_p, gam_p, bm = tables
    mix_map = lambda t: (jnp.minimum(t, nt - 1), 0)
    row_map = lambda t: (jnp.maximum(t - 1, 0), 0)
    const2 = lambda t: (0, 0)
    const3 = lambda t: (0, 0, 0)
    return pl.pallas_call(
        _even_mix_kernel,
        grid=(nt + 1,),
        in_specs=[
            pl.BlockSpec((R, RET_QK), mix_map),
            pl.BlockSpec((RET_QK, R), lambda t: (0, jnp.minimum(t, nt - 1))),
            pl.BlockSpec((R, 2 * RET_V), mix_map),
            pl.BlockSpec((R, CONV_W), row_map),
            pl.BlockSpec((R, D_MODEL), row_map),
            pl.BlockSpec((None, EVEN_MIX, D_MODEL), lambda t: (e, 0, 0)),
            pl.BlockSpec((None, 1, RET_V), lambda t: (e, 0, 0)),
            pl.BlockSpec((None, 1, D_MODEL), lambda t: (layer, 0, 0)),
            pl.BlockSpec((RET_PAIRS, RET_CHUNK, 2 * RET_CHUNK), const3),
            pl.BlockSpec((RET_PAIRS, RET_CHUNK, 2 * RET_DV), const3),
            pl.BlockSpec((RET_PAIRS, RET_CHUNK, 2 * RET_DK), const3),
            pl.BlockSpec((RET_PAIRS, 2 * RET_DK, 2 * RET_DV), const3),
            pl.BlockSpec((2 * RET_DK, 2 * RET_DV), const2),
        ],
        out_specs=pl.BlockSpec((R, D_MODEL), row_map),
        out_shape=jax.ShapeDtypeStruct((ROWS, D_MODEL), F32),
        scratch_shapes=[
            pltpu.VMEM((RET_PAIRS, 2 * RET_DK, 2 * RET_DV), F32),
            pltpu.VMEM((R, RET_V), BF16),
            pltpu.VMEM((R, RET_V), BF16),
        ],
        compiler_params=pltpu.CompilerParams(
            dimension_semantics=("arbitrary",), vmem_limit_bytes=VMEM_LIMIT),
        name="even_mix",
    )(q, kt, vg, cmix, h, w_out, gn, post_norm, intra_p, xi_p, zeta_p, gam_p, bm)


ODD_R = 1024
ATTN_T = 512


def _odd_in_kernel(x_ref, g_ref, wa_ref, wg_ref, qn_ref, wq_ref, kvn_ref, wkv_ref, cos_ref, sin_ref,
                   q_ref, k_ref, vt_ref, sg_ref):
    u = _rms(x_ref[...], g_ref[...]).astype(BF16)
    a = jnp.dot(u, wa_ref[...], preferred_element_type=F32)
    cq = a[:, :Q_LORA]
    ckv = a[:, Q_LORA:Q_LORA + KV_LORA]
    kr = a[:, Q_LORA + KV_LORA:]
    q = jnp.dot(_rms(cq, qn_ref[...]).astype(BF16), wq_ref[...], preferred_element_type=F32) * (MLA_SCALE * LOG2E)
    kv = jnp.dot(_rms(ckv, kvn_ref[...]).astype(BF16), wkv_ref[...], preferred_element_type=F32)
    cos = cos_ref[...]
    sin = sin_ref[...]
    kr_rot = _rope_tile(kr, cos, sin)
    kr_heads = (kr_rot.astype(BF16), pltpu.roll(kr_rot, ROPE_HALF, 1).astype(BF16))
    for hd in range(MLA_HEADS):
        lo, hi = hd * LANES, (hd + 1) * LANES
        q_ref[0, hd, 0:LANES, :] = q[:, lo:hi].T.astype(BF16)
        if hd % 2 == 0:
            pair = MLA_V + (hd // 2) * LANES
            q_rot_t = _rope_tile(q[:, pair:pair + LANES], cos, sin).T.astype(BF16)
        q_ref[0, hd, LANES:2 * LANES, :] = q_rot_t
        k_ref[0, hd, :, 0:LANES] = kv[:, lo:hi].astype(BF16)
        k_ref[0, hd, :, LANES:2 * LANES] = kr_heads[hd % 2]
        for s in range(ODD_R // ATTN_T):
            vt_ref[0, hd, s] = kv[s * ATTN_T:(s + 1) * ATTN_T, MLA_V + lo:MLA_V + hi].T.astype(BF16)
    gate = jnp.dot(u, wg_ref[...], preferred_element_type=F32)
    sg_ref[...] = _silu(gate).astype(BF16)


def _odd_in(h, pre_norm, wa, wg, qn, wq, kvn, wkv, cos_t, sin_t, layer):
    R = ODD_R
    nb = SEQ // R
    o = layer // 2
    row_map = lambda b, i: (b * nb + i, 0)
    odd_map = lambda b, i: (o, 0, 0)
    return pl.pallas_call(
        _odd_in_kernel,
        grid=(BATCH, nb),
        in_specs=[
            pl.BlockSpec((R, D_MODEL), row_map),
            pl.BlockSpec((None, 1, D_MODEL), lambda b, i: (layer, 0, 0)),
            pl.BlockSpec((None, D_MODEL, ODD_A), odd_map),
            pl.BlockSpec((None, D_MODEL, MLA_V), odd_map),
            pl.BlockSpec((None, 1, Q_LORA), odd_map),
            pl.BlockSpec((None, Q_LORA, ODD_WQ), odd_map),
            pl.BlockSpec((None, 1, KV_LORA), odd_map),
            pl.BlockSpec((None, KV_LORA, 2 * MLA_V), odd_map),
            pl.BlockSpec((R, LANES), row_map),
            pl.BlockSpec((R, LANES), row_map),
        ],
        out_specs=[
            pl.BlockSpec((1, MLA_HEADS, MLA_QK_PAD, R), lambda b, i: (b, 0, 0, i)),
            pl.BlockSpec((1, MLA_HEADS, R, MLA_QK_PAD), lambda b, i: (b, 0, i, 0)),
            pl.BlockSpec((1, MLA_HEADS, R // ATTN_T, MLA_DV, ATTN_T), lambda b, i: (b, 0, i, 0, 0)),
            pl.BlockSpec((R, MLA_V), row_map),
        ],
        out_shape=[
            jax.ShapeDtypeStruct((BATCH, MLA_HEADS, MLA_QK_PAD, SEQ), BF16),
            jax.ShapeDtypeStruct((BATCH, MLA_HEADS, SEQ, MLA_QK_PAD), BF16),
            jax.ShapeDtypeStruct((BATCH, MLA_HEADS, SEQ // ATTN_T, MLA_DV, ATTN_T), BF16),
            jax.ShapeDtypeStruct((ROWS, MLA_V), BF16),
        ],
        compiler_params=pltpu.CompilerParams(
            dimension_semantics=("arbitrary", "arbitrary"), vmem_limit_bytes=VMEM_LIMIT),
        name="odd_in",
    )(h, pre_norm, wa, wg, qn, wq, kvn, wkv, cos_t, sin_t)


ATTN_HG = 4


def _attn_kernel(q_ref, qn_ref, k_ref, vt_ref, sg_ref, o_ref, m_ref, l_ref, acc_ref, st_ref):
    T = ATTN_T
    qi = pl.program_id(2)
    m_ref[...] = jnp.full(m_ref.shape, -jnp.inf, F32)
    l_ref[...] = jnp.zeros(l_ref.shape, F32)
    acc_ref[...] = jnp.zeros(acc_ref.shape, F32)

    def scores(hh, j, queries=q_ref):
        k = k_ref[0, hh, pl.ds(pl.multiple_of(j * T, T), T), :]
        return jnp.dot(k, queries[0, hh], preferred_element_type=F32)

    def consume(hh, j, st):
        m_old = m_ref[hh]
        m_new = jnp.maximum(m_old, jnp.max(st, axis=0, keepdims=True))
        alpha = jnp.exp2(m_old - m_new)
        p = jnp.exp2(st - m_new)
        l_ref[hh] = alpha * l_ref[hh] + jnp.sum(p, axis=0, keepdims=True)
        acc_ref[hh] = alpha * acc_ref[hh] + jnp.dot(
            vt_ref[0, hh, j], p.astype(BF16), preferred_element_type=F32)
        m_ref[hh] = m_new

    @pl.when(qi == 0)
    def _():
        for hh in range(ATTN_HG):
            st_ref[hh] = scores(hh, 0)

    def body(j, carry):
        for hh in range(ATTN_HG):
            st = st_ref[hh]
            st_ref[hh] = scores(hh, j + 1)
            consume(hh, j, st)
        return carry

    lax.fori_loop(0, qi, body, 0)

    H = T // 2
    causal = lax.broadcasted_iota(jnp.int32, (H, H), 0) <= lax.broadcasted_iota(jnp.int32, (H, H), 1)

    def consume_diagonal(hh, st):
        s_ee = jnp.where(causal, st[:H, :H], -jnp.inf)
        s_el = st[:H, H:]
        s_ll = jnp.where(causal, st[H:, H:], -jnp.inf)
        m_old = m_ref[hh]
        col_max = jnp.concatenate(
            [jnp.max(s_ee, axis=0, keepdims=True),
             jnp.maximum(jnp.max(s_el, axis=0, keepdims=True), jnp.max(s_ll, axis=0, keepdims=True))], axis=1)
        m_new = jnp.maximum(m_old, col_max)
        alpha = jnp.exp2(m_old - m_new)
        p_ee = jnp.exp2(s_ee - m_new[:, :H])
        p_el = jnp.exp2(s_el - m_new[:, H:])
        p_ll = jnp.exp2(s_ll - m_new[:, H:])
        col_sum = jnp.concatenate(
            [jnp.sum(p_ee, axis=0, keepdims=True),
             jnp.sum(p_el, axis=0, keepdims=True) + jnp.sum(p_ll, axis=0, keepdims=True)], axis=1)
        l_ref[hh] = alpha * l_ref[hh] + col_sum
        vt = vt_ref[0, hh, qi]
        pv = jnp.concatenate(
            [jnp.dot(vt[:, :H], p_ee.astype(BF16), preferred_element_type=F32),
             jnp.dot(vt, jnp.concatenate([p_el, p_ll], axis=0).astype(BF16), preferred_element_type=F32)], axis=1)
        acc_ref[hh] = alpha * acc_ref[hh] + pv
        m_ref[hh] = m_new

    for hh in range(ATTN_HG):
        st = st_ref[hh]
        st_ref[hh] = scores(hh, 0, qn_ref)
        consume_diagonal(hh, st)
    for hh in range(ATTN_HG):
        cols = slice(hh * MLA_DV, (hh + 1) * MLA_DV)
        o = (acc_ref[hh] * (1.0 / l_ref[hh])).T
        o_ref[0, :, cols] = (o * sg_ref[0, :, cols].astype(F32)).astype(BF16)


def _attn(q_cat, k_cat, vt, sg):
    T = ATTN_T
    HG = ATTN_HG
    nq = SEQ // T
    return pl.pallas_call(
        _attn_kernel,
        grid=(BATCH, MLA_HEADS // HG, nq),
        in_specs=[
            pl.BlockSpec((1, HG, MLA_QK_PAD, T), lambda b, h, i: (b, h, 0, i)),
            pl.BlockSpec((1, HG, MLA_QK_PAD, T), lambda b, h, i: (b, h, 0, jnp.minimum(i + 1, nq - 1))),
            pl.BlockSpec((1, HG, SEQ, MLA_QK_PAD), lambda b, h, i: (b, h, 0, 0)),
            pl.BlockSpec((1, HG, nq, MLA_DV, T), lambda b, h, i: (b, h, 0, 0, 0)),
            pl.BlockSpec((1, T, HG * MLA_DV), lambda b, h, i: (b, i, h)),
        ],
        out_specs=pl.BlockSpec((1, T, HG * MLA_DV), lambda b, h, i: (b, i, h)),
        out_shape=jax.ShapeDtypeStruct((BATCH, SEQ, MLA_V), BF16),
        scratch_shapes=[
            pltpu.VMEM((HG, 1, T), F32),
            pltpu.VMEM((HG, 1, T), F32),
            pltpu.VMEM((HG, MLA_DV, T), F32),
            pltpu.VMEM((HG, T, T), F32),
        ],
        compiler_params=pltpu.CompilerParams(
            dimension_semantics=("arbitrary", "arbitrary", "arbitrary"), vmem_limit_bytes=VMEM_LIMIT),
        name="odd_attn",
    )(q_cat, q_cat, k_cat, vt, sg.reshape(BATCH, SEQ, MLA_V))


ODD_OUT_R = 2048


def _odd_out_kernel(og_ref, h_ref, w_ref, pn_ref, out_ref):
    y = jnp.dot(og_ref[...], w_ref[...], preferred_element_type=F32)
    out_ref[...] = h_ref[...] + _rms(y, pn_ref[...])


def _odd_out(og, h, w, post_norm, layer):
    R = ODD_OUT_R
    row_map = lambda i: (i, 0)
    return pl.pallas_call(
        _odd_out_kernel,
        grid=(ROWS // R,),
        in_specs=[
            pl.BlockSpec((R, MLA_V), row_map),
            pl.BlockSpec((R, D_MODEL), row_map),
            pl.BlockSpec((None, MLA_V, D_MODEL), lambda i: (layer // 2, 0, 0)),
            pl.BlockSpec((None, 1, D_MODEL), lambda i: (layer, 0, 0)),
        ],
        out_specs=pl.BlockSpec((R, D_MODEL), row_map),
        out_shape=jax.ShapeDtypeStruct((ROWS, D_MODEL), F32),
        compiler_params=pltpu.CompilerParams(
            dimension_semantics=("arbitrary",), vmem_limit_bytes=VMEM_LIMIT),
        name="odd_out",
    )(og, h, w, post_norm)


PREP_ROW_BLOCKS = 4


def _lane_iota(rows):
    return lax.broadcasted_iota(jnp.int32, (rows, LANES), 1)


def _prep_even_kernel(win_ref, wout_ref, win_o_ref, wout_o_ref):
    rows = win_ref.shape[0]
    lane = _lane_iota(rows)
    keep = (lane < ROPE_HALF) | (lane >= LANES - ROPE_HALF)
    for t in range(_V0 // LANES):
        cs = slice(t * LANES, (t + 1) * LANES)
        x = win_ref[:, cs]
        moved = jnp.where(lane < HALF_TILE, pltpu.roll(x, LANES - ROPE_HALF, 1), pltpu.roll(x, ROPE_HALF, 1))
        y = jnp.where(keep, x, moved)
        if t * LANES >= _K0:
            y = y * (RET_DK ** -0.5)
        win_o_ref[:, cs] = y.astype(BF16)
    win_o_ref[:, _V0:_CB0] = win_ref[:, _V0:_CB0].astype(BF16)
    for t in range(CONV_W // LANES):
        for k in range(4):
            src = _CB0 + k * CONV_W + t * LANES
            dst = _CB0 + (4 * t + k) * LANES
            win_o_ref[:, dst:dst + LANES] = win_ref[:, src:src + LANES].astype(BF16)
    wout_o_ref[...] = wout_ref[...].astype(BF16)


def _prep_even(even_w_in, even_w_out):
    n = even_w_in.shape[0]
    rb = PREP_ROW_BLOCKS
    spec = lambda rows, cols: pl.BlockSpec((None, rows // rb, cols), lambda e, r: (e, r, 0))
    return pl.pallas_call(
        _prep_even_kernel,
        grid=(n, rb),
        in_specs=[spec(D_MODEL, EVEN_IN), spec(EVEN_MIX, D_MODEL)],
        out_specs=[spec(D_MODEL, EVEN_IN), spec(EVEN_MIX, D_MODEL)],
        out_shape=[jax.ShapeDtypeStruct((n, D_MODEL, EVEN_IN), BF16),
                   jax.ShapeDtypeStruct((n, EVEN_MIX, D_MODEL), BF16)],
        compiler_params=pltpu.CompilerParams(
            dimension_semantics=("arbitrary", "arbitrary"), vmem_limit_bytes=VMEM_LIMIT),
        name="prep_even",
    )(even_w_in, even_w_out)


def _prep_odd_kernel(wint_ref, wqb_ref, wkvb_ref, wout_ref, wa_ref, wg_ref, wq_ref, wkv_ref, wo_ref):
    lat = Q_LORA + KV_LORA
    cols = wint_ref.shape[1]
    wa_ref[:, :lat] = wint_ref[:lat, :].T.astype(BF16)
    zeros = jnp.zeros((ROPE_HALF, cols), F32)
    rope_rows = jnp.concatenate(
        [wint_ref[lat:lat + ROPE_HALF, :], zeros, wint_ref[lat + ROPE_HALF:lat + MLA_ROPE, :], zeros], axis=0)
    wa_ref[:, lat:] = rope_rows.T.astype(BF16)
    wg_ref[...] = wint_ref[lat + MLA_ROPE:, :].T.astype(BF16)
    lane = _lane_iota(wqb_ref.shape[0])
    for hp in range(MLA_HEADS // 2):
        t0, t1, t2 = (wqb_ref[:, (3 * hp + k) * LANES:(3 * hp + k + 1) * LANES] for k in range(3))
        ha, hb = 2 * hp, 2 * hp + 1
        wq_ref[:, ha * LANES:(ha + 1) * LANES] = t0.astype(BF16)
        wq_ref[:, hb * LANES:(hb + 1) * LANES] = jnp.where(
            lane < HALF_TILE, pltpu.roll(t1, HALF_TILE, 1), pltpu.roll(t2, HALF_TILE, 1)).astype(BF16)
        pair = jnp.where(lane < ROPE_HALF, t1,
                         jnp.where(lane < HALF_TILE, pltpu.roll(t2, LANES - ROPE_HALF, 1),
                                   jnp.where(lane < HALF_TILE + ROPE_HALF, pltpu.roll(t1, ROPE_HALF, 1), t2)))
        wq_ref[:, MLA_V + hp * LANES:MLA_V + (hp + 1) * LANES] = pair.astype(BF16)
    for hd in range(MLA_HEADS):
        wkv_ref[:, hd * LANES:(hd + 1) * LANES] = wkvb_ref[:, 2 * hd * LANES:(2 * hd + 1) * LANES].astype(BF16)
        wkv_ref[:, MLA_V + hd * LANES:MLA_V + (hd + 1) * LANES] = (
            wkvb_ref[:, (2 * hd + 1) * LANES:(2 * hd + 2) * LANES].astype(BF16))
    wo_ref[...] = wout_ref[...].astype(BF16)


def _prep_odd(odd_w_in, w_qb, w_kvb, odd_w_out):
    n = odd_w_in.shape[0]
    rb = PREP_ROW_BLOCKS
    spec = lambda rows, cols: pl.BlockSpec((None, rows // rb, cols), lambda o, r: (o, r, 0))
    shape = lambda rows, cols: jax.ShapeDtypeStruct((n, rows, cols), BF16)
    odd_w_in_t = jnp.swapaxes(odd_w_in, 1, 2)
    return pl.pallas_call(
        _prep_odd_kernel,
        grid=(n, rb),
        in_specs=[pl.BlockSpec((None, ODD_IN, D_MODEL // rb), lambda o, r: (o, 0, r)),
                  spec(Q_LORA, MLA_HEADS * MLA_QK),
                  spec(KV_LORA, 2 * MLA_V), spec(MLA_V, D_MODEL)],
        out_specs=[spec(D_MODEL, ODD_A), spec(D_MODEL, MLA_V), spec(Q_LORA, ODD_WQ),
                   spec(KV_LORA, 2 * MLA_V), spec(MLA_V, D_MODEL)],
        out_shape=[shape(D_MODEL, ODD_A), shape(D_MODEL, MLA_V), shape(Q_LORA, ODD_WQ),
                   shape(KV_LORA, 2 * MLA_V), shape(MLA_V, D_MODEL)],
        compiler_params=pltpu.CompilerParams(
            dimension_semantics=("arbitrary", "arbitrary"), vmem_limit_bytes=VMEM_LIMIT),
        name="prep_odd",
    )(odd_w_in_t, w_qb, w_kvb, odd_w_out)


def _rows3(a):
    return a.reshape(a.shape[0], 1, a.shape[1])


def kernel(x, positions, pre_norm, post_norm, even_w_in, even_conv_w, even_conv_b, ret_gn, even_w_out,
           odd_w_in, q_a_norm, w_qb, kv_a_norm, w_kvb, odd_w_out):
    cos_t, sin_t = _rope_tables(positions)
    tables = _retention_tables()
    ew_in, ew_out = _prep_even(even_w_in, even_w_out)
    wa, wg, wq, wkv, wo = _prep_odd(odd_w_in, w_qb, w_kvb, odd_w_out)
    pre_norm, post_norm, ret_gn, even_conv_b = map(_rows3, (pre_norm, post_norm, ret_gn, even_conv_b))
    q_a_norm, kv_a_norm = _rows3(q_a_norm), _rows3(kv_a_norm)
    h = x.reshape(ROWS, D_MODEL)
    for layer in range(DEPTH):
        if layer % 2 == 0:
            q, kt, vg, cmix = _even_in(h, pre_norm, ew_in, cos_t, sin_t, even_conv_w, even_conv_b, layer)
            h = _even_mix(q, kt, vg, cmix, h, ew_out, ret_gn, post_norm, tables, layer)
        else:
            q_cat, k_cat, vt, sg = _odd_in(h, pre_norm, wa, wg, q_a_norm, wq, kv_a_norm, wkv, cos_t, sin_t, layer)
            og = _attn(q_cat, k_cat, vt, sg)
            h = _odd_out(og.reshape(ROWS, MLA_V), h, wo, post_norm, layer)
    return h.reshape(BATCH, SEQ, D_MODEL)
```

```python
import math

import jax
import jax.numpy as jnp
import numpy as np
from jax import lax
from jax.experimental import pallas as pl
from jax.experimental.pallas import tpu as pltpu

D_MODEL = 1024
BATCH = 4
SEQ = 4096
DEPTH = 4
ROWS = BATCH * SEQ

RET_HEADS = 8
RET_DK = 64
RET_DV = 128
RET_CHUNK = 128
RET_PAIRS = RET_HEADS // 2
CONV_W = 1024
CONV_K = 3
MLA_HEADS = 8
MLA_NOPE = 128
MLA_ROPE = 64
MLA_DV = 128
Q_LORA = 384
KV_LORA = 256
ROPE_BASE = 10000.0
EPS = 1e-6

RET_QK = RET_HEADS * RET_DK
RET_V = RET_HEADS * RET_DV
EVEN_IN = 2 * RET_QK + 2 * RET_V + 4 * CONV_W
EVEN_MIX = RET_V + CONV_W
MLA_QK = MLA_NOPE + MLA_ROPE
MLA_V = MLA_HEADS * MLA_DV
MLA_SCALE = MLA_QK ** -0.5
LOG2E = math.log2(math.e)
MLA_QK_PAD = 256
ODD_IN = Q_LORA + KV_LORA + MLA_ROPE + MLA_V
ODD_A = Q_LORA + KV_LORA + 128
ODD_WQ = MLA_V + (MLA_HEADS // 2) * 128

_Q0, _K0, _V0, _GR0, _CB0, _CC0, _CX0, _GC0 = 0, 512, 1024, 2048, 3072, 4096, 5120, 6144

LANES = 128
SUBLANES = 8
ROPE_HALF = 32
HALF_TILE = LANES // 2

VMEM_LIMIT = 56 * 1024 * 1024

BF16 = jnp.bfloat16
F32 = jnp.float32


def _silu(x):
    return (0.5 * x) * (1.0 + jnp.tanh(0.5 * x))


def _rms(x, g):
    ms = jnp.mean(x * x, axis=-1, keepdims=True)
    return x * lax.rsqrt(ms + EPS) * g


def _rope_tile(x, cos, sin_signed):
    return x * cos + pltpu.roll(x, HALF_TILE, 1) * sin_signed


ROPE_GROUPS = LANES // ROPE_HALF
ROPE_BLK = 512


def _rope_table_kernel(pos_ref, inv_ref, cos_ref, sin_ref):
    ang = pos_ref[...].astype(F32) * inv_ref[...]
    c = jnp.cos(ang)
    s = jnp.sin(ang)
    lane = lax.broadcasted_iota(jnp.int32, (ROPE_BLK, LANES), 1)
    low = lane < ROPE_HALF
    for g in range(ROPE_GROUPS):
        shift = (LANES - g * ROPE_HALF) % LANES
        cg = jnp.where(low, c if shift == 0 else pltpu.roll(c, shift, 1), 0.0)
        sg = jnp.where(low, s if shift == 0 else pltpu.roll(s, shift, 1), 0.0)
        cg = cg + pltpu.roll(cg, ROPE_HALF, 1)
        sg = sg + pltpu.roll(sg, ROPE_HALF, 1)
        cos_ref[g] = cg + pltpu.roll(cg, HALF_TILE, 1)
        sin_ref[g] = pltpu.roll(sg, HALF_TILE, 1) - sg


def _rope_tables(positions):
    inv = ROPE_BASE ** (-jnp.arange(0, 2 * ROPE_HALF, 2, dtype=F32) / (2 * ROPE_HALF))
    dense_rows = ROWS // ROPE_GROUPS
    pos_dense = jnp.repeat(positions.reshape(ROPE_GROUPS, dense_rows).T, ROPE_HALF, axis=1)
    inv_dense = jnp.tile(inv, ROPE_GROUPS).reshape(1, LANES)
    out_spec = pl.BlockSpec((ROPE_GROUPS, ROPE_BLK, LANES), lambda i: (0, i, 0))
    out = jax.ShapeDtypeStruct((ROPE_GROUPS, dense_rows, LANES), F32)
    cos_t, sin_t = pl.pallas_call(
        _rope_table_kernel,
        grid=(dense_rows // ROPE_BLK,),
        in_specs=[pl.BlockSpec((ROPE_BLK, LANES), lambda i: (i, 0)),
                  pl.BlockSpec((1, LANES), lambda i: (0, 0))],
        out_specs=[out_spec, out_spec],
        out_shape=[out, out],
        name="rope_table",
    )(pos_dense, inv_dense)
    return cos_t.reshape(ROWS, LANES), sin_t.reshape(ROWS, LANES)


EVEN_IN_TM = 2048
EVEN_IN_TN = 1024
EVEN_QKVG = 3 * EVEN_IN_TN
CONV_BLK = EVEN_IN_TN // 4
CONV_BLOCKS = CONV_W // CONV_BLK
EVEN_IN_STRIP = 256
CONV_STRIP = 128


def _even_in_kernel(x_ref, g_ref, w_ref, cos_ref, sin_ref, cw_ref, cb_ref,
                    q_ref, kt_ref, vg_ref, cmix_ref, u_ref, tail_ref):
    i = pl.program_id(0)
    j = pl.program_id(1)
    strips = [slice(r * EVEN_IN_STRIP, (r + 1) * EVEN_IN_STRIP) for r in range(EVEN_IN_TM // EVEN_IN_STRIP)]

    halves = [slice(0, EVEN_IN_TN // 2), slice(EVEN_IN_TN // 2, EVEN_IN_TN)]

    @pl.when(j == 0)
    def _():
        for rs in strips:
            u = _rms(x_ref[rs, :], g_ref[...]).astype(BF16)
            u_ref[rs, :] = u
            cos = cos_ref[rs, :]
            sin = sin_ref[rs, :]
            for ns in halves:
                acc = jnp.dot(u, w_ref[:, ns], preferred_element_type=F32)
                for c in range((ns.stop - ns.start) // LANES):
                    cs = slice(c * LANES, (c + 1) * LANES)
                    y = _rope_tile(acc[:, cs], cos, sin)
                    if ns.start >= _K0:
                        kt_ref[cs, rs] = y.T.astype(BF16)
                    else:
                        q_ref[rs, cs] = y.astype(BF16)

    @pl.when(j == 1)
    def _():
        vg_ref[...] = jnp.dot(u_ref[...], w_ref[...], preferred_element_type=F32).astype(BF16)

    @pl.when(j == 2)
    def _():
        for rs in strips:
            for ns in halves:
                acc = jnp.dot(u_ref[rs, :], w_ref[:, ns], preferred_element_type=F32)
                vg_ref[rs, ns] = _silu(acc).astype(BF16)

    @pl.when(j >= 3)
    def _():
        slot = j - 3
        sequence_start = (i % (SEQ // EVEN_IN_TM)) == 0
        lane_tiles = [slice(ct * LANES, (ct + 1) * LANES) for ct in range(CONV_BLK // LANES)]
        halos = [jnp.where(sequence_start, 0.0, tail_ref[slot, :, cs]) for cs in lane_tiles]
        for rs in strips:
            for ct, cs in enumerate(lane_tiles):
                acc = jnp.dot(u_ref[rs, :], w_ref[:, ct * 4 * LANES:(ct + 1) * 4 * LANES],
                              preferred_element_type=F32)
                for r in range(EVEN_IN_STRIP // CONV_STRIP):
                    rr = slice(r * CONV_STRIP, (r + 1) * CONV_STRIP)
                    part = lambda k: acc[rr, k * LANES:(k + 1) * LANES]
                    prod = part(1) * part(2)
                    ext = jnp.concatenate([halos[ct], prod], axis=0)
                    conv = cb_ref[:, cs] + prod * cw_ref[CONV_K - 1:CONV_K, cs]
                    for tap in range(CONV_K - 1):
                        off = SUBLANES - (CONV_K - 1) + tap
                        conv = conv + ext[off:off + CONV_STRIP, :] * cw_ref[tap:tap + 1, cs]
                    halos[ct] = prod[CONV_STRIP - SUBLANES:, :]
                    out_rows = slice(rs.start + rr.start, rs.start + rr.stop)
                    cmix_ref[out_rows, cs] = (part(0) * conv * _silu(part(3))).astype(BF16)
        for ct, cs in enumerate(lane_tiles):
            tail_ref[slot, :, cs] = halos[ct]


def _even_in(h, pre_norm, w, cos_t, sin_t, conv_w, conv_b, layer):
    tm, tn = EVEN_IN_TM, EVEN_IN_TN
    e = layer // 2
    qkvg_blocks = EVEN_QKVG // tn
    vg_blocks = qkvg_blocks - 1
    conv_map = lambda i, j: (e, 0, jnp.maximum(j - qkvg_blocks, 0))
    return pl.pallas_call(
        _even_in_kernel,
        grid=(ROWS // tm, EVEN_IN // tn),
        in_specs=[
            pl.BlockSpec((tm, D_MODEL), lambda i, j: (i, 0)),
            pl.BlockSpec((None, 1, D_MODEL), lambda i, j: (layer, 0, 0)),
            pl.BlockSpec((None, D_MODEL, tn), lambda i, j: (e, 0, j)),
            pl.BlockSpec((tm, LANES), lambda i, j: (i, 0)),
            pl.BlockSpec((tm, LANES), lambda i, j: (i, 0)),
            pl.BlockSpec((None, CONV_K, CONV_BLK), conv_map),
            pl.BlockSpec((None, 1, CONV_BLK), conv_map),
        ],
        out_specs=[
            pl.BlockSpec((tm, RET_QK), lambda i, j: (i, 0)),
            pl.BlockSpec((RET_QK, tm), lambda i, j: (0, i)),
            pl.BlockSpec((tm, tn), lambda i, j: (i, jnp.clip(j - 1, 0, vg_blocks - 1))),
            pl.BlockSpec((tm, CONV_BLK), lambda i, j: (i, jnp.maximum(j - qkvg_blocks, 0))),
        ],
        out_shape=[
            jax.ShapeDtypeStruct((ROWS, RET_QK), BF16),
            jax.ShapeDtypeStruct((RET_QK, ROWS), BF16),
            jax.ShapeDtypeStruct((ROWS, vg_blocks * tn), BF16),
            jax.ShapeDtypeStruct((ROWS, CONV_W), BF16),
        ],
        scratch_shapes=[
            pltpu.VMEM((tm, D_MODEL), BF16),
            pltpu.VMEM((CONV_BLOCKS, SUBLANES, CONV_BLK), F32),
        ],
        compiler_params=pltpu.CompilerParams(
            dimension_semantics=("arbitrary", "arbitrary"), vmem_limit_bytes=VMEM_LIMIT),
        name="even_in",
    )(h, pre_norm, w, cos_t, sin_t, conv_w, conv_b)


EVEN_MIX_R = 512
EVEN_MIX_STEPS = ROWS // EVEN_MIX_R
PROJ_PIECE = 256


def _even_mix_block(q_ref, kt_ref, vg_ref, cmix_ref, h_ref, wout_ref, gn_ref, pn_ref,
                    intra_ref, xi_ref, zeta_ref, gam_ref, bm_ref,
                    o_ref, state_ref, mix_prev_ref, mix_ref):
    C = RET_CHUNK
    chunks = EVEN_MIX_R // C

    row_k = lax.broadcasted_iota(jnp.int32, (LANES, C), 0)
    first_head = (row_k % HALF_TILE) < ROPE_HALF
    lane_v = lax.broadcasted_iota(jnp.int32, (C, 2 * RET_DV), 1)
    zero_k = jnp.zeros((LANES, C), BF16)
    zero_v = jnp.zeros((C, 2 * RET_DV), BF16)

    y_prev = []

    def project_piece():
        ns = slice(len(y_prev) * PROJ_PIECE, (len(y_prev) + 1) * PROJ_PIECE)
        y_prev.append(jnp.dot(mix_prev_ref[...], wout_ref[:RET_V, ns], preferred_element_type=F32)
                      + jnp.dot(cmix_ref[...], wout_ref[RET_V:, ns], preferred_element_type=F32))

    stages_per_piece = chunks * RET_PAIRS * PROJ_PIECE // D_MODEL
    for c, j in [(c, p) for c in range(chunks) for p in range(RET_PAIRS)]:
        rows = slice(c * C, (c + 1) * C)
        if (c * RET_PAIRS + j) % stages_per_piece == 0:
            project_piece()
        qp = q_ref[rows, j * LANES:(j + 1) * LANES]
        kt = kt_ref[j * LANES:(j + 1) * LANES, rows]
        vp = vg_ref[rows, j * 2 * RET_DV:(j + 1) * 2 * RET_DV]
        kt_heads = jnp.concatenate(
            [jnp.where(first_head, kt, zero_k), jnp.where(first_head, zero_k, kt)], axis=1)
        sc = jnp.dot(qp, kt_heads, preferred_element_type=F32)
        pm = (sc * intra_ref[j]).astype(BF16)
        vblk = jnp.concatenate(
            [jnp.where(lane_v < RET_DV, vp, zero_v), jnp.where(lane_v >= RET_DV, vp, zero_v)], axis=0)
        st = state_ref[j]
        r = (jnp.dot(pm, vblk, preferred_element_type=F32)
             + jnp.dot(qp, st.astype(BF16), preferred_element_type=F32) * xi_ref[j])
        kz = (kt.astype(F32) * zeta_ref[j]).astype(BF16)
        kv = jnp.dot(kz, vp, preferred_element_type=F32)
        state_ref[j] = st * gam_ref[j] + kv * bm_ref[...]
        for hh in range(2):
            hd = 2 * j + hh
            cols = slice(hd * RET_DV, (hd + 1) * RET_DV)
            y = _rms(r[:, hh * RET_DV:(hh + 1) * RET_DV], gn_ref[:, cols])
            gate = vg_ref[rows, RET_V + hd * RET_DV:RET_V + (hd + 1) * RET_DV].astype(F32)
            mix_ref[rows, cols] = (y * gate).astype(BF16)

    o_ref[...] = h_ref[...] + _rms(jnp.concatenate(y_prev, axis=1), pn_ref[...])


def _even_mix_kernel(q_ref, kt_ref, vg_ref, cmix_ref, h_ref, wout_ref, gn_ref, pn_ref,
                     intra_ref, xi_ref, zeta_ref, gam_ref, bm_ref,
                     o_ref, state_ref, mix_a_ref, mix_b_ref):
    t = pl.program_id(0)
    args = (q_ref, kt_ref, vg_ref, cmix_ref, h_ref, wout_ref, gn_ref, pn_ref,
            intra_ref, xi_ref, zeta_ref, gam_ref, bm_ref, o_ref, state_ref)

    @pl.when(t % (SEQ // EVEN_MIX_R) == 0)
    def _():
        state_ref[...] = jnp.zeros_like(state_ref)

    @pl.when(t == 0)
    def _():
        mix_b_ref[...] = jnp.zeros_like(mix_b_ref)

    last = EVEN_MIX_STEPS
    assert last % 2 == 0

    @pl.when((t % 2 == 0) & (t < last))
    def _():
        _even_mix_block(*args, mix_b_ref, mix_a_ref)

    @pl.when(t % 2 == 1)
    def _():
        _even_mix_block(*args, mix_a_ref, mix_b_ref)

    @pl.when(t == last)
    def _():
        y = (jnp.dot(mix_b_ref[...], wout_ref[:RET_V, :], preferred_element_type=F32)
             + jnp.dot(cmix_ref[...], wout_ref[RET_V:, :], preferred_element_type=F32))
        o_ref[...] = h_ref[...] + _rms(y, pn_ref[...])


def _retention_tables():
    h, c = RET_HEADS, RET_CHUNK
    log_gamma = np.log1p(-np.exp2(-5.0 - np.arange(h, dtype=np.float64)))
    i = np.arange(c, dtype=np.float64)
    rel = i[:, None] - i[None, :]
    intra = np.where(rel >= 0, np.exp(log_gamma[:, None, None] * np.maximum(rel, 0.0)), 0.0)
    xi = np.exp(log_gamma[:, None] * (i + 1.0))
    zeta = np.exp(log_gamma[:, None] * (c - 1.0 - i))
    gamma_c = np.exp(log_gamma * c)
    pair = np.arange(RET_PAIRS)
    intra_p = intra.reshape(RET_PAIRS, 2, c, c).transpose(0, 2, 1, 3).reshape(RET_PAIRS, c, 2 * c)
    xi_p = np.repeat(xi.reshape(RET_PAIRS, 2, c).transpose(0, 2, 1), RET_DV, axis=2)
    second = ((np.arange(2 * RET_DK) % HALF_TILE) >= ROPE_HALF).astype(np.int64)
    zeta_p = zeta.reshape(RET_PAIRS, 2, c)[pair[:, None, None], second[None, :, None], np.arange(c)[None, None, :]]
    gam_rows = gamma_c.reshape(RET_PAIRS, 2)[:, second]
    gam_p = np.broadcast_to(gam_rows[:, :, None], (RET_PAIRS, 2 * RET_DK, 2 * RET_DV))
    col_head = np.arange(2 * RET_DV) // RET_DV
    bm = (second[:, None] == col_head[None, :])
    as_f32 = lambda a: jnp.asarray(np.ascontiguousarray(a, dtype=np.float32))
    return as_f32(intra_p), as_f32(xi_p), as_f32(zeta_p), as_f32(gam_p), as_f32(bm)


def _even_mix(q, kt, vg, cmix, h, w_out, gn, post_norm, tables, layer):
    R = EVEN_MIX_R
    nt = EVEN_MIX_STEPS
    e = layer // 2
    intra_p, xi_p, zeta_p, gam_p, bm = tables
    mix_map = lambda t: (jnp.minimum(t, nt - 1), 0)
    row_map = lambda t: (jnp.maximum(t - 1, 0), 0)
    const2 = lambda t: (0, 0)
    const3 = lambda t: (0, 0, 0)
    return pl.pallas_call(
        _even_mix_kernel,
        grid=(nt + 1,),
        in_specs=[
            pl.BlockSpec((R, RET_QK), mix_map),
            pl.BlockSpec((RET_QK, R), lambda t: (0, jnp.minimum(t, nt - 1))),
            pl.BlockSpec((R, 2 * RET_V), mix_map),
            pl.BlockSpec((R, CONV_W), row_map),
            pl.BlockSpec((R, D_MODEL), row_map),
            pl.BlockSpec((None, EVEN_MIX, D_MODEL), lambda t: (e, 0, 0)),
            pl.BlockSpec((None, 1, RET_V), lambda t: (e, 0, 0)),
            pl.BlockSpec((None, 1, D_MODEL), lambda t: (layer, 0, 0)),
            pl.BlockSpec((RET_PAIRS, RET_CHUNK, 2 * RET_CHUNK), const3),
            pl.BlockSpec((RET_PAIRS, RET_CHUNK, 2 * RET_DV), const3),
            pl.BlockSpec((RET_PAIRS, RET_CHUNK, 2 * RET_DK), const3),
            pl.BlockSpec((RET_PAIRS, 2 * RET_DK, 2 * RET_DV), const3),
            pl.BlockSpec((2 * RET_DK, 2 * RET_DV), const2),
        ],
        out_specs=pl.BlockSpec((R, D_MODEL), row_map),
        out_shape=jax.ShapeDtypeStruct((ROWS, D_MODEL), F32),
        scratch_shapes=[
            pltpu.VMEM((RET_PAIRS, 2 * RET_DK, 2 * RET_DV), F32),
            pltpu.VMEM((R, RET_V), BF16),
            pltpu.VMEM((R, RET_V), BF16),
        ],
        compiler_params=pltpu.CompilerParams(
            dimension_semantics=("arbitrary",), vmem_limit_bytes=VMEM_LIMIT),
        name="even_mix",
    )(q, kt, vg, cmix, h, w_out, gn, post_norm, intra_p, xi_p, zeta_p, gam_p, bm)


ODD_R = 1024
ATTN_T = 512


def _odd_in_kernel(x_ref, g_ref, wa_ref, wg_ref, qn_ref, wq_ref, kvn_ref, wkv_ref, cos_ref, sin_ref,
                   q_ref, k_ref, vt_ref, sg_ref):
    u = _rms(x_ref[...], g_ref[...]).astype(BF16)
    a = jnp.dot(u, wa_ref[...], preferred_element_type=F32)
    cq = a[:, :Q_LORA]
    ckv = a[:, Q_LORA:Q_LORA + KV_LORA]
    kr = a[:, Q_LORA + KV_LORA:]
    q = jnp.dot(_rms(cq, qn_ref[...]).astype(BF16), wq_ref[...], preferred_element_type=F32) * (MLA_SCALE * LOG2E)
    kv = jnp.dot(_rms(ckv, kvn_ref[...]).astype(BF16), wkv_ref[...], preferred_element_type=F32)
    cos = cos_ref[...]
    sin = sin_ref[...]
    kr_rot = _rope_tile(kr, cos, sin)
    kr_heads = (kr_rot.astype(BF16), pltpu.roll(kr_rot, ROPE_HALF, 1).astype(BF16))
    for hd in range(MLA_HEADS):
        lo, hi = hd * LANES, (hd + 1) * LANES
        q_ref[0, hd, 0:LANES, :] = q[:, lo:hi].T.astype(BF16)
        if hd % 2 == 0:
            pair = MLA_V + (hd // 2) * LANES
            q_rot_t = _rope_tile(q[:, pair:pair + LANES], cos, sin).T.astype(BF16)
        q_ref[0, hd, LANES:2 * LANES, :] = q_rot_t
        k_ref[0, hd, :, 0:LANES] = kv[:, lo:hi].astype(BF16)
        k_ref[0, hd, :, LANES:2 * LANES] = kr_heads[hd % 2]
        for s in range(ODD_R // ATTN_T):
            vt_ref[0, hd, s] = kv[s * ATTN_T:(s + 1) * ATTN_T, MLA_V + lo:MLA_V + hi].T.astype(BF16)
    gate = jnp.dot(u, wg_ref[...], preferred_element_type=F32)
    sg_ref[...] = _silu(gate).astype(BF16)


def _odd_in(h, pre_norm, wa, wg, qn, wq, kvn, wkv, cos_t, sin_t, layer):
    R = ODD_R
    nb = SEQ // R
    o = layer // 2
    row_map = lambda b, i: (b * nb + i, 0)
    odd_map = lambda b, i: (o, 0, 0)
    return pl.pallas_call(
        _odd_in_kernel,
        grid=(BATCH, nb),
        in_specs=[
            pl.BlockSpec((R, D_MODEL), row_map),
            pl.BlockSpec((None, 1, D_MODEL), lambda b, i: (layer, 0, 0)),
            pl.BlockSpec((None, D_MODEL, ODD_A), odd_map),
            pl.BlockSpec((None, D_MODEL, MLA_V), odd_map),
            pl.BlockSpec((None, 1, Q_LORA), odd_map),
            pl.BlockSpec((None, Q_LORA, ODD_WQ), odd_map),
            pl.BlockSpec((None, 1, KV_LORA), odd_map),
            pl.BlockSpec((None, KV_LORA, 2 * MLA_V), odd_map),
            pl.BlockSpec((R, LANES), row_map),
            pl.BlockSpec((R, LANES), row_map),
        ],
        out_specs=[
            pl.BlockSpec((1, MLA_HEADS, MLA_QK_PAD, R), lambda b, i: (b, 0, 0, i)),
            pl.BlockSpec((1, MLA_HEADS, R, MLA_QK_PAD), lambda b, i: (b, 0, i, 0)),
            pl.BlockSpec((1, MLA_HEADS, R // ATTN_T, MLA_DV, ATTN_T), lambda b, i: (b, 0, i, 0, 0)),
            pl.BlockSpec((R, MLA_V), row_map),
        ],
        out_shape=[
            jax.ShapeDtypeStruct((BATCH, MLA_HEADS, MLA_QK_PAD, SEQ), BF16),
            jax.ShapeDtypeStruct((BATCH, MLA_HEADS, SEQ, MLA_QK_PAD), BF16),
            jax.ShapeDtypeStruct((BATCH, MLA_HEADS, SEQ // ATTN_T, MLA_DV, ATTN_T), BF16),
            jax.ShapeDtypeStruct((ROWS, MLA_V), BF16),
        ],
        compiler_params=pltpu.CompilerParams(
            dimension_semantics=("arbitrary", "arbitrary"), vmem_limit_bytes=VMEM_LIMIT),
        name="odd_in",
    )(h, pre_norm, wa, wg, qn, wq, kvn, wkv, cos_t, sin_t)


ATTN_HG = 4


def _attn_kernel(q_ref, qn_ref, k_ref, vt_ref, sg_ref, o_ref, m_ref, l_ref, acc_ref, st_ref):
    T = ATTN_T
    qi = pl.program_id(2)
    m_ref[...] = jnp.full(m_ref.shape, -jnp.inf, F32)
    l_ref[...] = jnp.zeros(l_ref.shape, F32)
    acc_ref[...] = jnp.zeros(acc_ref.shape, F32)

    def scores(hh, j, queries=q_ref):
        k = k_ref[0, hh, pl.ds(pl.multiple_of(j * T, T), T), :]
        return jnp.dot(k, queries[0, hh], preferred_element_type=F32)

    def consume(hh, j, st):
        m_old = m_ref[hh]
        m_new = jnp.maximum(m_old, jnp.max(st, axis=0, keepdims=True))
        alpha = jnp.exp2(m_old - m_new)
        p = jnp.exp2(st - m_new)
        l_ref[hh] = alpha * l_ref[hh] + jnp.sum(p, axis=0, keepdims=True)
        acc_ref[hh] = alpha * acc_ref[hh] + jnp.dot(
            vt_ref[0, hh, j], p.astype(BF16), preferred_element_type=F32)
        m_ref[hh] = m_new

    @pl.when(qi == 0)
    def _():
        for hh in range(ATTN_HG):
            st_ref[hh] = scores(hh, 0)

    def body(j, carry):
        for hh in range(ATTN_HG):
            st = st_ref[hh]
            st_ref[hh] = scores(hh, j + 1)
            consume(hh, j, st)
        return carry

    lax.fori_loop(0, qi, body, 0)

    H = T // 2
    causal = lax.broadcasted_iota(jnp.int32, (H, H), 0) <= lax.broadcasted_iota(jnp.int32, (H, H), 1)

    def consume_diagonal(hh, st):
        s_ee = jnp.where(causal, st[:H, :H], -jnp.inf)
        s_el = st[:H, H:]
        s_ll = jnp.where(causal, st[H:, H:], -jnp.inf)
        m_old = m_ref[hh]
        col_max = jnp.concatenate(
            [jnp.max(s_ee, axis=0, keepdims=True),
             jnp.maximum(jnp.max(s_el, axis=0, keepdims=True), jnp.max(s_ll, axis=0, keepdims=True))], axis=1)
        m_new = jnp.maximum(m_old, col_max)
        alpha = jnp.exp2(m_old - m_new)
        p_ee = jnp.exp2(s_ee - m_new[:, :H])
        p_el = jnp.exp2(s_el - m_new[:, H:])
        p_ll = jnp.exp2(s_ll - m_new[:, H:])
        col_sum = jnp.concatenate(
            [jnp.sum(p_ee, axis=0, keepdims=True),
             jnp.sum(p_el, axis=0, keepdims=True) + jnp.sum(p_ll, axis=0, keepdims=True)], axis=1)
        l_ref[hh] = alpha * l_ref[hh] + col_sum
        vt = vt_ref[0, hh, qi]
        pv = jnp.concatenate(
            [jnp.dot(vt[:, :H], p_ee.astype(BF16), preferred_element_type=F32),
             jnp.dot(vt, jnp.concatenate([p_el, p_ll], axis=0).astype(BF16), preferred_element_type=F32)], axis=1)
        acc_ref[hh] = alpha * acc_ref[hh] + pv
        m_ref[hh] = m_new

    for hh in range(ATTN_HG):
        st = st_ref[hh]
        st_ref[hh] = scores(hh, 0, qn_ref)
        consume_diagonal(hh, st)
    for hh in range(ATTN_HG):
        cols = slice(hh * MLA_DV, (hh + 1) * MLA_DV)
        o = (acc_ref[hh] * (1.0 / l_ref[hh])).T
        o_ref[0, :, cols] = (o * sg_ref[0, :, cols].astype(F32)).astype(BF16)


def _attn(q_cat, k_cat, vt, sg):
    T = ATTN_T
    HG = ATTN_HG
    nq = SEQ // T
    return pl.pallas_call(
        _attn_kernel,
        grid=(BATCH, MLA_HEADS // HG, nq),
        in_specs=[
            pl.BlockSpec((1, HG, MLA_QK_PAD, T), lambda b, h, i: (b, h, 0, i)),
            pl.BlockSpec((1, HG, MLA_QK_PAD, T), lambda b, h, i: (b, h, 0, jnp.minimum(i + 1, nq - 1))),
            pl.BlockSpec((1, HG, SEQ, MLA_QK_PAD), lambda b, h, i: (b, h, 0, 0)),
            pl.BlockSpec((1, HG, nq, MLA_DV, T), lambda b, h, i: (b, h, 0, 0, 0)),
            pl.BlockSpec((1, T, HG * MLA_DV), lambda b, h, i: (b, i, h)),
        ],
        out_specs=pl.BlockSpec((1, T, HG * MLA_DV), lambda b, h, i: (b, i, h)),
        out_shape=jax.ShapeDtypeStruct((BATCH, SEQ, MLA_V), BF16),
        scratch_shapes=[
            pltpu.VMEM((HG, 1, T), F32),
            pltpu.VMEM((HG, 1, T), F32),
            pltpu.VMEM((HG, MLA_DV, T), F32),
            pltpu.VMEM((HG, T, T), F32),
        ],
        compiler_params=pltpu.CompilerParams(
            dimension_semantics=("arbitrary", "arbitrary", "arbitrary"), vmem_limit_bytes=VMEM_LIMIT),
        name="odd_attn",
    )(q_cat, q_cat, k_cat, vt, sg.reshape(BATCH, SEQ, MLA_V))


ODD_OUT_R = 2048


def _odd_out_kernel(og_ref, h_ref, w_ref, pn_ref, out_ref):
    y = jnp.dot(og_ref[...], w_ref[...], preferred_element_type=F32)
    out_ref[...] = h_ref[...] + _rms(y, pn_ref[...])


def _odd_out(og, h, w, post_norm, layer):
    R = ODD_OUT_R
    row_map = lambda i: (i, 0)
    return pl.pallas_call(
        _odd_out_kernel,
        grid=(ROWS // R,),
        in_specs=[
            pl.BlockSpec((R, MLA_V), row_map),
            pl.BlockSpec((R, D_MODEL), row_map),
            pl.BlockSpec((None, MLA_V, D_MODEL), lambda i: (layer // 2, 0, 0)),
            pl.BlockSpec((None, 1, D_MODEL), lambda i: (layer, 0, 0)),
        ],
        out_specs=pl.BlockSpec((R, D_MODEL), row_map),
        out_shape=jax.ShapeDtypeStruct((ROWS, D_MODEL), F32),
        compiler_params=pltpu.CompilerParams(
            dimension_semantics=("arbitrary",), vmem_limit_bytes=VMEM_LIMIT),
        name="odd_out",
    )(og, h, w, post_norm)


PREP_ROW_BLOCKS = 4


def _lane_iota(rows):
    return lax.broadcasted_iota(jnp.int32, (rows, LANES), 1)


def _prep_even_kernel(win_ref, wout_ref, win_o_ref, wout_o_ref):
    rows = win_ref.shape[0]
    lane = _lane_iota(rows)
    keep = (lane < ROPE_HALF) | (lane >= LANES - ROPE_HALF)
    for t in range(_V0 // LANES):
        cs = slice(t * LANES, (t + 1) * LANES)
        x = win_ref[:, cs]
        moved = jnp.where(lane < HALF_TILE, pltpu.roll(x, LANES - ROPE_HALF, 1), pltpu.roll(x, ROPE_HALF, 1))
        y = jnp.where(keep, x, moved)
        if t * LANES >= _K0:
            y = y * (RET_DK ** -0.5)
        win_o_ref[:, cs] = y.astype(BF16)
    win_o_ref[:, _V0:_CB0] = win_ref[:, _V0:_CB0].astype(BF16)
    for t in range(CONV_W // LANES):
        for k in range(4):
            src = _CB0 + k * CONV_W + t * LANES
            dst = _CB0 + (4 * t + k) * LANES
            win_o_ref[:, dst:dst + LANES] = win_ref[:, src:src + LANES].astype(BF16)
    wout_o_ref[...] = wout_ref[...].astype(BF16)


def _prep_even(even_w_in, even_w_out):
    n = even_w_in.shape[0]
    rb = PREP_ROW_BLOCKS
    spec = lambda rows, cols: pl.BlockSpec((None, rows // rb, cols), lambda e, r: (e, r, 0))
    return pl.pallas_call(
        _prep_even_kernel,
        grid=(n, rb),
        in_specs=[spec(D_MODEL, EVEN_IN), spec(EVEN_MIX, D_MODEL)],
        out_specs=[spec(D_MODEL, EVEN_IN), spec(EVEN_MIX, D_MODEL)],
        out_shape=[jax.ShapeDtypeStruct((n, D_MODEL, EVEN_IN), BF16),
                   jax.ShapeDtypeStruct((n, EVEN_MIX, D_MODEL), BF16)],
        compiler_params=pltpu.CompilerParams(
            dimension_semantics=("arbitrary", "arbitrary"), vmem_limit_bytes=VMEM_LIMIT),
        name="prep_even",
    )(even_w_in, even_w_out)


def _prep_odd_kernel(wint_ref, wqb_ref, wkvb_ref, wout_ref, wa_ref, wg_ref, wq_ref, wkv_ref, wo_ref):
    lat = Q_LORA + KV_LORA
    cols = wint_ref.shape[1]
    wa_ref[:, :lat] = wint_ref[:lat, :].T.astype(BF16)
    zeros = jnp.zeros((ROPE_HALF, cols), F32)
    rope_rows = jnp.concatenate(
        [wint_ref[lat:lat + ROPE_HALF, :], zeros, wint_ref[lat + ROPE_HALF:lat + MLA_ROPE, :], zeros], axis=0)
    wa_ref[:, lat:] = rope_rows.T.astype(BF16)
    wg_ref[...] = wint_ref[lat + MLA_ROPE:, :].T.astype(BF16)
    lane = _lane_iota(wqb_ref.shape[0])
    for hp in range(MLA_HEADS // 2):
        t0, t1, t2 = (wqb_ref[:, (3 * hp + k) * LANES:(3 * hp + k + 1) * LANES] for k in range(3))
        ha, hb = 2 * hp, 2 * hp + 1
        wq_ref[:, ha * LANES:(ha + 1) * LANES] = t0.astype(BF16)
        wq_ref[:, hb * LANES:(hb + 1) * LANES] = jnp.where(
            lane < HALF_TILE, pltpu.roll(t1, HALF_TILE, 1), pltpu.roll(t2, HALF_TILE, 1)).astype(BF16)
        pair = jnp.where(lane < ROPE_HALF, t1,
                         jnp.where(lane < HALF_TILE, pltpu.roll(t2, LANES - ROPE_HALF, 1),
                                   jnp.where(lane < HALF_TILE + ROPE_HALF, pltpu.roll(t1, ROPE_HALF, 1), t2)))
        wq_ref[:, MLA_V + hp * LANES:MLA_V + (hp + 1) * LANES] = pair.astype(BF16)
    for hd in range(MLA_HEADS):
        wkv_ref[:, hd * LANES:(hd + 1) * LANES] = wkvb_ref[:, 2 * hd * LANES:(2 * hd + 1) * LANES].astype(BF16)
        wkv_ref[:, MLA_V + hd * LANES:MLA_V + (hd + 1) * LANES] = (
            wkvb_ref[:, (2 * hd + 1) * LANES:(2 * hd + 2) * LANES].astype(BF16))
    wo_ref[...] = wout_ref[...].astype(BF16)


def _prep_odd(odd_w_in, w_qb, w_kvb, odd_w_out):
    n = odd_w_in.shape[0]
    rb = PREP_ROW_BLOCKS
    spec = lambda rows, cols: pl.BlockSpec((None, rows // rb, cols), lambda o, r: (o, r, 0))
    shape = lambda rows, cols: jax.ShapeDtypeStruct((n, rows, cols), BF16)
    odd_w_in_t = jnp.swapaxes(odd_w_in, 1, 2)
    return pl.pallas_call(
        _prep_odd_kernel,
        grid=(n, rb),
        in_specs=[pl.BlockSpec((None, ODD_IN, D_MODEL // rb), lambda o, r: (o, 0, r)),
                  spec(Q_LORA, MLA_HEADS * MLA_QK),
                  spec(KV_LORA, 2 * MLA_V), spec(MLA_V, D_MODEL)],
        out_specs=[spec(D_MODEL, ODD_A), spec(D_MODEL, MLA_V), spec(Q_LORA, ODD_WQ),
                   spec(KV_LORA, 2 * MLA_V), spec(MLA_V, D_MODEL)],
        out_shape=[shape(D_MODEL, ODD_A), shape(D_MODEL, MLA_V), shape(Q_LORA, ODD_WQ),
                   shape(KV_LORA, 2 * MLA_V), shape(MLA_V, D_MODEL)],
        compiler_params=pltpu.CompilerParams(
            dimension_semantics=("arbitrary", "arbitrary"), vmem_limit_bytes=VMEM_LIMIT),
        name="prep_odd",
    )(odd_w_in_t, w_qb, w_kvb, odd_w_out)


def _rows3(a):
    return a.reshape(a.shape[0], 1, a.shape[1])


def kernel(x, positions, pre_norm, post_norm, even_w_in, even_conv_w, even_conv_b, ret_gn, even_w_out,
           odd_w_in, q_a_norm, w_qb, kv_a_norm, w_kvb, odd_w_out):
    cos_t, sin_t = _rope_tables(positions)
    tables = _retention_tables()
    ew_in, ew_out = _prep_even(even_w_in, even_w_out)
    wa, wg, wq, wkv, wo = _prep_odd(odd_w_in, w_qb, w_kvb, odd_w_out)
    pre_norm, post_norm, ret_gn, even_conv_b = map(_rows3, (pre_norm, post_norm, ret_gn, even_conv_b))
    q_a_norm, kv_a_norm = _rows3(q_a_norm), _rows3(kv_a_norm)
    h = x.reshape(ROWS, D_MODEL)
    for layer in range(DEPTH):
        if layer % 2 == 0:
            q, kt, vg, cmix = _even_in(h, pre_norm, ew_in, cos_t, sin_t, even_conv_w, even_conv_b, layer)
            h = _even_mix(q, kt, vg, cmix, h, ew_out, ret_gn, post_norm, tables, layer)
        else:
            q_cat, k_cat, vt, sg = _odd_in(h, pre_norm, wa, wg, q_a_norm, wq, kv_a_norm, wkv, cos_t, sin_t, layer)
            og = _attn(q_cat, k_cat, vt, sg)
            h = _odd_out(og.reshape(ROWS, MLA_V), h, wo, post_norm, layer)
    return h.reshape(BATCH, SEQ, D_MODEL)
```

```python
import math

import jax
import jax.numpy as jnp
import numpy as np
from jax import lax
from jax.experimental import pallas as pl
from jax.experimental.pallas import tpu as pltpu

D_MODEL = 1024
BATCH = 4
SEQ = 4096
DEPTH = 4
ROWS = BATCH * SEQ

RET_HEADS = 8
RET_DK = 64
RET_DV = 128
RET_CHUNK = 128
RET_PAIRS = RET_HEADS // 2
CONV_W = 1024
CONV_K = 3
MLA_HEADS = 8
MLA_NOPE = 128
MLA_ROPE = 64
MLA_DV = 128
Q_LORA = 384
KV_LORA = 256
ROPE_BASE = 10000.0
EPS = 1e-6

RET_QK = RET_HEADS * RET_DK
RET_V = RET_HEADS * RET_DV
EVEN_IN = 2 * RET_QK + 2 * RET_V + 4 * CONV_W
EVEN_MIX = RET_V + CONV_W
MLA_QK = MLA_NOPE + MLA_ROPE
MLA_V = MLA_HEADS * MLA_DV
MLA_SCALE = MLA_QK ** -0.5
LOG2E = math.log2(math.e)
MLA_QK_PAD = 256
ODD_IN = Q_LORA + KV_LORA + MLA_ROPE + MLA_V
ODD_A = Q_LORA + KV_LORA + 128
ODD_WQ = MLA_V + (MLA_HEADS // 2) * 128

_Q0, _K0, _V0, _GR0, _CB0, _CC0, _CX0, _GC0 = 0, 512, 1024, 2048, 3072, 4096, 5120, 6144

LANES = 128
SUBLANES = 8
ROPE_HALF = 32
HALF_TILE = LANES // 2

VMEM_LIMIT = 56 * 1024 * 1024

BF16 = jnp.bfloat16
F32 = jnp.float32


def _silu(x):
    return (0.5 * x) * (1.0 + jnp.tanh(0.5 * x))


def _rms(x, g):
    ms = jnp.mean(x * x, axis=-1, keepdims=True)
    return x * lax.rsqrt(ms + EPS) * g


def _rope_tile(x, cos, sin_signed):
    return x * cos + pltpu.roll(x, HALF_TILE, 1) * sin_signed


ROPE_GROUPS = LANES // ROPE_HALF
ROPE_BLK = 1024


def _rope_table_kernel(pos_ref, inv_ref, cos_ref, sin_ref):
    ang = pos_ref[...].astype(F32) * inv_ref[...]
    c = jnp.cos(ang)
    s = jnp.sin(ang)
    lane = lax.broadcasted_iota(jnp.int32, (ROPE_BLK, LANES), 1)
    low = lane < ROPE_HALF
    for g in range(ROPE_GROUPS):
        shift = (LANES - g * ROPE_HALF) % LANES
        cg = jnp.where(low, c if shift == 0 else pltpu.roll(c, shift, 1), 0.0)
        sg = jnp.where(low, s if shift == 0 else pltpu.roll(s, shift, 1), 0.0)
        cg = cg + pltpu.roll(cg, ROPE_HALF, 1)
        sg = sg + pltpu.roll(sg, ROPE_HALF, 1)
        cos_ref[g] = cg + pltpu.roll(cg, HALF_TILE, 1)
        sin_ref[g] = pltpu.roll(sg, HALF_TILE, 1) - sg


def _rope_tables(positions):
    inv = ROPE_BASE ** (-jnp.arange(0, 2 * ROPE_HALF, 2, dtype=F32) / (2 * ROPE_HALF))
    dense_rows = ROWS // ROPE_GROUPS
    pos_dense = jnp.repeat(positions.reshape(ROPE_GROUPS, dense_rows).T, ROPE_HALF, axis=1)
    inv_dense = jnp.tile(inv, ROPE_GROUPS).reshape(1, LANES)
    out_spec = pl.BlockSpec((ROPE_GROUPS, ROPE_BLK, LANES), lambda i: (0, i, 0))
    out = jax.ShapeDtypeStruct((ROPE_GROUPS, dense_rows, LANES), F32)
    cos_t, sin_t = pl.pallas_call(
        _rope_table_kernel,
        grid=(dense_rows // ROPE_BLK,),
        in_specs=[pl.BlockSpec((ROPE_BLK, LANES), lambda i: (i, 0)),
                  pl.BlockSpec((1, LANES), lambda i: (0, 0))],
        out_specs=[out_spec, out_spec],
        out_shape=[out, out],
        name="rope_table",
    )(pos_dense, inv_dense)
    return cos_t.reshape(ROWS, LANES), sin_t.reshape(ROWS, LANES)


EVEN_IN_TM = 2048
EVEN_IN_TN = 1024
EVEN_QKVG = 3 * EVEN_IN_TN
CONV_BLK = EVEN_IN_TN // 4
CONV_BLOCKS = CONV_W // CONV_BLK
EVEN_IN_STRIP = 256
CONV_STRIP = 128


def _even_in_kernel(x_ref, g_ref, w_ref, cos_ref, sin_ref, cw_ref, cb_ref,
                    q_ref, kt_ref, vg_ref, cmix_ref, u_ref, tail_ref):
    i = pl.program_id(0)
    j = pl.program_id(1)
    strips = [slice(r * EVEN_IN_STRIP, (r + 1) * EVEN_IN_STRIP) for r in range(EVEN_IN_TM // EVEN_IN_STRIP)]

    halves = [slice(0, EVEN_IN_TN // 2), slice(EVEN_IN_TN // 2, EVEN_IN_TN)]

    @pl.when(j == 0)
    def _():
        for rs in strips:
            u = _rms(x_ref[rs, :], g_ref[...]).astype(BF16)
            u_ref[rs, :] = u
            cos = cos_ref[rs, :]
            sin = sin_ref[rs, :]
            for ns in halves:
                acc = jnp.dot(u, w_ref[:, ns], preferred_element_type=F32)
                for c in range((ns.stop - ns.start) // LANES):
                    cs = slice(c * LANES, (c + 1) * LANES)
                    y = _rope_tile(acc[:, cs], cos, sin)
                    if ns.start >= _K0:
                        kt_ref[cs, rs] = y.T.astype(BF16)
                    else:
                        q_ref[rs, cs] = y.astype(BF16)

    @pl.when(j == 1)
    def _():
        vg_ref[...] = jnp.dot(u_ref[...], w_ref[...], preferred_element_type=F32).astype(BF16)

    @pl.when(j == 2)
    def _():
        for rs in strips:
            for ns in halves:
                acc = jnp.dot(u_ref[rs, :], w_ref[:, ns], preferred_element_type=F32)
                vg_ref[rs, ns] = _silu(acc).astype(BF16)

    @pl.when(j >= 3)
    def _():
        slot = j - 3
        sequence_start = (i % (SEQ // EVEN_IN_TM)) == 0
        lane_tiles = [slice(ct * LANES, (ct + 1) * LANES) for ct in range(CONV_BLK // LANES)]
        halos = [jnp.where(sequence_start, 0.0, tail_ref[slot, :, cs]) for cs in lane_tiles]
        for rs in strips:
            for ct, cs in enumerate(lane_tiles):
                acc = jnp.dot(u_ref[rs, :], w_ref[:, ct * 4 * LANES:(ct + 1) * 4 * LANES],
                              preferred_element_type=F32)
                for r in range(EVEN_IN_STRIP // CONV_STRIP):
                    rr = slice(r * CONV_STRIP, (r + 1) * CONV_STRIP)
                    part = lambda k: acc[rr, k * LANES:(k + 1) * LANES]
                    prod = part(1) * part(2)
                    ext = jnp.concatenate([halos[ct], prod], axis=0)
                    conv = cb_ref[:, cs] + prod * cw_ref[CONV_K - 1:CONV_K, cs]
                    for tap in range(CONV_K - 1):
                        off = SUBLANES - (CONV_K - 1) + tap
                        conv = conv + ext[off:off + CONV_STRIP, :] * cw_ref[tap:tap + 1, cs]
                    halos[ct] = prod[CONV_STRIP - SUBLANES:, :]
                    out_rows = slice(rs.start + rr.start, rs.start + rr.stop)
                    cmix_ref[out_rows, cs] = (part(0) * conv * _silu(part(3))).astype(BF16)
        for ct, cs in enumerate(lane_tiles):
            tail_ref[slot, :, cs] = halos[ct]


def _even_in(h, pre_norm, w, cos_t, sin_t, conv_w, conv_b, layer):
    tm, tn = EVEN_IN_TM, EVEN_IN_TN
    e = layer // 2
    qkvg_blocks = EVEN_QKVG // tn
    vg_blocks = qkvg_blocks - 1
    conv_map = lambda i, j: (e, 0, jnp.maximum(j - qkvg_blocks, 0))
    return pl.pallas_call(
        _even_in_kernel,
        grid=(ROWS // tm, EVEN_IN // tn),
        in_specs=[
            pl.BlockSpec((tm, D_MODEL), lambda i, j: (i, 0)),
            pl.BlockSpec((None, 1, D_MODEL), lambda i, j: (layer, 0, 0)),
            pl.BlockSpec((None, D_MODEL, tn), lambda i, j: (e, 0, j)),
            pl.BlockSpec((tm, LANES), lambda i, j: (i, 0)),
            pl.BlockSpec((tm, LANES), lambda i, j: (i, 0)),
            pl.BlockSpec((None, CONV_K, CONV_BLK), conv_map),
            pl.BlockSpec((None, 1, CONV_BLK), conv_map),
        ],
        out_specs=[
            pl.BlockSpec((tm, RET_QK), lambda i, j: (i, 0)),
            pl.BlockSpec((RET_QK, tm), lambda i, j: (0, i)),
            pl.BlockSpec((tm, tn), lambda i, j: (i, jnp.clip(j - 1, 0, vg_blocks - 1))),
            pl.BlockSpec((tm, CONV_BLK), lambda i, j: (i, jnp.maximum(j - qkvg_blocks, 0))),
        ],
        out_shape=[
            jax.ShapeDtypeStruct((ROWS, RET_QK), BF16),
            jax.ShapeDtypeStruct((RET_QK, ROWS), BF16),
            jax.ShapeDtypeStruct((ROWS, vg_blocks * tn), BF16),
            jax.ShapeDtypeStruct((ROWS, CONV_W), BF16),
        ],
        scratch_shapes=[
            pltpu.VMEM((tm, D_MODEL), BF16),
            pltpu.VMEM((CONV_BLOCKS, SUBLANES, CONV_BLK), F32),
        ],
        compiler_params=pltpu.CompilerParams(
            dimension_semantics=("arbitrary", "arbitrary"), vmem_limit_bytes=VMEM_LIMIT),
        name="even_in",
    )(h, pre_norm, w, cos_t, sin_t, conv_w, conv_b)


EVEN_MIX_R = 512
EVEN_MIX_STEPS = ROWS // EVEN_MIX_R
PROJ_PIECE = 256


def _even_mix_block(q_ref, kt_ref, vg_ref, cmix_ref, h_ref, wout_ref, gn_ref, pn_ref,
                    intra_ref, xi_ref, zeta_ref, gam_ref, bm_ref,
                    o_ref, state_ref, mix_prev_ref, mix_ref):
    C = RET_CHUNK
    chunks = EVEN_MIX_R // C

    row_k = lax.broadcasted_iota(jnp.int32, (LANES, C), 0)
    first_head = (row_k % HALF_TILE) < ROPE_HALF
    lane_v = lax.broadcasted_iota(jnp.int32, (C, 2 * RET_DV), 1)
    zero_k = jnp.zeros((LANES, C), BF16)
    zero_v = jnp.zeros((C, 2 * RET_DV), BF16)

    y_prev = []

    def project_piece():
        ns = slice(len(y_prev) * PROJ_PIECE, (len(y_prev) + 1) * PROJ_PIECE)
        y_prev.append(jnp.dot(mix_prev_ref[...], wout_ref[:RET_V, ns], preferred_element_type=F32)
                      + jnp.dot(cmix_ref[...], wout_ref[RET_V:, ns], preferred_element_type=F32))

    stages_per_piece = chunks * RET_PAIRS * PROJ_PIECE // D_MODEL
    for c, j in [(c, p) for c in range(chunks) for p in range(RET_PAIRS)]:
        rows = slice(c * C, (c + 1) * C)
        if (c * RET_PAIRS + j) % stages_per_piece == 0:
            project_piece()
        qp = q_ref[rows, j * LANES:(j + 1) * LANES]
        kt = kt_ref[j * LANES:(j + 1) * LANES, rows]
        vp = vg_ref[rows, j * 2 * RET_DV:(j + 1) * 2 * RET_DV]
        kt_heads = jnp.concatenate(
            [jnp.where(first_head, kt, zero_k), jnp.where(first_head, zero_k, kt)], axis=1)
        sc = jnp.dot(qp, kt_heads, preferred_element_type=F32)
        pm = (sc * intra_ref[j]).astype(BF16)
        vblk = jnp.concatenate(
            [jnp.where(lane_v < RET_DV, vp, zero_v), jnp.where(lane_v >= RET_DV, vp, zero_v)], axis=0)
        st = state_ref[j]
        r = (jnp.dot(pm, vblk, preferred_element_type=F32)
             + jnp.dot(qp, st.astype(BF16), preferred_element_type=F32) * xi_ref[j])
        kz = (kt.astype(F32) * zeta_ref[j]).astype(BF16)
        kv = jnp.dot(kz, vp, preferred_element_type=F32)
        state_ref[j] = st * gam_ref[j] + kv * bm_ref[...]
        for hh in range(2):
            hd = 2 * j + hh
            cols = slice(hd * RET_DV, (hd + 1) * RET_DV)
            y = _rms(r[:, hh * RET_DV:(hh + 1) * RET_DV], gn_ref[:, cols])
            gate = vg_ref[rows, RET_V + hd * RET_DV:RET_V + (hd + 1) * RET_DV].astype(F32)
            mix_ref[rows, cols] = (y * gate).astype(BF16)

    o_ref[...] = h_ref[...] + _rms(jnp.concatenate(y_prev, axis=1), pn_ref[...])


def _even_mix_kernel(q_ref, kt_ref, vg_ref, cmix_ref, h_ref, wout_ref, gn_ref, pn_ref,
                     intra_ref, xi_ref, zeta_ref, gam_ref, bm_ref,
                     o_ref, state_ref, mix_a_ref, mix_b_ref):
    t = pl.program_id(0)
    args = (q_ref, kt_ref, vg_ref, cmix_ref, h_ref, wout_ref, gn_ref, pn_ref,
            intra_ref, xi_ref, zeta_ref, gam_ref, bm_ref, o_ref, state_ref)

    @pl.when(t % (SEQ // EVEN_MIX_R) == 0)
    def _():
        state_ref[...] = jnp.zeros_like(state_ref)

    @pl.when(t == 0)
    def _():
        mix_b_ref[...] = jnp.zeros_like(mix_b_ref)

    last = EVEN_MIX_STEPS
    assert last % 2 == 0

    @pl.when((t % 2 == 0) & (t < last))
    def _():
        _even_mix_block(*args, mix_b_ref, mix_a_ref)

    @pl.when(t % 2 == 1)
    def _():
        _even_mix_block(*args, mix_a_ref, mix_b_ref)

    @pl.when(t == last)
    def _():
        y = (jnp.dot(mix_b_ref[...], wout_ref[:RET_V, :], preferred_element_type=F32)
             + jnp.dot(cmix_ref[...], wout_ref[RET_V:, :], preferred_element_type=F32))
        o_ref[...] = h_ref[...] + _rms(y, pn_ref[...])


def _retention_tables():
    h, c = RET_HEADS, RET_CHUNK
    log_gamma = np.log1p(-np.exp2(-5.0 - np.arange(h, dtype=np.float64)))
    i = np.arange(c, dtype=np.float64)
    rel = i[:, None] - i[None, :]
    intra = np.where(rel >= 0, np.exp(log_gamma[:, None, None] * np.maximum(rel, 0.0)), 0.0)
    xi = np.exp(log_gamma[:, None] * (i + 1.0))
    zeta = np.exp(log_gamma[:, None] * (c - 1.0 - i))
    gamma_c = np.exp(log_gamma * c)
    pair = np.arange(RET_PAIRS)
    intra_p = intra.reshape(RET_PAIRS, 2, c, c).transpose(0, 2, 1, 3).reshape(RET_PAIRS, c, 2 * c)
    xi_p = np.repeat(xi.reshape(RET_PAIRS, 2, c).transpose(0, 2, 1), RET_DV, axis=2)
    second = ((np.arange(2 * RET_DK) % HALF_TILE) >= ROPE_HALF).astype(np.int64)
    zeta_p = zeta.reshape(RET_PAIRS, 2, c)[pair[:, None, None], second[None, :, None], np.arange(c)[None, None, :]]
    gam_rows = gamma_c.reshape(RET_PAIRS, 2)[:, second]
    gam_p = np.broadcast_to(gam_rows[:, :, None], (RET_PAIRS, 2 * RET_DK, 2 * RET_DV))
    col_head = np.arange(2 * RET_DV) // RET_DV
    bm = (second[:, None] == col_head[None, :])
    as_f32 = lambda a: jnp.asarray(np.ascontiguousarray(a, dtype=np.float32))
    return as_f32(intra_p), as_f32(xi_p), as_f32(zeta_p), as_f32(gam_p), as_f32(bm)


def _even_mix(q, kt, vg, cmix, h, w_out, gn, post_norm, tables, layer):
    R = EVEN_MIX_R
    nt = EVEN_MIX_STEPS
    e = layer // 2
    intra_p, xi_p, zeta_p, gam_p, bm = tables
    mix_map = lambda t: (jnp.minimum(t, nt - 1), 0)
    row_map = lambda t: (jnp.maximum(t - 1, 0), 0)
    const2 = lambda t: (0, 0)
    const3 = lambda t: (0, 0, 0)
    return pl.pallas_call(
        _even_mix_kernel,
        grid=(nt + 1,),
        in_specs=[
            pl.BlockSpec((R, RET_QK), mix_map),
            pl.BlockSpec((RET_QK, R), lambda t: (0, jnp.minimum(t, nt - 1))),
            pl.BlockSpec((R, 2 * RET_V), mix_map),
            pl.BlockSpec((R, CONV_W), row_map),
            pl.BlockSpec((R, D_MODEL), row_map),
            pl.BlockSpec((None, EVEN_MIX, D_MODEL), lambda t: (e, 0, 0)),
            pl.BlockSpec((None, 1, RET_V), lambda t: (e, 0, 0)),
            pl.BlockSpec((None, 1, D_MODEL), lambda t: (layer, 0, 0)),
            pl.BlockSpec((RET_PAIRS, RET_CHUNK, 2 * RET_CHUNK), const3),
            pl.BlockSpec((RET_PAIRS, RET_CHUNK, 2 * RET_DV), const3),
            pl.BlockSpec((RET_PAIRS, RET_CHUNK, 2 * RET_DK), const3),
            pl.BlockSpec((RET_PAIRS, 2 * RET_DK, 2 * RET_DV), const3),
            pl.BlockSpec((2 * RET_DK, 2 * RET_DV), const2),
        ],
        out_specs=pl.BlockSpec((R, D_MODEL), row_map),
        out_shape=jax.ShapeDtypeStruct((ROWS, D_MODEL), F32),
        scratch_shapes=[
            pltpu.VMEM((RET_PAIRS, 2 * RET_DK, 2 * RET_DV), F32),
            pltpu.VMEM((R, RET_V), BF16),
            pltpu.VMEM((R, RET_V), BF16),
        ],
        compiler_params=pltpu.CompilerParams(
            dimension_semantics=("arbitrary",), vmem_limit_bytes=VMEM_LIMIT),
        name="even_mix",
    )(q, kt, vg, cmix, h, w_out, gn, post_norm, intra_p, xi_p, zeta_p, gam_p, bm)


ODD_R = 1024
ATTN_T = 512


def _odd_in_kernel(x_ref, g_ref, wa_ref, wg_ref, qn_ref, wq_ref, kvn_ref, wkv_ref, cos_ref, sin_ref,
                   q_ref, k_ref, vt_ref, sg_ref):
    u = _rms(x_ref[...], g_ref[...]).astype(BF16)
    a = jnp.dot(u, wa_ref[...], preferred_element_type=F32)
    cq = a[:, :Q_LORA]
    ckv = a[:, Q_LORA:Q_LORA + KV_LORA]
    kr = a[:, Q_LORA + KV_LORA:]
    q = jnp.dot(_rms(cq, qn_ref[...]).astype(BF16), wq_ref[...], preferred_element_type=F32) * (MLA_SCALE * LOG2E)
    kv = jnp.dot(_rms(ckv, kvn_ref[...]).astype(BF16), wkv_ref[...], preferred_element_type=F32)
    cos = cos_ref[...]
    sin = sin_ref[...]
    kr_rot = _rope_tile(kr, cos, sin)
    kr_heads = (kr_rot.astype(BF16), pltpu.roll(kr_rot, ROPE_HALF, 1).astype(BF16))
    for hd in range(MLA_HEADS):
        lo, hi = hd * LANES, (hd + 1) * LANES
        q_ref[0, hd, 0:LANES, :] = q[:, lo:hi].T.astype(BF16)
        if hd % 2 == 0:
            pair = MLA_V + (hd // 2) * LANES
            q_rot_t = _rope_tile(q[:, pair:pair + LANES], cos, sin).T.astype(BF16)
        q_ref[0, hd, LANES:2 * LANES, :] = q_rot_t
        k_ref[0, hd, :, 0:LANES] = kv[:, lo:hi].astype(BF16)
        k_ref[0, hd, :, LANES:2 * LANES] = kr_heads[hd % 2]
        for s in range(ODD_R // ATTN_T):
            vt_ref[0, hd, s] = kv[s * ATTN_T:(s + 1) * ATTN_T, MLA_V + lo:MLA_V + hi].T.astype(BF16)
    gate = jnp.dot(u, wg_ref[...], preferred_element_type=F32)
    sg_ref[...] = _silu(gate).astype(BF16)


def _odd_in(h, pre_norm, wa, wg, qn, wq, kvn, wkv, cos_t, sin_t, layer):
    R = ODD_R
    nb = SEQ // R
    o = layer // 2
    row_map = lambda b, i: (b * nb + i, 0)
    odd_map = lambda b, i: (o, 0, 0)
    return pl.pallas_call(
        _odd_in_kernel,
        grid=(BATCH, nb),
        in_specs=[
            pl.BlockSpec((R, D_MODEL), row_map),
            pl.BlockSpec((None, 1, D_MODEL), lambda b, i: (layer, 0, 0)),
            pl.BlockSpec((None, D_MODEL, ODD_A), odd_map),
            pl.BlockSpec((None, D_MODEL, MLA_V), odd_map),
            pl.BlockSpec((None, 1, Q_LORA), odd_map),
            pl.BlockSpec((None, Q_LORA, ODD_WQ), odd_map),
            pl.BlockSpec((None, 1, KV_LORA), odd_map),
            pl.BlockSpec((None, KV_LORA, 2 * MLA_V), odd_map),
            pl.BlockSpec((R, LANES), row_map),
            pl.BlockSpec((R, LANES), row_map),
        ],
        out_specs=[
            pl.BlockSpec((1, MLA_HEADS, MLA_QK_PAD, R), lambda b, i: (b, 0, 0, i)),
            pl.BlockSpec((1, MLA_HEADS, R, MLA_QK_PAD), lambda b, i: (b, 0, i, 0)),
            pl.BlockSpec((1, MLA_HEADS, R // ATTN_T, MLA_DV, ATTN_T), lambda b, i: (b, 0, i, 0, 0)),
            pl.BlockSpec((R, MLA_V), row_map),
        ],
        out_shape=[
            jax.ShapeDtypeStruct((BATCH, MLA_HEADS, MLA_QK_PAD, SEQ), BF16),
            jax.ShapeDtypeStruct((BATCH, MLA_HEADS, SEQ, MLA_QK_PAD), BF16),
            jax.ShapeDtypeStruct((BATCH, MLA_HEADS, SEQ // ATTN_T, MLA_DV, ATTN_T), BF16),
            jax.ShapeDtypeStruct((ROWS, MLA_V), BF16),
        ],
        compiler_params=pltpu.CompilerParams(
            dimension_semantics=("arbitrary", "arbitrary"), vmem_limit_bytes=VMEM_LIMIT),
        name="odd_in",
    )(h, pre_norm, wa, wg, qn, wq, kvn, wkv, cos_t, sin_t)


ATTN_HG = 4


def _attn_kernel(q_ref, qn_ref, k_ref, vt_ref, sg_ref, o_ref, m_ref, l_ref, acc_ref, st_ref):
    T = ATTN_T
    qi = pl.program_id(2)
    m_ref[...] = jnp.full(m_ref.shape, -jnp.inf, F32)
    l_ref[...] = jnp.zeros(l_ref.shape, F32)
    acc_ref[...] = jnp.zeros(acc_ref.shape, F32)

    def scores(hh, j, queries=q_ref):
        k = k_ref[0, hh, pl.ds(pl.multiple_of(j * T, T), T), :]
        return jnp.dot(k, queries[0, hh], preferred_element_type=F32)

    def consume(hh, j, st):
        m_old = m_ref[hh]
        m_new = jnp.maximum(m_old, jnp.max(st, axis=0, keepdims=True))
        alpha = jnp.exp2(m_old - m_new)
        p = jnp.exp2(st - m_new)
        l_ref[hh] = alpha * l_ref[hh] + jnp.sum(p, axis=0, keepdims=True)
        acc_ref[hh] = alpha * acc_ref[hh] + jnp.dot(
            vt_ref[0, hh, j], p.astype(BF16), preferred_element_type=F32)
        m_ref[hh] = m_new

    @pl.when(qi == 0)
    def _():
        for hh in range(ATTN_HG):
            st_ref[hh] = scores(hh, 0)

    def body(j, carry):
        for hh in range(ATTN_HG):
            st = st_ref[hh]
            st_ref[hh] = scores(hh, j + 1)
            consume(hh, j, st)
        return carry

    lax.fori_loop(0, qi, body, 0)

    H = T // 2
    causal = lax.broadcasted_iota(jnp.int32, (H, H), 0) <= lax.broadcasted_iota(jnp.int32, (H, H), 1)

    def consume_diagonal(hh, st):
        s_ee = jnp.where(causal, st[:H, :H], -jnp.inf)
        s_el = st[:H, H:]
        s_ll = jnp.where(causal, st[H:, H:], -jnp.inf)
        m_old = m_ref[hh]
        col_max = jnp.concatenate(
            [jnp.max(s_ee, axis=0, keepdims=True),
             jnp.maximum(jnp.max(s_el, axis=0, keepdims=True), jnp.max(s_ll, axis=0, keepdims=True))], axis=1)
        m_new = jnp.maximum(m_old, col_max)
        alpha = jnp.exp2(m_old - m_new)
        p_ee = jnp.exp2(s_ee - m_new[:, :H])
        p_el = jnp.exp2(s_el - m_new[:, H:])
        p_ll = jnp.exp2(s_ll - m_new[:, H:])
        col_sum = jnp.concatenate(
            [jnp.sum(p_ee, axis=0, keepdims=True),
             jnp.sum(p_el, axis=0, keepdims=True) + jnp.sum(p_ll, axis=0, keepdims=True)], axis=1)
        l_ref[hh] = alpha * l_ref[hh] + col_sum
        vt = vt_ref[0, hh, qi]
        pv = jnp.concatenate(
            [jnp.dot(vt[:, :H], p_ee.astype(BF16), preferred_element_type=F32),
             jnp.dot(vt, jnp.concatenate([p_el, p_ll], axis=0).astype(BF16), preferred_element_type=F32)], axis=1)
        acc_ref[hh] = alpha * acc_ref[hh] + pv
        m_ref[hh] = m_new

    for hh in range(ATTN_HG):
        st = st_ref[hh]
        st_ref[hh] = scores(hh, 0, qn_ref)
        consume_diagonal(hh, st)
    for hh in range(ATTN_HG):
        cols = slice(hh * MLA_DV, (hh + 1) * MLA_DV)
        o = (acc_ref[hh] * (1.0 / l_ref[hh])).T
        o_ref[0, :, cols] = (o * sg_ref[0, :, cols].astype(F32)).astype(BF16)


def _attn(q_cat, k_cat, vt, sg):
    T = ATTN_T
    HG = ATTN_HG
    nq = SEQ // T
    return pl.pallas_call(
        _attn_kernel,
        grid=(BATCH, MLA_HEADS // HG, nq),
        in_specs=[
            pl.BlockSpec((1, HG, MLA_QK_PAD, T), lambda b, h, i: (b, h, 0, i)),
            pl.BlockSpec((1, HG, MLA_QK_PAD, T), lambda b, h, i: (b, h, 0, jnp.minimum(i + 1, nq - 1))),
            pl.BlockSpec((1, HG, SEQ, MLA_QK_PAD), lambda b, h, i: (b, h, 0, 0)),
            pl.BlockSpec((1, HG, nq, MLA_DV, T), lambda b, h, i: (b, h, 0, 0, 0)),
            pl.BlockSpec((1, T, HG * MLA_DV), lambda b, h, i: (b, i, h)),
        ],
        out_specs=pl.BlockSpec((1, T, HG * MLA_DV), lambda b, h, i: (b, i, h)),
        out_shape=jax.ShapeDtypeStruct((BATCH, SEQ, MLA_V), BF16),
        scratch_shapes=[
            pltpu.VMEM((HG, 1, T), F32),
            pltpu.VMEM((HG, 1, T), F32),
            pltpu.VMEM((HG, MLA_DV, T), F32),
            pltpu.VMEM((HG, T, T), F32),
        ],
        compiler_params=pltpu.CompilerParams(
            dimension_semantics=("arbitrary", "arbitrary", "arbitrary"), vmem_limit_bytes=VMEM_LIMIT),
        name="odd_attn",
    )(q_cat, q_cat, k_cat, vt, sg.reshape(BATCH, SEQ, MLA_V))


ODD_OUT_R = 2048


def _odd_out_kernel(og_ref, h_ref, w_ref, pn_ref, out_ref):
    y = jnp.dot(og_ref[...], w_ref[...], preferred_element_type=F32)
    out_ref[...] = h_ref[...] + _rms(y, pn_ref[...])


def _odd_out(og, h, w, post_norm, layer):
    R = ODD_OUT_R
    row_map = lambda i: (i, 0)
    return pl.pallas_call(
        _odd_out_kernel,
        grid=(ROWS // R,),
        in_specs=[
            pl.BlockSpec((R, MLA_V), row_map),
            pl.BlockSpec((R, D_MODEL), row_map),
            pl.BlockSpec((None, MLA_V, D_MODEL), lambda i: (layer // 2, 0, 0)),
            pl.BlockSpec((None, 1, D_MODEL), lambda i: (layer, 0, 0)),
        ],
        out_specs=pl.BlockSpec((R, D_MODEL), row_map),
        out_shape=jax.ShapeDtypeStruct((ROWS, D_MODEL), F32),
        compiler_params=pltpu.CompilerParams(
            dimension_semantics=("arbitrary",), vmem_limit_bytes=VMEM_LIMIT),
        name="odd_out",
    )(og, h, w, post_norm)


PREP_ROW_BLOCKS = 4


def _lane_iota(rows):
    return lax.broadcasted_iota(jnp.int32, (rows, LANES), 1)


def _copy_rows(vec_refs, out_refs):
    @pl.when((pl.program_id(0) == 0) & (pl.program_id(1) == 0))
    def _():
        for src, dst in zip(vec_refs, out_refs):
            for layer in range(src.shape[0]):
                dst[layer] = src[layer:layer + 1, :]


def _prep_even_kernel(win_ref, wout_ref, *rest):
    vec_refs, (win_o_ref, wout_o_ref), vec_out_refs = rest[:4], rest[4:6], rest[6:]
    _copy_rows(vec_refs, vec_out_refs)
    rows = win_ref.shape[0]
    lane = _lane_iota(rows)
    keep = (lane < ROPE_HALF) | (lane >= LANES - ROPE_HALF)
    for t in range(_V0 // LANES):
        cs = slice(t * LANES, (t + 1) * LANES)
        x = win_ref[:, cs]
        moved = jnp.where(lane < HALF_TILE, pltpu.roll(x, LANES - ROPE_HALF, 1), pltpu.roll(x, ROPE_HALF, 1))
        y = jnp.where(keep, x, moved)
        if t * LANES >= _K0:
            y = y * (RET_DK ** -0.5)
        win_o_ref[:, cs] = y.astype(BF16)
    win_o_ref[:, _V0:_CB0] = win_ref[:, _V0:_CB0].astype(BF16)
    for t in range(CONV_W // LANES):
        for k in range(4):
            src = _CB0 + k * CONV_W + t * LANES
            dst = _CB0 + (4 * t + k) * LANES
            win_o_ref[:, dst:dst + LANES] = win_ref[:, src:src + LANES].astype(BF16)
    wout_o_ref[...] = wout_ref[...].astype(BF16)


def _vec_specs(vecs):
    ins = [pl.BlockSpec(v.shape, lambda *_: (0, 0)) for v in vecs]
    outs = [pl.BlockSpec((v.shape[0], 1, v.shape[1]), lambda *_: (0, 0, 0)) for v in vecs]
    shapes = [jax.ShapeDtypeStruct((v.shape[0], 1, v.shape[1]), v.dtype) for v in vecs]
    return ins, outs, shapes


def _prep_even(even_w_in, even_w_out, vecs):
    n = even_w_in.shape[0]
    rb = PREP_ROW_BLOCKS
    spec = lambda rows, cols: pl.BlockSpec((None, rows // rb, cols), lambda e, r: (e, r, 0))
    vec_in, vec_out, vec_shapes = _vec_specs(vecs)
    outs = pl.pallas_call(
        _prep_even_kernel,
        grid=(n, rb),
        in_specs=[spec(D_MODEL, EVEN_IN), spec(EVEN_MIX, D_MODEL)] + vec_in,
        out_specs=[spec(D_MODEL, EVEN_IN), spec(EVEN_MIX, D_MODEL)] + vec_out,
        out_shape=[jax.ShapeDtypeStruct((n, D_MODEL, EVEN_IN), BF16),
                   jax.ShapeDtypeStruct((n, EVEN_MIX, D_MODEL), BF16)] + vec_shapes,
        compiler_params=pltpu.CompilerParams(
            dimension_semantics=("arbitrary", "arbitrary"), vmem_limit_bytes=VMEM_LIMIT),
        name="prep_even",
    )(even_w_in, even_w_out, *vecs)
    return outs[0], outs[1], outs[2:]


def _prep_odd_kernel(wint_ref, wqb_ref, wkvb_ref, wout_ref, qn_ref, kvn_ref,
                     wa_ref, wg_ref, wq_ref, wkv_ref, wo_ref, qn_o_ref, kvn_o_ref):
    _copy_rows((qn_ref, kvn_ref), (qn_o_ref, kvn_o_ref))
    lat = Q_LORA + KV_LORA
    cols = wint_ref.shape[1]
    wa_ref[:, :lat] = wint_ref[:lat, :].T.astype(BF16)
    zeros = jnp.zeros((ROPE_HALF, cols), F32)
    rope_rows = jnp.concatenate(
        [wint_ref[lat:lat + ROPE_HALF, :], zeros, wint_ref[lat + ROPE_HALF:lat + MLA_ROPE, :], zeros], axis=0)
    wa_ref[:, lat:] = rope_rows.T.astype(BF16)
    wg_ref[...] = wint_ref[lat + MLA_ROPE:, :].T.astype(BF16)
    lane = _lane_iota(wqb_ref.shape[0])
    for hp in range(MLA_HEADS // 2):
        t0, t1, t2 = (wqb_ref[:, (3 * hp + k) * LANES:(3 * hp + k + 1) * LANES] for k in range(3))
        ha, hb = 2 * hp, 2 * hp + 1
        wq_ref[:, ha * LANES:(ha + 1) * LANES] = t0.astype(BF16)
        wq_ref[:, hb * LANES:(hb + 1) * LANES] = jnp.where(
            lane < HALF_TILE, pltpu.roll(t1, HALF_TILE, 1), pltpu.roll(t2, HALF_TILE, 1)).astype(BF16)
        pair = jnp.where(lane < ROPE_HALF, t1,
                         jnp.where(lane < HALF_TILE, pltpu.roll(t2, LANES - ROPE_HALF, 1),
                                   jnp.where(lane < HALF_TILE + ROPE_HALF, pltpu.roll(t1, ROPE_HALF, 1), t2)))
        wq_ref[:, MLA_V + hp * LANES:MLA_V + (hp + 1) * LANES] = pair.astype(BF16)
    for hd in range(MLA_HEADS):
        wkv_ref[:, hd * LANES:(hd + 1) * LANES] = wkvb_ref[:, 2 * hd * LANES:(2 * hd + 1) * LANES].astype(BF16)
        wkv_ref[:, MLA_V + hd * LANES:MLA_V + (hd + 1) * LANES] = (
            wkvb_ref[:, (2 * hd + 1) * LANES:(2 * hd + 2) * LANES].astype(BF16))
    wo_ref[...] = wout_ref[...].astype(BF16)


def _prep_odd(odd_w_in, w_qb, w_kvb, odd_w_out, vecs):
    n = odd_w_in.shape[0]
    rb = PREP_ROW_BLOCKS
    spec = lambda rows, cols: pl.BlockSpec((None, rows // rb, cols), lambda o, r: (o, r, 0))
    shape = lambda rows, cols: jax.ShapeDtypeStruct((n, rows, cols), BF16)
    vec_in, vec_out, vec_shapes = _vec_specs(vecs)
    odd_w_in_t = jnp.swapaxes(odd_w_in, 1, 2)
    return pl.pallas_call(
        _prep_odd_kernel,
        grid=(n, rb),
        in_specs=[pl.BlockSpec((None, ODD_IN, D_MODEL // rb), lambda o, r: (o, 0, r)),
                  spec(Q_LORA, MLA_HEADS * MLA_QK),
                  spec(KV_LORA, 2 * MLA_V), spec(MLA_V, D_MODEL)] + vec_in,
        out_specs=[spec(D_MODEL, ODD_A), spec(D_MODEL, MLA_V), spec(Q_LORA, ODD_WQ),
                   spec(KV_LORA, 2 * MLA_V), spec(MLA_V, D_MODEL)] + vec_out,
        out_shape=[shape(D_MODEL, ODD_A), shape(D_MODEL, MLA_V), shape(Q_LORA, ODD_WQ),
                   shape(KV_LORA, 2 * MLA_V), shape(MLA_V, D_MODEL)] + vec_shapes,
        compiler_params=pltpu.CompilerParams(
            dimension_semantics=("arbitrary", "arbitrary"), vmem_limit_bytes=VMEM_LIMIT),
        name="prep_odd",
    )(odd_w_in_t, w_qb, w_kvb, odd_w_out, *vecs)


def kernel(x, positions, pre_norm, post_norm, even_w_in, even_conv_w, even_conv_b, ret_gn, even_w_out,
           odd_w_in, q_a_norm, w_qb, kv_a_norm, w_kvb, odd_w_out):
    cos_t, sin_t = _rope_tables(positions)
    tables = _retention_tables()
    ew_in, ew_out, (pre_norm, post_norm, ret_gn, even_conv_b) = _prep_even(
        even_w_in, even_w_out, (pre_norm, post_norm, ret_gn, even_conv_b))
    wa, wg, wq, wkv, wo, q_a_norm, kv_a_norm = _prep_odd(odd_w_in, w_qb, w_kvb, odd_w_out, (q_a_norm, kv_a_norm))
    h = x.reshape(ROWS, D_MODEL)
    for layer in range(DEPTH):
        if layer % 2 == 0:
            q, kt, vg, cmix = _even_in(h, pre_norm, ew_in, cos_t, sin_t, even_conv_w, even_conv_b, layer)
            h = _even_mix(q, kt, vg, cmix, h, ew_out, ret_gn, post_norm, tables, layer)
        else:
            q_cat, k_cat, vt, sg = _odd_in(h, pre_norm, wa, wg, q_a_norm, wq, kv_a_norm, wkv, cos_t, sin_t, layer)
            og = _attn(q_cat, k_cat, vt, sg)
            h = _odd_out(og.reshape(ROWS, MLA_V), h, wo, post_norm, layer)
    return h.reshape(BATCH, SEQ, D_MODEL)
```

```python
import math

import jax
import jax.numpy as jnp
import numpy as np
from jax import lax
from jax.experimental import pallas as pl
from jax.experimental.pallas import tpu as pltpu

D_MODEL = 1024
BATCH = 4
SEQ = 4096
DEPTH = 4
ROWS = BATCH * SEQ

RET_HEADS = 8
RET_DK = 64
RET_DV = 128
RET_CHUNK = 128
RET_PAIRS = RET_HEADS // 2
CONV_W = 1024
CONV_K = 3
MLA_HEADS = 8
MLA_NOPE = 128
MLA_ROPE = 64
MLA_DV = 128
Q_LORA = 384
KV_LORA = 256
ROPE_BASE = 10000.0
EPS = 1e-6

RET_QK = RET_HEADS * RET_DK
RET_V = RET_HEADS * RET_DV
EVEN_IN = 2 * RET_QK + 2 * RET_V + 4 * CONV_W
EVEN_MIX = RET_V + CONV_W
MLA_QK = MLA_NOPE + MLA_ROPE
MLA_V = MLA_HEADS * MLA_DV
MLA_SCALE = MLA_QK ** -0.5
LOG2E = math.log2(math.e)
MLA_QK_PAD = 256
ODD_IN = Q_LORA + KV_LORA + MLA_ROPE + MLA_V
ODD_A = Q_LORA + KV_LORA + 128
ODD_WQ = MLA_V + (MLA_HEADS // 2) * 128

_K0, _V0, _CB0 = 512, 1024, 3072

LANES = 128
SUBLANES = 8
ROPE_HALF = 32
HALF_TILE = LANES // 2

VMEM_LIMIT = 56 * 1024 * 1024

BF16 = jnp.bfloat16
F32 = jnp.float32


def _silu(x):
    return (0.5 * x) * (1.0 + jnp.tanh(0.5 * x))


def _rms(x, g):
    ms = jnp.mean(x * x, axis=-1, keepdims=True)
    return x * lax.rsqrt(ms + EPS) * g


def _rope_tile(x, cos, sin_signed):
    return x * cos + pltpu.roll(x, HALF_TILE, 1) * sin_signed


ROPE_GROUPS = LANES // ROPE_HALF
ROPE_BLK = 1024


def _rope_table_kernel(pos_ref, inv_ref, cos_ref, sin_ref):
    ang = pos_ref[...].astype(F32) * inv_ref[...]
    c = jnp.cos(ang)
    s = jnp.sin(ang)
    lane = lax.broadcasted_iota(jnp.int32, (ROPE_BLK, LANES), 1)
    low = lane < ROPE_HALF
    for g in range(ROPE_GROUPS):
        shift = (LANES - g * ROPE_HALF) % LANES
        cg = jnp.where(low, c if shift == 0 else pltpu.roll(c, shift, 1), 0.0)
        sg = jnp.where(low, s if shift == 0 else pltpu.roll(s, shift, 1), 0.0)
        cg = cg + pltpu.roll(cg, ROPE_HALF, 1)
        sg = sg + pltpu.roll(sg, ROPE_HALF, 1)
        cos_ref[g] = cg + pltpu.roll(cg, HALF_TILE, 1)
        sin_ref[g] = pltpu.roll(sg, HALF_TILE, 1) - sg


def _rope_tables(positions):
    inv = ROPE_BASE ** (-jnp.arange(0, 2 * ROPE_HALF, 2, dtype=F32) / (2 * ROPE_HALF))
    dense_rows = ROWS // ROPE_GROUPS
    pos_dense = jnp.repeat(positions.reshape(ROPE_GROUPS, dense_rows).T, ROPE_HALF, axis=1)
    inv_dense = jnp.tile(inv, ROPE_GROUPS).reshape(1, LANES)
    out_spec = pl.BlockSpec((ROPE_GROUPS, ROPE_BLK, LANES), lambda i: (0, i, 0))
    out = jax.ShapeDtypeStruct((ROPE_GROUPS, dense_rows, LANES), F32)
    cos_t, sin_t = pl.pallas_call(
        _rope_table_kernel,
        grid=(dense_rows // ROPE_BLK,),
        in_specs=[pl.BlockSpec((ROPE_BLK, LANES), lambda i: (i, 0)),
                  pl.BlockSpec((1, LANES), lambda i: (0, 0))],
        out_specs=[out_spec, out_spec],
        out_shape=[out, out],
        name="rope_table",
    )(pos_dense, inv_dense)
    return cos_t.reshape(ROWS, LANES), sin_t.reshape(ROWS, LANES)


EVEN_IN_TM = 2048
EVEN_IN_TN = 1024
EVEN_QKVG = 3 * EVEN_IN_TN
CONV_BLK = EVEN_IN_TN // 4
CONV_BLOCKS = CONV_W // CONV_BLK
EVEN_IN_STRIP = 256
CONV_STRIP = 128


def _even_in_kernel(x_ref, g_ref, w_ref, cos_ref, sin_ref, cw_ref, cb_ref,
                    q_ref, kt_ref, vg_ref, cmix_ref, u_ref, tail_ref):
    i = pl.program_id(0)
    j = pl.program_id(1)
    strips = [slice(r * EVEN_IN_STRIP, (r + 1) * EVEN_IN_STRIP) for r in range(EVEN_IN_TM // EVEN_IN_STRIP)]

    halves = [slice(0, EVEN_IN_TN // 2), slice(EVEN_IN_TN // 2, EVEN_IN_TN)]

    @pl.when(j == 0)
    def _():
        for rs in strips:
            u = _rms(x_ref[rs, :], g_ref[...]).astype(BF16)
            u_ref[rs, :] = u
            cos = cos_ref[rs, :]
            sin = sin_ref[rs, :]
            for ns in halves:
                acc = jnp.dot(u, w_ref[:, ns], preferred_element_type=F32)
                for c in range((ns.stop - ns.start) // LANES):
                    cs = slice(c * LANES, (c + 1) * LANES)
                    y = _rope_tile(acc[:, cs], cos, sin)
                    if ns.start >= _K0:
                        kt_ref[cs, rs] = y.T.astype(BF16)
                    else:
                        q_ref[rs, cs] = y.astype(BF16)

    @pl.when(j == 1)
    def _():
        vg_ref[...] = jnp.dot(u_ref[...], w_ref[...], preferred_element_type=F32).astype(BF16)

    @pl.when(j == 2)
    def _():
        for rs in strips:
            for ns in halves:
                acc = jnp.dot(u_ref[rs, :], w_ref[:, ns], preferred_element_type=F32)
                vg_ref[rs, ns] = _silu(acc).astype(BF16)

    @pl.when(j >= 3)
    def _():
        slot = j - 3
        sequence_start = (i % (SEQ // EVEN_IN_TM)) == 0
        lane_tiles = [slice(ct * LANES, (ct + 1) * LANES) for ct in range(CONV_BLK // LANES)]
        halos = [jnp.where(sequence_start, 0.0, tail_ref[slot, :, cs]) for cs in lane_tiles]
        for rs in strips:
            for ct, cs in enumerate(lane_tiles):
                acc = jnp.dot(u_ref[rs, :], w_ref[:, ct * 4 * LANES:(ct + 1) * 4 * LANES],
                              preferred_element_type=F32)
                for r in range(EVEN_IN_STRIP // CONV_STRIP):
                    rr = slice(r * CONV_STRIP, (r + 1) * CONV_STRIP)
                    part = lambda k: acc[rr, k * LANES:(k + 1) * LANES]
                    prod = part(1) * part(2)
                    ext = jnp.concatenate([halos[ct], prod], axis=0)
                    conv = cb_ref[:, cs] + prod * cw_ref[CONV_K - 1:CONV_K, cs]
                    for tap in range(CONV_K - 1):
                        off = SUBLANES - (CONV_K - 1) + tap
                        conv = conv + ext[off:off + CONV_STRIP, :] * cw_ref[tap:tap + 1, cs]
                    halos[ct] = prod[CONV_STRIP - SUBLANES:, :]
                    out_rows = slice(rs.start + rr.start, rs.start + rr.stop)
                    cmix_ref[out_rows, cs] = (part(0) * conv * _silu(part(3))).astype(BF16)
        for ct, cs in enumerate(lane_tiles):
            tail_ref[slot, :, cs] = halos[ct]


def _even_in(h, pre_norm, w, cos_t, sin_t, conv_w, conv_b, layer):
    tm, tn = EVEN_IN_TM, EVEN_IN_TN
    e = layer // 2
    qkvg_blocks = EVEN_QKVG // tn
    vg_blocks = qkvg_blocks - 1
    conv_map = lambda i, j: (e, 0, jnp.maximum(j - qkvg_blocks, 0))
    return pl.pallas_call(
        _even_in_kernel,
        grid=(ROWS // tm, EVEN_IN // tn),
        in_specs=[
            pl.BlockSpec((tm, D_MODEL), lambda i, j: (i, 0)),
            pl.BlockSpec((None, 1, D_MODEL), lambda i, j: (layer, 0, 0)),
            pl.BlockSpec((None, D_MODEL, tn), lambda i, j: (e, 0, j)),
            pl.BlockSpec((tm, LANES), lambda i, j: (i, 0)),
            pl.BlockSpec((tm, LANES), lambda i, j: (i, 0)),
            pl.BlockSpec((None, CONV_K, CONV_BLK), conv_map),
            pl.BlockSpec((None, 1, CONV_BLK), conv_map),
        ],
        out_specs=[
            pl.BlockSpec((tm, RET_QK), lambda i, j: (i, 0)),
            pl.BlockSpec((RET_QK, tm), lambda i, j: (0, i)),
            pl.BlockSpec((tm, tn), lambda i, j: (i, jnp.clip(j - 1, 0, vg_blocks - 1))),
            pl.BlockSpec((tm, CONV_BLK), lambda i, j: (i, jnp.maximum(j - qkvg_blocks, 0))),
        ],
        out_shape=[
            jax.ShapeDtypeStruct((ROWS, RET_QK), BF16),
            jax.ShapeDtypeStruct((RET_QK, ROWS), BF16),
            jax.ShapeDtypeStruct((ROWS, vg_blocks * tn), BF16),
            jax.ShapeDtypeStruct((ROWS, CONV_W), BF16),
        ],
        scratch_shapes=[
            pltpu.VMEM((tm, D_MODEL), BF16),
            pltpu.VMEM((CONV_BLOCKS, SUBLANES, CONV_BLK), F32),
        ],
        compiler_params=pltpu.CompilerParams(
            dimension_semantics=("arbitrary", "arbitrary"), vmem_limit_bytes=VMEM_LIMIT),
        name="even_in",
    )(h, pre_norm, w, cos_t, sin_t, conv_w, conv_b)


EVEN_MIX_R = 512
EVEN_MIX_STEPS = ROWS // EVEN_MIX_R
PROJ_PIECE = 256


def _even_mix_block(q_ref, kt_ref, vg_ref, cmix_ref, h_ref, wout_ref, gn_ref, pn_ref,
                    intra_ref, xi_ref, zeta_ref, gam_ref, bm_ref,
                    o_ref, state_ref, mix_prev_ref, mix_ref):
    C = RET_CHUNK
    chunks = EVEN_MIX_R // C

    row_k = lax.broadcasted_iota(jnp.int32, (LANES, C), 0)
    first_head = (row_k % HALF_TILE) < ROPE_HALF
    lane_v = lax.broadcasted_iota(jnp.int32, (C, 2 * RET_DV), 1)
    zero_k = jnp.zeros((LANES, C), BF16)
    zero_v = jnp.zeros((C, 2 * RET_DV), BF16)

    y_prev = []

    def project_piece():
        ns = slice(len(y_prev) * PROJ_PIECE, (len(y_prev) + 1) * PROJ_PIECE)
        y_prev.append(jnp.dot(mix_prev_ref[...], wout_ref[:RET_V, ns], preferred_element_type=F32)
                      + jnp.dot(cmix_ref[...], wout_ref[RET_V:, ns], preferred_element_type=F32))

    stages_per_piece = chunks * RET_PAIRS * PROJ_PIECE // D_MODEL
    for c, j in [(c, p) for c in range(chunks) for p in range(RET_PAIRS)]:
        rows = slice(c * C, (c + 1) * C)
        if (c * RET_PAIRS + j) % stages_per_piece == 0:
            project_piece()
        qp = q_ref[rows, j * LANES:(j + 1) * LANES]
        kt = kt_ref[j * LANES:(j + 1) * LANES, rows]
        vp = vg_ref[rows, j * 2 * RET_DV:(j + 1) * 2 * RET_DV]
        kt_heads = jnp.concatenate(
            [jnp.where(first_head, kt, zero_k), jnp.where(first_head, zero_k, kt)], axis=1)
        sc = jnp.dot(qp, kt_heads, preferred_element_type=F32)
        pm = (sc * intra_ref[j]).astype(BF16)
        vblk = jnp.concatenate(
            [jnp.where(lane_v < RET_DV, vp, zero_v), jnp.where(lane_v >= RET_DV, vp, zero_v)], axis=0)
        st = state_ref[j]
        r = (jnp.dot(pm, vblk, preferred_element_type=F32)
             + jnp.dot(qp, st.astype(BF16), preferred_element_type=F32) * xi_ref[j])
        kz = (kt.astype(F32) * zeta_ref[j]).astype(BF16)
        kv = jnp.dot(kz, vp, preferred_element_type=F32)
        state_ref[j] = st * gam_ref[j] + kv * bm_ref[...]
        for hh in range(2):
            hd = 2 * j + hh
            cols = slice(hd * RET_DV, (hd + 1) * RET_DV)
            y = _rms(r[:, hh * RET_DV:(hh + 1) * RET_DV], gn_ref[:, cols])
            gate = vg_ref[rows, RET_V + hd * RET_DV:RET_V + (hd + 1) * RET_DV].astype(F32)
            mix_ref[rows, cols] = (y * gate).astype(BF16)

    o_ref[...] = h_ref[...] + _rms(jnp.concatenate(y_prev, axis=1), pn_ref[...])


def _even_mix_kernel(q_ref, kt_ref, vg_ref, cmix_ref, h_ref, wout_ref, gn_ref, pn_ref,
                     intra_ref, xi_ref, zeta_ref, gam_ref, bm_ref,
                     o_ref, state_ref, mix_a_ref, mix_b_ref):
    t = pl.program_id(0)
    args = (q_ref, kt_ref, vg_ref, cmix_ref, h_ref, wout_ref, gn_ref, pn_ref,
            intra_ref, xi_ref, zeta_ref, gam_ref, bm_ref, o_ref, state_ref)

    @pl.when(t % (SEQ // EVEN_MIX_R) == 0)
    def _():
        state_ref[...] = jnp.zeros_like(state_ref)

    @pl.when(t == 0)
    def _():
        mix_b_ref[...] = jnp.zeros_like(mix_b_ref)

    last = EVEN_MIX_STEPS
    assert last % 2 == 0

    @pl.when((t % 2 == 0) & (t < last))
    def _():
        _even_mix_block(*args, mix_b_ref, mix_a_ref)

    @pl.when(t % 2 == 1)
    def _():
        _even_mix_block(*args, mix_a_ref, mix_b_ref)

    @pl.when(t == last)
    def _():
        y = (jnp.dot(mix_b_ref[...], wout_ref[:RET_V, :], preferred_element_type=F32)
             + jnp.dot(cmix_ref[...], wout_ref[RET_V:, :], preferred_element_type=F32))
        o_ref[...] = h_ref[...] + _rms(y, pn_ref[...])


def _retention_tables():
    h, c = RET_HEADS, RET_CHUNK
    log_gamma = np.log1p(-np.exp2(-5.0 - np.arange(h, dtype=np.float64)))
    i = np.arange(c, dtype=np.float64)
    rel = i[:, None] - i[None, :]
    intra = np.where(rel >= 0, np.exp(log_gamma[:, None, None] * np.maximum(rel, 0.0)), 0.0)
    xi = np.exp(log_gamma[:, None] * (i + 1.0))
    zeta = np.exp(log_gamma[:, None] * (c - 1.0 - i))
    gamma_c = np.exp(log_gamma * c)
    pair = np.arange(RET_PAIRS)
    intra_p = intra.reshape(RET_PAIRS, 2, c, c).transpose(0, 2, 1, 3).reshape(RET_PAIRS, c, 2 * c)
    xi_p = np.repeat(xi.reshape(RET_PAIRS, 2, c).transpose(0, 2, 1), RET_DV, axis=2)
    second = ((np.arange(2 * RET_DK) % HALF_TILE) >= ROPE_HALF).astype(np.int64)
    zeta_p = zeta.reshape(RET_PAIRS, 2, c)[pair[:, None, None], second[None, :, None], np.arange(c)[None, None, :]]
    gam_rows = gamma_c.reshape(RET_PAIRS, 2)[:, second]
    gam_p = np.broadcast_to(gam_rows[:, :, None], (RET_PAIRS, 2 * RET_DK, 2 * RET_DV))
    col_head = np.arange(2 * RET_DV) // RET_DV
    bm = (second[:, None] == col_head[None, :])
    as_f32 = lambda a: jnp.asarray(np.ascontiguousarray(a, dtype=np.float32))
    return as_f32(intra_p), as_f32(xi_p), as_f32(zeta_p), as_f32(gam_p), as_f32(bm)


def _even_mix(q, kt, vg, cmix, h, w_out, gn, post_norm, tables, layer):
    R = EVEN_MIX_R
    nt = EVEN_MIX_STEPS
    e = layer // 2
    intra_p, xi_p, zeta_p, gam_p, bm = tables
    mix_map = lambda t: (jnp.minimum(t, nt - 1), 0)
    row_map = lambda t: (jnp.maximum(t - 1, 0), 0)
    const2 = lambda t: (0, 0)
    const3 = lambda t: (0, 0, 0)
    return pl.pallas_call(
        _even_mix_kernel,
        grid=(nt + 1,),
        in_specs=[
            pl.BlockSpec((R, RET_QK), mix_map),
            pl.BlockSpec((RET_QK, R), lambda t: (0, jnp.minimum(t, nt - 1))),
            pl.BlockSpec((R, 2 * RET_V), mix_map),
            pl.BlockSpec((R, CONV_W), row_map),
            pl.BlockSpec((R, D_MODEL), row_map),
            pl.BlockSpec((None, EVEN_MIX, D_MODEL), lambda t: (e, 0, 0)),
            pl.BlockSpec((None, 1, RET_V), lambda t: (e, 0, 0)),
            pl.BlockSpec((None, 1, D_MODEL), lambda t: (layer, 0, 0)),
            pl.BlockSpec((RET_PAIRS, RET_CHUNK, 2 * RET_CHUNK), const3),
            pl.BlockSpec((RET_PAIRS, RET_CHUNK, 2 * RET_DV), const3),
            pl.BlockSpec((RET_PAIRS, RET_CHUNK, 2 * RET_DK), const3),
            pl.BlockSpec((RET_PAIRS, 2 * RET_DK, 2 * RET_DV), const3),
            pl.BlockSpec((2 * RET_DK, 2 * RET_DV), const2),
        ],
        out_specs=pl.BlockSpec((R, D_MODEL), row_map),
        out_shape=jax.ShapeDtypeStruct((ROWS, D_MODEL), F32),
        scratch_shapes=[
            pltpu.VMEM((RET_PAIRS, 2 * RET_DK, 2 * RET_DV), F32),
            pltpu.VMEM((R, RET_V), BF16),
            pltpu.VMEM((R, RET_V), BF16),
        ],
        compiler_params=pltpu.CompilerParams(
            dimension_semantics=("arbitrary",), vmem_limit_bytes=VMEM_LIMIT),
        name="even_mix",
    )(q, kt, vg, cmix, h, w_out, gn, post_norm, intra_p, xi_p, zeta_p, gam_p, bm)


ODD_R = 1024
ATTN_T = 512


def _odd_in_kernel(x_ref, g_ref, wa_ref, wg_ref, qn_ref, wq_ref, kvn_ref, wkv_ref, cos_ref, sin_ref,
                   q_ref, k_ref, vt_ref, sg_ref):
    u = _rms(x_ref[...], g_ref[...]).astype(BF16)
    a = jnp.dot(u, wa_ref[...], preferred_element_type=F32)
    cq = a[:, :Q_LORA]
    ckv = a[:, Q_LORA:Q_LORA + KV_LORA]
    kr = a[:, Q_LORA + KV_LORA:]
    q = jnp.dot(_rms(cq, qn_ref[...]).astype(BF16), wq_ref[...], preferred_element_type=F32) * (MLA_SCALE * LOG2E)
    kv = jnp.dot(_rms(ckv, kvn_ref[...]).astype(BF16), wkv_ref[...], preferred_element_type=F32)
    cos = cos_ref[...]
    sin = sin_ref[...]
    kr_rot = _rope_tile(kr, cos, sin)
    kr_heads = (kr_rot.astype(BF16), pltpu.roll(kr_rot, ROPE_HALF, 1).astype(BF16))
    for hd in range(MLA_HEADS):
        lo, hi = hd * LANES, (hd + 1) * LANES
        q_ref[0, hd, 0:LANES, :] = q[:, lo:hi].T.astype(BF16)
        if hd % 2 == 0:
            pair = MLA_V + (hd // 2) * LANES
            q_rot_t = _rope_tile(q[:, pair:pair + LANES], cos, sin).T.astype(BF16)
        q_ref[0, hd, LANES:2 * LANES, :] = q_rot_t
        k_ref[0, hd, :, 0:LANES] = kv[:, lo:hi].astype(BF16)
        k_ref[0, hd, :, LANES:2 * LANES] = kr_heads[hd % 2]
        for s in range(ODD_R // ATTN_T):
            vt_ref[0, hd, s] = kv[s * ATTN_T:(s + 1) * ATTN_T, MLA_V + lo:MLA_V + hi].T.astype(BF16)
    gate = jnp.dot(u, wg_ref[...], preferred_element_type=F32)
    sg_ref[...] = _silu(gate).astype(BF16)


def _odd_in(h, pre_norm, wa, wg, qn, wq, kvn, wkv, cos_t, sin_t, layer):
    R = ODD_R
    nb = SEQ // R
    o = layer // 2
    row_map = lambda b, i: (b * nb + i, 0)
    odd_map = lambda b, i: (o, 0, 0)
    return pl.pallas_call(
        _odd_in_kernel,
        grid=(BATCH, nb),
        in_specs=[
            pl.BlockSpec((R, D_MODEL), row_map),
            pl.BlockSpec((None, 1, D_MODEL), lambda b, i: (layer, 0, 0)),
            pl.BlockSpec((None, D_MODEL, ODD_A), odd_map),
            pl.BlockSpec((None, D_MODEL, MLA_V), odd_map),
            pl.BlockSpec((None, 1, Q_LORA), odd_map),
            pl.BlockSpec((None, Q_LORA, ODD_WQ), odd_map),
            pl.BlockSpec((None, 1, KV_LORA), odd_map),
            pl.BlockSpec((None, KV_LORA, 2 * MLA_V), odd_map),
            pl.BlockSpec((R, LANES), row_map),
            pl.BlockSpec((R, LANES), row_map),
        ],
        out_specs=[
            pl.BlockSpec((1, MLA_HEADS, MLA_QK_PAD, R), lambda b, i: (b, 0, 0, i)),
            pl.BlockSpec((1, MLA_HEADS, R, MLA_QK_PAD), lambda b, i: (b, 0, i, 0)),
            pl.BlockSpec((1, MLA_HEADS, R // ATTN_T, MLA_DV, ATTN_T), lambda b, i: (b, 0, i, 0, 0)),
            pl.BlockSpec((R, MLA_V), row_map),
        ],
        out_shape=[
            jax.ShapeDtypeStruct((BATCH, MLA_HEADS, MLA_QK_PAD, SEQ), BF16),
            jax.ShapeDtypeStruct((BATCH, MLA_HEADS, SEQ, MLA_QK_PAD), BF16),
            jax.ShapeDtypeStruct((BATCH, MLA_HEADS, SEQ // ATTN_T, MLA_DV, ATTN_T), BF16),
            jax.ShapeDtypeStruct((ROWS, MLA_V), BF16),
        ],
        compiler_params=pltpu.CompilerParams(
            dimension_semantics=("arbitrary", "arbitrary"), vmem_limit_bytes=VMEM_LIMIT),
        name="odd_in",
    )(h, pre_norm, wa, wg, qn, wq, kvn, wkv, cos_t, sin_t)


ATTN_HG = 4


def _attn_kernel(q_ref, qn_ref, k_ref, vt_ref, sg_ref, o_ref, m_ref, l_ref, acc_ref, st_ref):
    T = ATTN_T
    qi = pl.program_id(2)
    m_ref[...] = jnp.full(m_ref.shape, -jnp.inf, F32)
    l_ref[...] = jnp.zeros(l_ref.shape, F32)
    acc_ref[...] = jnp.zeros(acc_ref.shape, F32)

    def scores(hh, j, queries=q_ref):
        k = k_ref[0, hh, pl.ds(pl.multiple_of(j * T, T), T), :]
        return jnp.dot(k, queries[0, hh], preferred_element_type=F32)

    def consume(hh, j, st):
        m_old = m_ref[hh]
        m_new = jnp.maximum(m_old, jnp.max(st, axis=0, keepdims=True))
        alpha = jnp.exp2(m_old - m_new)
        p = jnp.exp2(st - m_new)
        l_ref[hh] = alpha * l_ref[hh] + jnp.sum(p, axis=0, keepdims=True)
        acc_ref[hh] = alpha * acc_ref[hh] + jnp.dot(
            vt_ref[0, hh, j], p.astype(BF16), preferred_element_type=F32)
        m_ref[hh] = m_new

    @pl.when(qi == 0)
    def _():
        for hh in range(ATTN_HG):
            st_ref[hh] = scores(hh, 0)

    def body(j, carry):
        for hh in range(ATTN_HG):
            st = st_ref[hh]
            st_ref[hh] = scores(hh, j + 1)
            consume(hh, j, st)
        return carry

    lax.fori_loop(0, qi, body, 0)

    H = T // 2
    causal = lax.broadcasted_iota(jnp.int32, (H, H), 0) <= lax.broadcasted_iota(jnp.int32, (H, H), 1)

    def consume_diagonal(hh, st):
        s_ee = jnp.where(causal, st[:H, :H], -jnp.inf)
        s_el = st[:H, H:]
        s_ll = jnp.where(causal, st[H:, H:], -jnp.inf)
        m_old = m_ref[hh]
        col_max = jnp.concatenate(
            [jnp.max(s_ee, axis=0, keepdims=True),
             jnp.maximum(jnp.max(s_el, axis=0, keepdims=True), jnp.max(s_ll, axis=0, keepdims=True))], axis=1)
        m_new = jnp.maximum(m_old, col_max)
        alpha = jnp.exp2(m_old - m_new)
        p_ee = jnp.exp2(s_ee - m_new[:, :H])
        p_el = jnp.exp2(s_el - m_new[:, H:])
        p_ll = jnp.exp2(s_ll - m_new[:, H:])
        col_sum = jnp.concatenate(
            [jnp.sum(p_ee, axis=0, keepdims=True),
             jnp.sum(p_el, axis=0, keepdims=True) + jnp.sum(p_ll, axis=0, keepdims=True)], axis=1)
        l_ref[hh] = alpha * l_ref[hh] + col_sum
        vt = vt_ref[0, hh, qi]
        pv = jnp.concatenate(
            [jnp.dot(vt[:, :H], p_ee.astype(BF16), preferred_element_type=F32),
             jnp.dot(vt, jnp.concatenate([p_el, p_ll], axis=0).astype(BF16), preferred_element_type=F32)], axis=1)
        acc_ref[hh] = alpha * acc_ref[hh] + pv
        m_ref[hh] = m_new

    for hh in range(ATTN_HG):
        st = st_ref[hh]
        st_ref[hh] = scores(hh, 0, qn_ref)
        consume_diagonal(hh, st)
    for hh in range(ATTN_HG):
        cols = slice(hh * MLA_DV, (hh + 1) * MLA_DV)
        o = (acc_ref[hh] * (1.0 / l_ref[hh])).T
        o_ref[0, :, cols] = (o * sg_ref[0, :, cols].astype(F32)).astype(BF16)


def _attn(q_cat, k_cat, vt, sg):
    T = ATTN_T
    HG = ATTN_HG
    nq = SEQ // T
    return pl.pallas_call(
        _attn_kernel,
        grid=(BATCH, MLA_HEADS // HG, nq),
        in_specs=[
            pl.BlockSpec((1, HG, MLA_QK_PAD, T), lambda b, h, i: (b, h, 0, i)),
            pl.BlockSpec((1, HG, MLA_QK_PAD, T), lambda b, h, i: (b, h, 0, jnp.minimum(i + 1, nq - 1))),
            pl.BlockSpec((1, HG, SEQ, MLA_QK_PAD), lambda b, h, i: (b, h, 0, 0)),
            pl.BlockSpec((1, HG, nq, MLA_DV, T), lambda b, h, i: (b, h, 0, 0, 0)),
            pl.BlockSpec((1, T, HG * MLA_DV), lambda b, h, i: (b, i, h)),
        ],
        out_specs=pl.BlockSpec((1, T, HG * MLA_DV), lambda b, h, i: (b, i, h)),
        out_shape=jax.ShapeDtypeStruct((BATCH, SEQ, MLA_V), BF16),
        scratch_shapes=[
            pltpu.VMEM((HG, 1, T), F32),
            pltpu.VMEM((HG, 1, T), F32),
            pltpu.VMEM((HG, MLA_DV, T), F32),
            pltpu.VMEM((HG, T, T), F32),
        ],
        compiler_params=pltpu.CompilerParams(
            dimension_semantics=("arbitrary", "arbitrary", "arbitrary"), vmem_limit_bytes=VMEM_LIMIT),
        name="odd_attn",
    )(q_cat, q_cat, k_cat, vt, sg.reshape(BATCH, SEQ, MLA_V))


ODD_OUT_R = 2048


def _odd_out_kernel(og_ref, h_ref, w_ref, pn_ref, out_ref):
    y = jnp.dot(og_ref[...], w_ref[...], preferred_element_type=F32)
    out_ref[...] = h_ref[...] + _rms(y, pn_ref[...])


def _odd_out(og, h, w, post_norm, layer):
    R = ODD_OUT_R
    row_map = lambda i: (i, 0)
    return pl.pallas_call(
        _odd_out_kernel,
        grid=(ROWS // R,),
        in_specs=[
            pl.BlockSpec((R, MLA_V), row_map),
            pl.BlockSpec((R, D_MODEL), row_map),
            pl.BlockSpec((None, MLA_V, D_MODEL), lambda i: (layer // 2, 0, 0)),
            pl.BlockSpec((None, 1, D_MODEL), lambda i: (layer, 0, 0)),
        ],
        out_specs=pl.BlockSpec((R, D_MODEL), row_map),
        out_shape=jax.ShapeDtypeStruct((ROWS, D_MODEL), F32),
        compiler_params=pltpu.CompilerParams(
            dimension_semantics=("arbitrary",), vmem_limit_bytes=VMEM_LIMIT),
        name="odd_out",
    )(og, h, w, post_norm)


PREP_ROW_BLOCKS = 4


def _lane_iota(rows):
    return lax.broadcasted_iota(jnp.int32, (rows, LANES), 1)


def _copy_rows(vec_refs, out_refs):
    @pl.when((pl.program_id(0) == 0) & (pl.program_id(1) == 0))
    def _():
        for src, dst in zip(vec_refs, out_refs):
            for layer in range(src.shape[0]):
                dst[layer] = src[layer:layer + 1, :]


def _prep_even_kernel(win_ref, wout_ref, *rest):
    vec_refs, (win_o_ref, wout_o_ref), vec_out_refs = rest[:4], rest[4:6], rest[6:]
    _copy_rows(vec_refs, vec_out_refs)
    rows = win_ref.shape[0]
    lane = _lane_iota(rows)
    keep = (lane < ROPE_HALF) | (lane >= LANES - ROPE_HALF)
    for t in range(_V0 // LANES):
        cs = slice(t * LANES, (t + 1) * LANES)
        x = win_ref[:, cs]
        moved = jnp.where(lane < HALF_TILE, pltpu.roll(x, LANES - ROPE_HALF, 1), pltpu.roll(x, ROPE_HALF, 1))
        y = jnp.where(keep, x, moved)
        if t * LANES >= _K0:
            y = y * (RET_DK ** -0.5)
        win_o_ref[:, cs] = y.astype(BF16)
    win_o_ref[:, _V0:_CB0] = win_ref[:, _V0:_CB0].astype(BF16)
    for t in range(CONV_W // LANES):
        for k in range(4):
            src = _CB0 + k * CONV_W + t * LANES
            dst = _CB0 + (4 * t + k) * LANES
            win_o_ref[:, dst:dst + LANES] = win_ref[:, src:src + LANES].astype(BF16)
    wout_o_ref[...] = wout_ref[...].astype(BF16)


def _vec_specs(vecs):
    ins = [pl.BlockSpec(v.shape, lambda *_: (0, 0)) for v in vecs]
    outs = [pl.BlockSpec((v.shape[0], 1, v.shape[1]), lambda *_: (0, 0, 0)) for v in vecs]
    shapes = [jax.ShapeDtypeStruct((v.shape[0], 1, v.shape[1]), v.dtype) for v in vecs]
    return ins, outs, shapes


def _prep_even(even_w_in, even_w_out, vecs):
    n = even_w_in.shape[0]
    rb = PREP_ROW_BLOCKS
    spec = lambda rows, cols: pl.BlockSpec((None, rows // rb, cols), lambda e, r: (e, r, 0))
    vec_in, vec_out, vec_shapes = _vec_specs(vecs)
    outs = pl.pallas_call(
        _prep_even_kernel,
        grid=(n, rb),
        in_specs=[spec(D_MODEL, EVEN_IN), spec(EVEN_MIX, D_MODEL)] + vec_in,
        out_specs=[spec(D_MODEL, EVEN_IN), spec(EVEN_MIX, D_MODEL)] + vec_out,
        out_shape=[jax.ShapeDtypeStruct((n, D_MODEL, EVEN_IN), BF16),
                   jax.ShapeDtypeStruct((n, EVEN_MIX, D_MODEL), BF16)] + vec_shapes,
        compiler_params=pltpu.CompilerParams(
            dimension_semantics=("arbitrary", "arbitrary"), vmem_limit_bytes=VMEM_LIMIT),
        name="prep_even",
    )(even_w_in, even_w_out, *vecs)
    return outs[0], outs[1], outs[2:]


def _prep_odd_kernel(wint_ref, wqb_ref, wkvb_ref, wout_ref, qn_ref, kvn_ref,
                     wa_ref, wg_ref, wq_ref, wkv_ref, wo_ref, qn_o_ref, kvn_o_ref):
    _copy_rows((qn_ref, kvn_ref), (qn_o_ref, kvn_o_ref))
    lat = Q_LORA + KV_LORA
    cols = wint_ref.shape[1]
    wa_ref[:, :lat] = wint_ref[:lat, :].T.astype(BF16)
    zeros = jnp.zeros((ROPE_HALF, cols), F32)
    rope_rows = jnp.concatenate(
        [wint_ref[lat:lat + ROPE_HALF, :], zeros, wint_ref[lat + ROPE_HALF:lat + MLA_ROPE, :], zeros], axis=0)
    wa_ref[:, lat:] = rope_rows.T.astype(BF16)
    wg_ref[...] = wint_ref[lat + MLA_ROPE:, :].T.astype(BF16)
    lane = _lane_iota(wqb_ref.shape[0])
    for hp in range(MLA_HEADS // 2):
        t0, t1, t2 = (wqb_ref[:, (3 * hp + k) * LANES:(3 * hp + k + 1) * LANES] for k in range(3))
        ha, hb = 2 * hp, 2 * hp + 1
        wq_ref[:, ha * LANES:(ha + 1) * LANES] = t0.astype(BF16)
        wq_ref[:, hb * LANES:(hb + 1) * LANES] = jnp.where(
            lane < HALF_TILE, pltpu.roll(t1, HALF_TILE, 1), pltpu.roll(t2, HALF_TILE, 1)).astype(BF16)
        pair = jnp.where(lane < ROPE_HALF, t1,
                         jnp.where(lane < HALF_TILE, pltpu.roll(t2, LANES - ROPE_HALF, 1),
                                   jnp.where(lane < HALF_TILE + ROPE_HALF, pltpu.roll(t1, ROPE_HALF, 1), t2)))
        wq_ref[:, MLA_V + hp * LANES:MLA_V + (hp + 1) * LANES] = pair.astype(BF16)
    for hd in range(MLA_HEADS):
        wkv_ref[:, hd * LANES:(hd + 1) * LANES] = wkvb_ref[:, 2 * hd * LANES:(2 * hd + 1) * LANES].astype(BF16)
        wkv_ref[:, MLA_V + hd * LANES:MLA_V + (hd + 1) * LANES] = (
            wkvb_ref[:, (2 * hd + 1) * LANES:(2 * hd + 2) * LANES].astype(BF16))
    wo_ref[...] = wout_ref[...].astype(BF16)


def _prep_odd(odd_w_in, w_qb, w_kvb, odd_w_out, vecs):
    n = odd_w_in.shape[0]
    rb = PREP_ROW_BLOCKS
    spec = lambda rows, cols: pl.BlockSpec((None, rows // rb, cols), lambda o, r: (o, r, 0))
    shape = lambda rows, cols: jax.ShapeDtypeStruct((n, rows, cols), BF16)
    vec_in, vec_out, vec_shapes = _vec_specs(vecs)
    odd_w_in_t = jnp.swapaxes(odd_w_in, 1, 2)
    return pl.pallas_call(
        _prep_odd_kernel,
        grid=(n, rb),
        in_specs=[pl.BlockSpec((None, ODD_IN, D_MODEL // rb), lambda o, r: (o, 0, r)),
                  spec(Q_LORA, MLA_HEADS * MLA_QK),
                  spec(KV_LORA, 2 * MLA_V), spec(MLA_V, D_MODEL)] + vec_in,
        out_specs=[spec(D_MODEL, ODD_A), spec(D_MODEL, MLA_V), spec(Q_LORA, ODD_WQ),
                   spec(KV_LORA, 2 * MLA_V), spec(MLA_V, D_MODEL)] + vec_out,
        out_shape=[shape(D_MODEL, ODD_A), shape(D_MODEL, MLA_V), shape(Q_LORA, ODD_WQ),
                   shape(KV_LORA, 2 * MLA_V), shape(MLA_V, D_MODEL)] + vec_shapes,
        compiler_params=pltpu.CompilerParams(
            dimension_semantics=("arbitrary", "arbitrary"), vmem_limit_bytes=VMEM_LIMIT),
        name="prep_odd",
    )(odd_w_in_t, w_qb, w_kvb, odd_w_out, *vecs)


def kernel(x, positions, pre_norm, post_norm, even_w_in, even_conv_w, even_conv_b, ret_gn, even_w_out,
           odd_w_in, q_a_norm, w_qb, kv_a_norm, w_kvb, odd_w_out):
    cos_t, sin_t = _rope_tables(positions)
    tables = _retention_tables()
    ew_in, ew_out, (pre_norm, post_norm, ret_gn, even_conv_b) = _prep_even(
        even_w_in, even_w_out, (pre_norm, post_norm, ret_gn, even_conv_b))
    wa, wg, wq, wkv, wo, q_a_norm, kv_a_norm = _prep_odd(odd_w_in, w_qb, w_kvb, odd_w_out, (q_a_norm, kv_a_norm))
    h = x.reshape(ROWS, D_MODEL)
    for layer in range(DEPTH):
        if layer % 2 == 0:
            q, kt, vg, cmix = _even_in(h, pre_norm, ew_in, cos_t, sin_t, even_conv_w, even_conv_b, layer)
            h = _even_mix(q, kt, vg, cmix, h, ew_out, ret_gn, post_norm, tables, layer)
        else:
            q_cat, k_cat, vt, sg = _odd_in(h, pre_norm, wa, wg, q_a_norm, wq, kv_a_norm, wkv, cos_t, sin_t, layer)
            og = _attn(q_cat, k_cat, vt, sg)
            h = _odd_out(og.reshape(ROWS, MLA_V), h, wo, post_norm, layer)
    return h.reshape(BATCH, SEQ, D_MODEL)
```

```python
import math

import jax
import jax.numpy as jnp
import numpy as np
from jax import lax
from jax.experimental import pallas as pl
from jax.experimental.pallas import tpu as pltpu

D_MODEL = 1024
BATCH = 4
SEQ = 4096
DEPTH = 4
ROWS = BATCH * SEQ

RET_HEADS = 8
RET_DK = 64
RET_DV = 128
RET_CHUNK = 128
RET_PAIRS = RET_HEADS // 2
CONV_W = 1024
CONV_K = 3
MLA_HEADS = 8
MLA_NOPE = 128
MLA_ROPE = 64
MLA_DV = 128
Q_LORA = 384
KV_LORA = 256
ROPE_BASE = 10000.0
EPS = 1e-6

RET_QK = RET_HEADS * RET_DK
RET_V = RET_HEADS * RET_DV
EVEN_IN = 2 * RET_QK + 2 * RET_V + 4 * CONV_W
EVEN_MIX = RET_V + CONV_W
MLA_QK = MLA_NOPE + MLA_ROPE
MLA_V = MLA_HEADS * MLA_DV
MLA_SCALE = MLA_QK ** -0.5
LOG2E = math.log2(math.e)
MLA_QK_PAD = 256
ODD_IN = Q_LORA + KV_LORA + MLA_ROPE + MLA_V
ODD_A = Q_LORA + KV_LORA + 128
ODD_WQ = MLA_V + (MLA_HEADS // 2) * 128

_K0, _V0, _CB0 = 512, 1024, 3072

LANES = 128
SUBLANES = 8
ROPE_HALF = 32
HALF_TILE = LANES // 2

VMEM_LIMIT = 56 * 1024 * 1024

BF16 = jnp.bfloat16
F32 = jnp.float32


def _silu(x):
    return (0.5 * x) * (1.0 + jnp.tanh(0.5 * x))


def _rms(x, g):
    ms = jnp.mean(x * x, axis=-1, keepdims=True)
    return x * lax.rsqrt(ms + EPS) * g


def _rope_tile(x, cos, sin_signed):
    return x * cos + pltpu.roll(x, HALF_TILE, 1) * sin_signed


ROPE_GROUPS = LANES // ROPE_HALF
ROPE_BLK = 1024


def _rope_table_kernel(pos_ref, inv_ref, cos_ref, sin_ref):
    ang = pos_ref[...].astype(F32) * inv_ref[...]
    c = jnp.cos(ang)
    s = jnp.sin(ang)
    lane = lax.broadcasted_iota(jnp.int32, (ROPE_BLK, LANES), 1)
    low = lane < ROPE_HALF
    for g in range(ROPE_GROUPS):
        shift = (LANES - g * ROPE_HALF) % LANES
        cg = jnp.where(low, c if shift == 0 else pltpu.roll(c, shift, 1), 0.0)
        sg = jnp.where(low, s if shift == 0 else pltpu.roll(s, shift, 1), 0.0)
        cg = cg + pltpu.roll(cg, ROPE_HALF, 1)
        sg = sg + pltpu.roll(sg, ROPE_HALF, 1)
        cos_ref[g] = cg + pltpu.roll(cg, HALF_TILE, 1)
        sin_ref[g] = pltpu.roll(sg, HALF_TILE, 1) - sg


def _rope_tables(positions):
    inv = ROPE_BASE ** (-jnp.arange(0, 2 * ROPE_HALF, 2, dtype=F32) / (2 * ROPE_HALF))
    dense_rows = ROWS // ROPE_GROUPS
    pos_dense = jnp.repeat(positions.reshape(ROPE_GROUPS, dense_rows).T, ROPE_HALF, axis=1)
    inv_dense = jnp.tile(inv, ROPE_GROUPS).reshape(1, LANES)
    out_spec = pl.BlockSpec((ROPE_GROUPS, ROPE_BLK, LANES), lambda i: (0, i, 0))
    out = jax.ShapeDtypeStruct((ROPE_GROUPS, dense_rows, LANES), F32)
    cos_t, sin_t = pl.pallas_call(
        _rope_table_kernel,
        grid=(dense_rows // ROPE_BLK,),
        in_specs=[pl.BlockSpec((ROPE_BLK, LANES), lambda i: (i, 0)),
                  pl.BlockSpec((1, LANES), lambda i: (0, 0))],
        out_specs=[out_spec, out_spec],
        out_shape=[out, out],
        name="rope_table",
    )(pos_dense, inv_dense)
    return cos_t.reshape(ROWS, LANES), sin_t.reshape(ROWS, LANES)


EVEN_IN_TM = 2048
EVEN_IN_TN = 1024
EVEN_QKVG = 3 * EVEN_IN_TN
CONV_BLK = EVEN_IN_TN // 4
CONV_BLOCKS = CONV_W // CONV_BLK
EVEN_IN_STRIP = 256
CONV_STRIP = 128


def _even_in_kernel(x_ref, g_ref, w_ref, cos_ref, sin_ref, cw_ref, cb_ref,
                    q_ref, kt_ref, vg_ref, cmix_ref, u_ref, tail_ref):
    i = pl.program_id(0)
    j = pl.program_id(1)
    strips = [slice(r * EVEN_IN_STRIP, (r + 1) * EVEN_IN_STRIP) for r in range(EVEN_IN_TM // EVEN_IN_STRIP)]

    halves = [slice(0, EVEN_IN_TN // 2), slice(EVEN_IN_TN // 2, EVEN_IN_TN)]

    @pl.when(j == 0)
    def _():
        for rs in strips:
            u = _rms(x_ref[rs, :], g_ref[...]).astype(BF16)
            u_ref[rs, :] = u
            cos = cos_ref[rs, :]
            sin = sin_ref[rs, :]
            for ns in halves:
                acc = jnp.dot(u, w_ref[:, ns], preferred_element_type=F32)
                for c in range((ns.stop - ns.start) // LANES):
                    cs = slice(c * LANES, (c + 1) * LANES)
                    y = _rope_tile(acc[:, cs], cos, sin)
                    if ns.start >= _K0:
                        kt_ref[cs, rs] = y.T.astype(BF16)
                    else:
                        q_ref[rs, cs] = y.astype(BF16)

    @pl.when(j == 1)
    def _():
        vg_ref[...] = jnp.dot(u_ref[...], w_ref[...], preferred_element_type=F32).astype(BF16)

    @pl.when(j == 2)
    def _():
        for rs in strips:
            for ns in halves:
                acc = jnp.dot(u_ref[rs, :], w_ref[:, ns], preferred_element_type=F32)
                vg_ref[rs, ns] = _silu(acc).astype(BF16)

    @pl.when(j >= 3)
    def _():
        slot = j - 3
        sequence_start = (i % (SEQ // EVEN_IN_TM)) == 0
        lane_tiles = [slice(ct * LANES, (ct + 1) * LANES) for ct in range(CONV_BLK // LANES)]
        halos = [jnp.where(sequence_start, 0.0, tail_ref[slot, :, cs]) for cs in lane_tiles]
        for rs in strips:
            for ct, cs in enumerate(lane_tiles):
                acc = jnp.dot(u_ref[rs, :], w_ref[:, ct * 4 * LANES:(ct + 1) * 4 * LANES],
                              preferred_element_type=F32)
                for r in range(EVEN_IN_STRIP // CONV_STRIP):
                    rr = slice(r * CONV_STRIP, (r + 1) * CONV_STRIP)
                    part = lambda k: acc[rr, k * LANES:(k + 1) * LANES]
                    prod = part(1) * part(2)
                    ext = jnp.concatenate([halos[ct], prod], axis=0)
                    conv = cb_ref[:, cs] + prod * cw_ref[CONV_K - 1:CONV_K, cs]
                    for tap in range(CONV_K - 1):
                        off = SUBLANES - (CONV_K - 1) + tap
                        conv = conv + ext[off:off + CONV_STRIP, :] * cw_ref[tap:tap + 1, cs]
                    halos[ct] = prod[CONV_STRIP - SUBLANES:, :]
                    out_rows = slice(rs.start + rr.start, rs.start + rr.stop)
                    cmix_ref[out_rows, cs] = (part(0) * conv * _silu(part(3))).astype(BF16)
        for ct, cs in enumerate(lane_tiles):
            tail_ref[slot, :, cs] = halos[ct]


def _even_in(h, pre_norm, w, cos_t, sin_t, conv_w, conv_b, layer):
    tm, tn = EVEN_IN_TM, EVEN_IN_TN
    e = layer // 2
    qkvg_blocks = EVEN_QKVG // tn
    vg_blocks = qkvg_blocks - 1
    conv_map = lambda i, j: (e, 0, jnp.maximum(j - qkvg_blocks, 0))
    return pl.pallas_call(
        _even_in_kernel,
        grid=(ROWS // tm, EVEN_IN // tn),
        in_specs=[
            pl.BlockSpec((tm, D_MODEL), lambda i, j: (i, 0)),
            pl.BlockSpec((None, 1, D_MODEL), lambda i, j: (layer, 0, 0)),
            pl.BlockSpec((None, D_MODEL, tn), lambda i, j: (e, 0, j)),
            pl.BlockSpec((tm, LANES), lambda i, j: (i, 0)),
            pl.BlockSpec((tm, LANES), lambda i, j: (i, 0)),
            pl.BlockSpec((None, CONV_K, CONV_BLK), conv_map),
            pl.BlockSpec((None, 1, CONV_BLK), conv_map),
        ],
        out_specs=[
            pl.BlockSpec((tm, RET_QK), lambda i, j: (i, 0)),
            pl.BlockSpec((RET_QK, tm), lambda i, j: (0, i)),
            pl.BlockSpec((tm, tn), lambda i, j: (i, jnp.clip(j - 1, 0, vg_blocks - 1))),
            pl.BlockSpec((tm, CONV_BLK), lambda i, j: (i, jnp.maximum(j - qkvg_blocks, 0))),
        ],
        out_shape=[
            jax.ShapeDtypeStruct((ROWS, RET_QK), BF16),
            jax.ShapeDtypeStruct((RET_QK, ROWS), BF16),
            jax.ShapeDtypeStruct((ROWS, vg_blocks * tn), BF16),
            jax.ShapeDtypeStruct((ROWS, CONV_W), BF16),
        ],
        scratch_shapes=[
            pltpu.VMEM((tm, D_MODEL), BF16),
            pltpu.VMEM((CONV_BLOCKS, SUBLANES, CONV_BLK), F32),
        ],
        compiler_params=pltpu.CompilerParams(
            dimension_semantics=("arbitrary", "arbitrary"), vmem_limit_bytes=VMEM_LIMIT),
        name="even_in",
    )(h, pre_norm, w, cos_t, sin_t, conv_w, conv_b)


EVEN_MIX_R = 512
EVEN_MIX_STEPS = ROWS // EVEN_MIX_R
PROJ_PIECE = 256


def _even_mix_block(q_ref, kt_ref, vg_ref, cmix_ref, h_ref, wout_ref, gn_ref, pn_ref,
                    intra_ref, xi_ref, zeta_ref, gam_ref, bm_ref,
                    o_ref, state_ref, mix_prev_ref, mix_ref):
    C = RET_CHUNK
    chunks = EVEN_MIX_R // C

    row_k = lax.broadcasted_iota(jnp.int32, (LANES, C), 0)
    first_head = (row_k % HALF_TILE) < ROPE_HALF
    lane_v = lax.broadcasted_iota(jnp.int32, (C, 2 * RET_DV), 1)
    zero_k = jnp.zeros((LANES, C), BF16)
    zero_v = jnp.zeros((C, 2 * RET_DV), BF16)

    y_prev = []

    def project_piece():
        ns = slice(len(y_prev) * PROJ_PIECE, (len(y_prev) + 1) * PROJ_PIECE)
        y_prev.append(jnp.dot(mix_prev_ref[...], wout_ref[:RET_V, ns], preferred_element_type=F32)
                      + jnp.dot(cmix_ref[...], wout_ref[RET_V:, ns], preferred_element_type=F32))

    stages_per_piece = chunks * RET_PAIRS * PROJ_PIECE // D_MODEL
    for c, j in [(c, p) for c in range(chunks) for p in range(RET_PAIRS)]:
        rows = slice(c * C, (c + 1) * C)
        if (c * RET_PAIRS + j) % stages_per_piece == 0:
            project_piece()
        qp = q_ref[rows, j * LANES:(j + 1) * LANES]
        kt = kt_ref[j * LANES:(j + 1) * LANES, rows]
        vp = vg_ref[rows, j * 2 * RET_DV:(j + 1) * 2 * RET_DV]
        kt_heads = jnp.concatenate(
            [jnp.where(first_head, kt, zero_k), jnp.where(first_head, zero_k, kt)], axis=1)
        sc = jnp.dot(qp, kt_heads, preferred_element_type=F32)
        pm = (sc * intra_ref[j]).astype(BF16)
        vblk = jnp.concatenate(
            [jnp.where(lane_v < RET_DV, vp, zero_v), jnp.where(lane_v >= RET_DV, vp, zero_v)], axis=0)
        st = state_ref[j]
        r = (jnp.dot(pm, vblk, preferred_element_type=F32)
             + jnp.dot(qp, st.astype(BF16), preferred_element_type=F32) * xi_ref[j])
        kz = (kt.astype(F32) * zeta_ref[j]).astype(BF16)
        kv = jnp.dot(kz, vp, preferred_element_type=F32)
        state_ref[j] = st * gam_ref[j] + kv * bm_ref[...]
        for hh in range(2):
            hd = 2 * j + hh
            cols = slice(hd * RET_DV, (hd + 1) * RET_DV)
            y = _rms(r[:, hh * RET_DV:(hh + 1) * RET_DV], gn_ref[:, cols])
            gate = vg_ref[rows, RET_V + hd * RET_DV:RET_V + (hd + 1) * RET_DV].astype(F32)
            mix_ref[rows, cols] = (y * gate).astype(BF16)

    o_ref[...] = h_ref[...] + _rms(jnp.concatenate(y_prev, axis=1), pn_ref[...])


def _even_mix_kernel(q_ref, kt_ref, vg_ref, cmix_ref, h_ref, wout_ref, gn_ref, pn_ref,
                     intra_ref, xi_ref, zeta_ref, gam_ref, bm_ref,
                     o_ref, state_ref, mix_a_ref, mix_b_ref):
    t = pl.program_id(0)
    args = (q_ref, kt_ref, vg_ref, cmix_ref, h_ref, wout_ref, gn_ref, pn_ref,
            intra_ref, xi_ref, zeta_ref, gam_ref, bm_ref, o_ref, state_ref)

    @pl.when(t % (SEQ // EVEN_MIX_R) == 0)
    def _():
        state_ref[...] = jnp.zeros_like(state_ref)

    @pl.when(t == 0)
    def _():
        mix_b_ref[...] = jnp.zeros_like(mix_b_ref)

    last = EVEN_MIX_STEPS
    assert last % 2 == 0

    @pl.when((t % 2 == 0) & (t < last))
    def _():
        _even_mix_block(*args, mix_b_ref, mix_a_ref)

    @pl.when(t % 2 == 1)
    def _():
        _even_mix_block(*args, mix_a_ref, mix_b_ref)

    @pl.when(t == last)
    def _():
        y = (jnp.dot(mix_b_ref[...], wout_ref[:RET_V, :], preferred_element_type=F32)
             + jnp.dot(cmix_ref[...], wout_ref[RET_V:, :], preferred_element_type=F32))
        o_ref[...] = h_ref[...] + _rms(y, pn_ref[...])


def _retention_tables():
    h, c = RET_HEADS, RET_CHUNK
    log_gamma = np.log1p(-np.exp2(-5.0 - np.arange(h, dtype=np.float64)))
    i = np.arange(c, dtype=np.float64)
    rel = i[:, None] - i[None, :]
    intra = np.where(rel >= 0, np.exp(log_gamma[:, None, None] * np.maximum(rel, 0.0)), 0.0)
    xi = np.exp(log_gamma[:, None] * (i + 1.0))
    zeta = np.exp(log_gamma[:, None] * (c - 1.0 - i))
    gamma_c = np.exp(log_gamma * c)
    pair = np.arange(RET_PAIRS)
    intra_p = intra.reshape(RET_PAIRS, 2, c, c).transpose(0, 2, 1, 3).reshape(RET_PAIRS, c, 2 * c)
    xi_p = np.repeat(xi.reshape(RET_PAIRS, 2, c).transpose(0, 2, 1), RET_DV, axis=2)
    second = ((np.arange(2 * RET_DK) % HALF_TILE) >= ROPE_HALF).astype(np.int64)
    zeta_p = zeta.reshape(RET_PAIRS, 2, c)[pair[:, None, None], second[None, :, None], np.arange(c)[None, None, :]]
    gam_rows = gamma_c.reshape(RET_PAIRS, 2)[:, second]
    gam_p = np.broadcast_to(gam_rows[:, :, None], (RET_PAIRS, 2 * RET_DK, 2 * RET_DV))
    col_head = np.arange(2 * RET_DV) // RET_DV
    bm = (second[:, None] == col_head[None, :])
    as_f32 = lambda a: jnp.asarray(np.ascontiguousarray(a, dtype=np.float32))
    return as_f32(intra_p), as_f32(xi_p), as_f32(zeta_p), as_f32(gam_p), as_f32(bm)


def _even_mix(q, kt, vg, cmix, h, w_out, gn, post_norm, tables, layer):
    R = EVEN_MIX_R
    nt = EVEN_MIX_STEPS
    e = layer // 2
    intra_p, xi_p, zeta_p, gam_p, bm = tables
    mix_map = lambda t: (jnp.minimum(t, nt - 1), 0)
    row_map = lambda t: (jnp.maximum(t - 1, 0), 0)
    const2 = lambda t: (0, 0)
    const3 = lambda t: (0, 0, 0)
    return pl.pallas_call(
        _even_mix_kernel,
        grid=(nt + 1,),
        in_specs=[
            pl.BlockSpec((R, RET_QK), mix_map),
            pl.BlockSpec((RET_QK, R), lambda t: (0, jnp.minimum(t, nt - 1))),
            pl.BlockSpec((R, 2 * RET_V), mix_map),
            pl.BlockSpec((R, CONV_W), row_map),
            pl.BlockSpec((R, D_MODEL), row_map),
            pl.BlockSpec((None, EVEN_MIX, D_MODEL), lambda t: (e, 0, 0)),
            pl.BlockSpec((None, 1, RET_V), lambda t: (e, 0, 0)),
            pl.BlockSpec((None, 1, D_MODEL), lambda t: (layer, 0, 0)),
            pl.BlockSpec((RET_PAIRS, RET_CHUNK, 2 * RET_CHUNK), const3),
            pl.BlockSpec((RET_PAIRS, RET_CHUNK, 2 * RET_DV), const3),
            pl.BlockSpec((RET_PAIRS, RET_CHUNK, 2 * RET_DK), const3),
            pl.BlockSpec((RET_PAIRS, 2 * RET_DK, 2 * RET_DV), const3),
            pl.BlockSpec((2 * RET_DK, 2 * RET_DV), const2),
        ],
        out_specs=pl.BlockSpec((R, D_MODEL), row_map),
        out_shape=jax.ShapeDtypeStruct((ROWS, D_MODEL), F32),
        scratch_shapes=[
            pltpu.VMEM((RET_PAIRS, 2 * RET_DK, 2 * RET_DV), F32),
            pltpu.VMEM((R, RET_V), BF16),
            pltpu.VMEM((R, RET_V), BF16),
        ],
        compiler_params=pltpu.CompilerParams(
            dimension_semantics=("arbitrary",), vmem_limit_bytes=VMEM_LIMIT),
        name="even_mix",
    )(q, kt, vg, cmix, h, w_out, gn, post_norm, intra_p, xi_p, zeta_p, gam_p, bm)


ODD_R = 1024
ATTN_T = 512


def _odd_in_kernel(x_ref, g_ref, wa_ref, wg_ref, qn_ref, wq_ref, kvn_ref, wkv_ref, cos_ref, sin_ref,
                   q_ref, k_ref, vt_ref, sg_ref):
    u = _rms(x_ref[...], g_ref[...]).astype(BF16)
    a = jnp.dot(u, wa_ref[...], preferred_element_type=F32)
    cq = a[:, :Q_LORA]
    ckv = a[:, Q_LORA:Q_LORA + KV_LORA]
    kr = a[:, Q_LORA + KV_LORA:]
    q = jnp.dot(_rms(cq, qn_ref[...]).astype(BF16), wq_ref[...], preferred_element_type=F32) * (MLA_SCALE * LOG2E)
    kv = jnp.dot(_rms(ckv, kvn_ref[...]).astype(BF16), wkv_ref[...], preferred_element_type=F32)
    cos = cos_ref[...]
    sin = sin_ref[...]
    kr_rot = _rope_tile(kr, cos, sin)
    kr_heads = (kr_rot.astype(BF16), pltpu.roll(kr_rot, ROPE_HALF, 1).astype(BF16))
    for hd in range(MLA_HEADS):
        lo, hi = hd * LANES, (hd + 1) * LANES
        q_ref[0, hd, 0:LANES, :] = q[:, lo:hi].T.astype(BF16)
        if hd % 2 == 0:
            pair = MLA_V + (hd // 2) * LANES
            q_rot_t = _rope_tile(q[:, pair:pair + LANES], cos, sin).T.astype(BF16)
        q_ref[0, hd, LANES:2 * LANES, :] = q_rot_t
        k_ref[0, hd, :, 0:LANES] = kv[:, lo:hi].astype(BF16)
        k_ref[0, hd, :, LANES:2 * LANES] = kr_heads[hd % 2]
        for s in range(ODD_R // ATTN_T):
            vt_ref[0, hd, s] = kv[s * ATTN_T:(s + 1) * ATTN_T, MLA_V + lo:MLA_V + hi].T.astype(BF16)
    gate = jnp.dot(u, wg_ref[...], preferred_element_type=F32)
    sg_ref[...] = _silu(gate).astype(BF16)


def _odd_in(h, pre_norm, wa, wg, qn, wq, kvn, wkv, cos_t, sin_t, layer):
    R = ODD_R
    nb = SEQ // R
    o = layer // 2
    row_map = lambda b, i: (b * nb + i, 0)
    odd_map = lambda b, i: (o, 0, 0)
    return pl.pallas_call(
        _odd_in_kernel,
        grid=(BATCH, nb),
        in_specs=[
            pl.BlockSpec((R, D_MODEL), row_map),
            pl.BlockSpec((None, 1, D_MODEL), lambda b, i: (layer, 0, 0)),
            pl.BlockSpec((None, D_MODEL, ODD_A), odd_map),
            pl.BlockSpec((None, D_MODEL, MLA_V), odd_map),
            pl.BlockSpec((None, 1, Q_LORA), odd_map),
            pl.BlockSpec((None, Q_LORA, ODD_WQ), odd_map),
            pl.BlockSpec((None, 1, KV_LORA), odd_map),
            pl.BlockSpec((None, KV_LORA, 2 * MLA_V), odd_map),
            pl.BlockSpec((R, LANES), row_map),
            pl.BlockSpec((R, LANES), row_map),
        ],
        out_specs=[
            pl.BlockSpec((1, MLA_HEADS, MLA_QK_PAD, R), lambda b, i: (b, 0, 0, i)),
            pl.BlockSpec((1, MLA_HEADS, R, MLA_QK_PAD), lambda b, i: (b, 0, i, 0)),
            pl.BlockSpec((1, MLA_HEADS, R // ATTN_T, MLA_DV, ATTN_T), lambda b, i: (b, 0, i, 0, 0)),
            pl.BlockSpec((R, MLA_V), row_map),
        ],
        out_shape=[
            jax.ShapeDtypeStruct((BATCH, MLA_HEADS, MLA_QK_PAD, SEQ), BF16),
            jax.ShapeDtypeStruct((BATCH, MLA_HEADS, SEQ, MLA_QK_PAD), BF16),
            jax.ShapeDtypeStruct((BATCH, MLA_HEADS, SEQ // ATTN_T, MLA_DV, ATTN_T), BF16),
            jax.ShapeDtypeStruct((ROWS, MLA_V), BF16),
        ],
        compiler_params=pltpu.CompilerParams(
            dimension_semantics=("arbitrary", "arbitrary"), vmem_limit_bytes=VMEM_LIMIT),
        name="odd_in",
    )(h, pre_norm, wa, wg, qn, wq, kvn, wkv, cos_t, sin_t)


ATTN_HG = 8


def _attn_kernel(q_ref, qn_ref, k_ref, vt_ref, sg_ref, o_ref, m_ref, l_ref, acc_ref, st_ref):
    T = ATTN_T
    qi = pl.program_id(2)
    m_ref[...] = jnp.full(m_ref.shape, -jnp.inf, F32)
    l_ref[...] = jnp.zeros(l_ref.shape, F32)
    acc_ref[...] = jnp.zeros(acc_ref.shape, F32)

    def scores(hh, j, queries=q_ref):
        k = k_ref[0, hh, pl.ds(pl.multiple_of(j * T, T), T), :]
        return jnp.dot(k, queries[0, hh], preferred_element_type=F32)

    def consume(hh, j, st):
        m_old = m_ref[hh]
        m_new = jnp.maximum(m_old, jnp.max(st, axis=0, keepdims=True))
        alpha = jnp.exp2(m_old - m_new)
        p = jnp.exp2(st - m_new)
        l_ref[hh] = alpha * l_ref[hh] + jnp.sum(p, axis=0, keepdims=True)
        acc_ref[hh] = alpha * acc_ref[hh] + jnp.dot(
            vt_ref[0, hh, j], p.astype(BF16), preferred_element_type=F32)
        m_ref[hh] = m_new

    @pl.when(qi == 0)
    def _():
        for hh in range(ATTN_HG):
            st_ref[hh] = scores(hh, 0)

    def body(j, carry):
        for hh in range(ATTN_HG):
            st = st_ref[hh]
            st_ref[hh] = scores(hh, j + 1)
            consume(hh, j, st)
        return carry

    lax.fori_loop(0, qi, body, 0)

    H = T // 2
    causal = lax.broadcasted_iota(jnp.int32, (H, H), 0) <= lax.broadcasted_iota(jnp.int32, (H, H), 1)

    def consume_diagonal(hh, st):
        s_ee = jnp.where(causal, st[:H, :H], -jnp.inf)
        s_el = st[:H, H:]
        s_ll = jnp.where(causal, st[H:, H:], -jnp.inf)
        m_old = m_ref[hh]
        col_max = jnp.concatenate(
            [jnp.max(s_ee, axis=0, keepdims=True),
             jnp.maximum(jnp.max(s_el, axis=0, keepdims=True), jnp.max(s_ll, axis=0, keepdims=True))], axis=1)
        m_new = jnp.maximum(m_old, col_max)
        alpha = jnp.exp2(m_old - m_new)
        p_ee = jnp.exp2(s_ee - m_new[:, :H])
        p_el = jnp.exp2(s_el - m_new[:, H:])
        p_ll = jnp.exp2(s_ll - m_new[:, H:])
        col_sum = jnp.concatenate(
            [jnp.sum(p_ee, axis=0, keepdims=True),
             jnp.sum(p_el, axis=0, keepdims=True) + jnp.sum(p_ll, axis=0, keepdims=True)], axis=1)
        l_ref[hh] = alpha * l_ref[hh] + col_sum
        vt = vt_ref[0, hh, qi]
        pv = jnp.concatenate(
            [jnp.dot(vt[:, :H], p_ee.astype(BF16), preferred_element_type=F32),
             jnp.dot(vt, jnp.concatenate([p_el, p_ll], axis=0).astype(BF16), preferred_element_type=F32)], axis=1)
        acc_ref[hh] = alpha * acc_ref[hh] + pv
        m_ref[hh] = m_new

    for hh in range(ATTN_HG):
        st = st_ref[hh]
        st_ref[hh] = scores(hh, 0, qn_ref)
        consume_diagonal(hh, st)
    for hh in range(ATTN_HG):
        cols = slice(hh * MLA_DV, (hh + 1) * MLA_DV)
        o = (acc_ref[hh] * (1.0 / l_ref[hh])).T
        o_ref[0, :, cols] = (o * sg_ref[0, :, cols].astype(F32)).astype(BF16)


def _attn(q_cat, k_cat, vt, sg):
    T = ATTN_T
    HG = ATTN_HG
    nq = SEQ // T
    return pl.pallas_call(
        _attn_kernel,
        grid=(BATCH, MLA_HEADS // HG, nq),
        in_specs=[
            pl.BlockSpec((1, HG, MLA_QK_PAD, T), lambda b, h, i: (b, h, 0, i)),
            pl.BlockSpec((1, HG, MLA_QK_PAD, T), lambda b, h, i: (b, h, 0, jnp.minimum(i + 1, nq - 1))),
            pl.BlockSpec((1, HG, SEQ, MLA_QK_PAD), lambda b, h, i: (b, h, 0, 0), pipeline_mode=pl.Buffered(1)),
            pl.BlockSpec((1, HG, nq, MLA_DV, T), lambda b, h, i: (b, h, 0, 0, 0), pipeline_mode=pl.Buffered(1)),
            pl.BlockSpec((1, T, HG * MLA_DV), lambda b, h, i: (b, i, h)),
        ],
        out_specs=pl.BlockSpec((1, T, HG * MLA_DV), lambda b, h, i: (b, i, h)),
        out_shape=jax.ShapeDtypeStruct((BATCH, SEQ, MLA_V), BF16),
        scratch_shapes=[
            pltpu.VMEM((HG, 1, T), F32),
            pltpu.VMEM((HG, 1, T), F32),
            pltpu.VMEM((HG, MLA_DV, T), F32),
            pltpu.VMEM((HG, T, T), F32),
        ],
        compiler_params=pltpu.CompilerParams(
            dimension_semantics=("arbitrary", "arbitrary", "arbitrary"), vmem_limit_bytes=VMEM_LIMIT),
        name="odd_attn",
    )(q_cat, q_cat, k_cat, vt, sg.reshape(BATCH, SEQ, MLA_V))


ODD_OUT_R = 2048


def _odd_out_kernel(og_ref, h_ref, w_ref, pn_ref, out_ref):
    y = jnp.dot(og_ref[...], w_ref[...], preferred_element_type=F32)
    out_ref[...] = h_ref[...] + _rms(y, pn_ref[...])


def _odd_out(og, h, w, post_norm, layer):
    R = ODD_OUT_R
    row_map = lambda i: (i, 0)
    return pl.pallas_call(
        _odd_out_kernel,
        grid=(ROWS // R,),
        in_specs=[
            pl.BlockSpec((R, MLA_V), row_map),
            pl.BlockSpec((R, D_MODEL), row_map),
            pl.BlockSpec((None, MLA_V, D_MODEL), lambda i: (layer // 2, 0, 0)),
            pl.BlockSpec((None, 1, D_MODEL), lambda i: (layer, 0, 0)),
        ],
        out_specs=pl.BlockSpec((R, D_MODEL), row_map),
        out_shape=jax.ShapeDtypeStruct((ROWS, D_MODEL), F32),
        compiler_params=pltpu.CompilerParams(
            dimension_semantics=("arbitrary",), vmem_limit_bytes=VMEM_LIMIT),
        name="odd_out",
    )(og, h, w, post_norm)


PREP_ROW_BLOCKS = 4


def _lane_iota(rows):
    return lax.broadcasted_iota(jnp.int32, (rows, LANES), 1)


def _copy_rows(vec_refs, out_refs):
    @pl.when((pl.program_id(0) == 0) & (pl.program_id(1) == 0))
    def _():
        for src, dst in zip(vec_refs, out_refs):
            for layer in range(src.shape[0]):
                dst[layer] = src[layer:layer + 1, :]


def _prep_even_kernel(win_ref, wout_ref, *rest):
    vec_refs, (win_o_ref, wout_o_ref), vec_out_refs = rest[:4], rest[4:6], rest[6:]
    _copy_rows(vec_refs, vec_out_refs)
    rows = win_ref.shape[0]
    lane = _lane_iota(rows)
    keep = (lane < ROPE_HALF) | (lane >= LANES - ROPE_HALF)
    for t in range(_V0 // LANES):
        cs = slice(t * LANES, (t + 1) * LANES)
        x = win_ref[:, cs]
        moved = jnp.where(lane < HALF_TILE, pltpu.roll(x, LANES - ROPE_HALF, 1), pltpu.roll(x, ROPE_HALF, 1))
        y = jnp.where(keep, x, moved)
        if t * LANES >= _K0:
            y = y * (RET_DK ** -0.5)
        win_o_ref[:, cs] = y.astype(BF16)
    win_o_ref[:, _V0:_CB0] = win_ref[:, _V0:_CB0].astype(BF16)
    for t in range(CONV_W // LANES):
        for k in range(4):
            src = _CB0 + k * CONV_W + t * LANES
            dst = _CB0 + (4 * t + k) * LANES
            win_o_ref[:, dst:dst + LANES] = win_ref[:, src:src + LANES].astype(BF16)
    wout_o_ref[...] = wout_ref[...].astype(BF16)


def _vec_specs(vecs):
    ins = [pl.BlockSpec(v.shape, lambda *_: (0, 0)) for v in vecs]
    outs = [pl.BlockSpec((v.shape[0], 1, v.shape[1]), lambda *_: (0, 0, 0)) for v in vecs]
    shapes = [jax.ShapeDtypeStruct((v.shape[0], 1, v.shape[1]), v.dtype) for v in vecs]
    return ins, outs, shapes


def _prep_even(even_w_in, even_w_out, vecs):
    n = even_w_in.shape[0]
    rb = PREP_ROW_BLOCKS
    spec = lambda rows, cols: pl.BlockSpec((None, rows // rb, cols), lambda e, r: (e, r, 0))
    vec_in, vec_out, vec_shapes = _vec_specs(vecs)
    outs = pl.pallas_call(
        _prep_even_kernel,
        grid=(n, rb),
        in_specs=[spec(D_MODEL, EVEN_IN), spec(EVEN_MIX, D_MODEL)] + vec_in,
        out_specs=[spec(D_MODEL, EVEN_IN), spec(EVEN_MIX, D_MODEL)] + vec_out,
        out_shape=[jax.ShapeDtypeStruct((n, D_MODEL, EVEN_IN), BF16),
                   jax.ShapeDtypeStruct((n, EVEN_MIX, D_MODEL), BF16)] + vec_shapes,
        compiler_params=pltpu.CompilerParams(
            dimension_semantics=("arbitrary", "arbitrary"), vmem_limit_bytes=VMEM_LIMIT),
        name="prep_even",
    )(even_w_in, even_w_out, *vecs)
    return outs[0], outs[1], outs[2:]


def _prep_odd_kernel(wint_ref, wqb_ref, wkvb_ref, wout_ref, qn_ref, kvn_ref,
                     wa_ref, wg_ref, wq_ref, wkv_ref, wo_ref, qn_o_ref, kvn_o_ref):
    _copy_rows((qn_ref, kvn_ref), (qn_o_ref, kvn_o_ref))
    lat = Q_LORA + KV_LORA
    cols = wint_ref.shape[1]
    wa_ref[:, :lat] = wint_ref[:lat, :].T.astype(BF16)
    zeros = jnp.zeros((ROPE_HALF, cols), F32)
    rope_rows = jnp.concatenate(
        [wint_ref[lat:lat + ROPE_HALF, :], zeros, wint_ref[lat + ROPE_HALF:lat + MLA_ROPE, :], zeros], axis=0)
    wa_ref[:, lat:] = rope_rows.T.astype(BF16)
    wg_ref[...] = wint_ref[lat + MLA_ROPE:, :].T.astype(BF16)
    lane = _lane_iota(wqb_ref.shape[0])
    for hp in range(MLA_HEADS // 2):
        t0, t1, t2 = (wqb_ref[:, (3 * hp + k) * LANES:(3 * hp + k + 1) * LANES] for k in range(3))
        ha, hb = 2 * hp, 2 * hp + 1
        wq_ref[:, ha * LANES:(ha + 1) * LANES] = t0.astype(BF16)
        wq_ref[:, hb * LANES:(hb + 1) * LANES] = jnp.where(
            lane < HALF_TILE, pltpu.roll(t1, HALF_TILE, 1), pltpu.roll(t2, HALF_TILE, 1)).astype(BF16)
        pair = jnp.where(lane < ROPE_HALF, t1,
                         jnp.where(lane < HALF_TILE, pltpu.roll(t2, LANES - ROPE_HALF, 1),
                                   jnp.where(lane < HALF_TILE + ROPE_HALF, pltpu.roll(t1, ROPE_HALF, 1), t2)))
        wq_ref[:, MLA_V + hp * LANES:MLA_V + (hp + 1) * LANES] = pair.astype(BF16)
    for hd in range(MLA_HEADS):
        wkv_ref[:, hd * LANES:(hd + 1) * LANES] = wkvb_ref[:, 2 * hd * LANES:(2 * hd + 1) * LANES].astype(BF16)
        wkv_ref[:, MLA_V + hd * LANES:MLA_V + (hd + 1) * LANES] = (
            wkvb_ref[:, (2 * hd + 1) * LANES:(2 * hd + 2) * LANES].astype(BF16))
    wo_ref[...] = wout_ref[...].astype(BF16)


def _prep_odd(odd_w_in, w_qb, w_kvb, odd_w_out, vecs):
    n = odd_w_in.shape[0]
    rb = PREP_ROW_BLOCKS
    spec = lambda rows, cols: pl.BlockSpec((None, rows // rb, cols), lambda o, r: (o, r, 0))
    shape = lambda rows, cols: jax.ShapeDtypeStruct((n, rows, cols), BF16)
    vec_in, vec_out, vec_shapes = _vec_specs(vecs)
    odd_w_in_t = jnp.swapaxes(odd_w_in, 1, 2)
    return pl.pallas_call(
        _prep_odd_kernel,
        grid=(n, rb),
        in_specs=[pl.BlockSpec((None, ODD_IN, D_MODEL // rb), lambda o, r: (o, 0, r)),
                  spec(Q_LORA, MLA_HEADS * MLA_QK),
                  spec(KV_LORA, 2 * MLA_V), spec(MLA_V, D_MODEL)] + vec_in,
        out_specs=[spec(D_MODEL, ODD_A), spec(D_MODEL, MLA_V), spec(Q_LORA, ODD_WQ),
                   spec(KV_LORA, 2 * MLA_V), spec(MLA_V, D_MODEL)] + vec_out,
        out_shape=[shape(D_MODEL, ODD_A), shape(D_MODEL, MLA_V), shape(Q_LORA, ODD_WQ),
                   shape(KV_LORA, 2 * MLA_V), shape(MLA_V, D_MODEL)] + vec_shapes,
        compiler_params=pltpu.CompilerParams(
            dimension_semantics=("arbitrary", "arbitrary"), vmem_limit_bytes=VMEM_LIMIT),
        name="prep_odd",
    )(odd_w_in_t, w_qb, w_kvb, odd_w_out, *vecs)


def kernel(x, positions, pre_norm, post_norm, even_w_in, even_conv_w, even_conv_b, ret_gn, even_w_out,
           odd_w_in, q_a_norm, w_qb, kv_a_norm, w_kvb, odd_w_out):
    cos_t, sin_t = _rope_tables(positions)
    tables = _retention_tables()
    ew_in, ew_out, (pre_norm, post_norm, ret_gn, even_conv_b) = _prep_even(
        even_w_in, even_w_out, (pre_norm, post_norm, ret_gn, even_conv_b))
    wa, wg, wq, wkv, wo, q_a_norm, kv_a_norm = _prep_odd(odd_w_in, w_qb, w_kvb, odd_w_out, (q_a_norm, kv_a_norm))
    h = x.reshape(ROWS, D_MODEL)
    for layer in range(DEPTH):
        if layer % 2 == 0:
            q, kt, vg, cmix = _even_in(h, pre_norm, ew_in, cos_t, sin_t, even_conv_w, even_conv_b, layer)
            h = _even_mix(q, kt, vg, cmix, h, ew_out, ret_gn, post_norm, tables, layer)
        else:
            q_cat, k_cat, vt, sg = _odd_in(h, pre_norm, wa, wg, q_a_norm, wq, kv_a_norm, wkv, cos_t, sin_t, layer)
            og = _attn(q_cat, k_cat, vt, sg)
            h = _odd_out(og.reshape(ROWS, MLA_V), h, wo, post_norm, layer)
    return h.reshape(BATCH, SEQ, D_MODEL)
```

```python
import math

import jax
import jax.numpy as jnp
import numpy as np
from jax import lax
from jax.experimental import pallas as pl
from jax.experimental.pallas import tpu as pltpu

D_MODEL = 1024
BATCH = 4
SEQ = 4096
DEPTH = 4
ROWS = BATCH * SEQ

RET_HEADS = 8
RET_DK = 64
RET_DV = 128
RET_CHUNK = 128
RET_PAIRS = RET_HEADS // 2
CONV_W = 1024
CONV_K = 3
MLA_HEADS = 8
MLA_NOPE = 128
MLA_ROPE = 64
MLA_DV = 128
Q_LORA = 384
KV_LORA = 256
ROPE_BASE = 10000.0
EPS = 1e-6

RET_QK = RET_HEADS * RET_DK
RET_V = RET_HEADS * RET_DV
EVEN_IN = 2 * RET_QK + 2 * RET_V + 4 * CONV_W
EVEN_MIX = RET_V + CONV_W
MLA_QK = MLA_NOPE + MLA_ROPE
MLA_V = MLA_HEADS * MLA_DV
MLA_SCALE = MLA_QK ** -0.5
LOG2E = math.log2(math.e)
MLA_QK_PAD = 256
ODD_IN = Q_LORA + KV_LORA + MLA_ROPE + MLA_V
ODD_A = Q_LORA + KV_LORA + 128
ODD_WQ = MLA_V + (MLA_HEADS // 2) * 128

_K0, _V0, _CB0 = 512, 1024, 3072

LANES = 128
SUBLANES = 8
ROPE_HALF = 32
HALF_TILE = LANES // 2

VMEM_LIMIT = 56 * 1024 * 1024

BF16 = jnp.bfloat16
F32 = jnp.float32


def _silu(x):
    return (0.5 * x) * (1.0 + jnp.tanh(0.5 * x))


def _rms(x, g):
    ms = jnp.mean(x * x, axis=-1, keepdims=True)
    return x * lax.rsqrt(ms + EPS) * g


def _rope_tile(x, cos, sin_signed):
    return x * cos + pltpu.roll(x, HALF_TILE, 1) * sin_signed


ROPE_GROUPS = LANES // ROPE_HALF
ROPE_BLK = 1024


def _rope_table_kernel(pos_ref, inv_ref, cos_ref, sin_ref):
    ang = pos_ref[...].astype(F32) * inv_ref[...]
    c = jnp.cos(ang)
    s = jnp.sin(ang)
    lane = lax.broadcasted_iota(jnp.int32, (ROPE_BLK, LANES), 1)
    low = lane < ROPE_HALF
    for g in range(ROPE_GROUPS):
        shift = (LANES - g * ROPE_HALF) % LANES
        cg = jnp.where(low, c if shift == 0 else pltpu.roll(c, shift, 1), 0.0)
        sg = jnp.where(low, s if shift == 0 else pltpu.roll(s, shift, 1), 0.0)
        cg = cg + pltpu.roll(cg, ROPE_HALF, 1)
        sg = sg + pltpu.roll(sg, ROPE_HALF, 1)
        cos_ref[g] = cg + pltpu.roll(cg, HALF_TILE, 1)
        sin_ref[g] = pltpu.roll(sg, HALF_TILE, 1) - sg


def _rope_tables(positions):
    inv = ROPE_BASE ** (-jnp.arange(0, 2 * ROPE_HALF, 2, dtype=F32) / (2 * ROPE_HALF))
    dense_rows = ROWS // ROPE_GROUPS
    pos_dense = jnp.repeat(positions.reshape(ROPE_GROUPS, dense_rows).T, ROPE_HALF, axis=1)
    inv_dense = jnp.tile(inv, ROPE_GROUPS).reshape(1, LANES)
    out_spec = pl.BlockSpec((ROPE_GROUPS, ROPE_BLK, LANES), lambda i: (0, i, 0))
    out = jax.ShapeDtypeStruct((ROPE_GROUPS, dense_rows, LANES), F32)
    cos_t, sin_t = pl.pallas_call(
        _rope_table_kernel,
        grid=(dense_rows // ROPE_BLK,),
        in_specs=[pl.BlockSpec((ROPE_BLK, LANES), lambda i: (i, 0)),
                  pl.BlockSpec((1, LANES), lambda i: (0, 0))],
        out_specs=[out_spec, out_spec],
        out_shape=[out, out],
        name="rope_table",
    )(pos_dense, inv_dense)
    return cos_t.reshape(ROWS, LANES), sin_t.reshape(ROWS, LANES)


EVEN_IN_TM = 2048
EVEN_IN_TN = 1024
EVEN_QKVG = 3 * EVEN_IN_TN
CONV_BLK = EVEN_IN_TN // 4
CONV_BLOCKS = CONV_W // CONV_BLK
EVEN_IN_STRIP = 256
CONV_STRIP = 128


def _even_in_kernel(x_ref, g_ref, w_ref, cos_ref, sin_ref, cw_ref, cb_ref,
                    q_ref, kt_ref, vg_ref, cmix_ref, u_ref, tail_ref):
    i = pl.program_id(0)
    j = pl.program_id(1)
    strips = [slice(r * EVEN_IN_STRIP, (r + 1) * EVEN_IN_STRIP) for r in range(EVEN_IN_TM // EVEN_IN_STRIP)]

    halves = [slice(0, EVEN_IN_TN // 2), slice(EVEN_IN_TN // 2, EVEN_IN_TN)]

    @pl.when(j == 0)
    def _():
        for rs in strips:
            u = _rms(x_ref[rs, :], g_ref[...]).astype(BF16)
            u_ref[rs, :] = u
            cos = cos_ref[rs, :]
            sin = sin_ref[rs, :]
            for ns in halves:
                acc = jnp.dot(u, w_ref[:, ns], preferred_element_type=F32)
                for c in range((ns.stop - ns.start) // LANES):
                    cs = slice(c * LANES, (c + 1) * LANES)
                    y = _rope_tile(acc[:, cs], cos, sin)
                    if ns.start >= _K0:
                        kt_ref[cs, rs] = y.T.astype(BF16)
                    else:
                        q_ref[rs, cs] = y.astype(BF16)

    @pl.when(j == 1)
    def _():
        vg_ref[...] = jnp.dot(u_ref[...], w_ref[...], preferred_element_type=F32).astype(BF16)

    @pl.when(j == 2)
    def _():
        for rs in strips:
            for ns in halves:
                acc = jnp.dot(u_ref[rs, :], w_ref[:, ns], preferred_element_type=F32)
                vg_ref[rs, ns] = _silu(acc).astype(BF16)

    @pl.when(j >= 3)
    def _():
        slot = j - 3
        sequence_start = (i % (SEQ // EVEN_IN_TM)) == 0
        lane_tiles = [slice(ct * LANES, (ct + 1) * LANES) for ct in range(CONV_BLK // LANES)]
        halos = [jnp.where(sequence_start, 0.0, tail_ref[slot, :, cs]) for cs in lane_tiles]
        for rs in strips:
            for ct, cs in enumerate(lane_tiles):
                acc = jnp.dot(u_ref[rs, :], w_ref[:, ct * 4 * LANES:(ct + 1) * 4 * LANES],
                              preferred_element_type=F32)
                for r in range(EVEN_IN_STRIP // CONV_STRIP):
                    rr = slice(r * CONV_STRIP, (r + 1) * CONV_STRIP)
                    part = lambda k: acc[rr, k * LANES:(k + 1) * LANES]
                    prod = part(1) * part(2)
                    ext = jnp.concatenate([halos[ct], prod], axis=0)
                    conv = cb_ref[:, cs] + prod * cw_ref[CONV_K - 1:CONV_K, cs]
                    for tap in range(CONV_K - 1):
                        off = SUBLANES - (CONV_K - 1) + tap
                        conv = conv + ext[off:off + CONV_STRIP, :] * cw_ref[tap:tap + 1, cs]
                    halos[ct] = prod[CONV_STRIP - SUBLANES:, :]
                    out_rows = slice(rs.start + rr.start, rs.start + rr.stop)
                    cmix_ref[out_rows, cs] = (part(0) * conv * _silu(part(3))).astype(BF16)
        for ct, cs in enumerate(lane_tiles):
            tail_ref[slot, :, cs] = halos[ct]


def _even_in(h, pre_norm, w, cos_t, sin_t, conv_w, conv_b, layer):
    tm, tn = EVEN_IN_TM, EVEN_IN_TN
    e = layer // 2
    qkvg_blocks = EVEN_QKVG // tn
    vg_blocks = qkvg_blocks - 1
    conv_map = lambda i, j: (e, 0, jnp.maximum(j - qkvg_blocks, 0))
    return pl.pallas_call(
        _even_in_kernel,
        grid=(ROWS // tm, EVEN_IN // tn),
        in_specs=[
            pl.BlockSpec((tm, D_MODEL), lambda i, j: (i, 0)),
            pl.BlockSpec((None, 1, D_MODEL), lambda i, j: (layer, 0, 0)),
            pl.BlockSpec((None, D_MODEL, tn), lambda i, j: (e, 0, j)),
            pl.BlockSpec((tm, LANES), lambda i, j: (i, 0)),
            pl.BlockSpec((tm, LANES), lambda i, j: (i, 0)),
            pl.BlockSpec((None, CONV_K, CONV_BLK), conv_map),
            pl.BlockSpec((None, 1, CONV_BLK), conv_map),
        ],
        out_specs=[
            pl.BlockSpec((tm, RET_QK), lambda i, j: (i, 0)),
            pl.BlockSpec((RET_QK, tm), lambda i, j: (0, i)),
            pl.BlockSpec((tm, tn), lambda i, j: (i, jnp.clip(j - 1, 0, vg_blocks - 1))),
            pl.BlockSpec((tm, CONV_BLK), lambda i, j: (i, jnp.maximum(j - qkvg_blocks, 0))),
        ],
        out_shape=[
            jax.ShapeDtypeStruct((ROWS, RET_QK), BF16),
            jax.ShapeDtypeStruct((RET_QK, ROWS), BF16),
            jax.ShapeDtypeStruct((ROWS, vg_blocks * tn), BF16),
            jax.ShapeDtypeStruct((ROWS, CONV_W), BF16),
        ],
        scratch_shapes=[
            pltpu.VMEM((tm, D_MODEL), BF16),
            pltpu.VMEM((CONV_BLOCKS, SUBLANES, CONV_BLK), F32),
        ],
        compiler_params=pltpu.CompilerParams(
            dimension_semantics=("arbitrary", "arbitrary"), vmem_limit_bytes=VMEM_LIMIT),
        name="even_in",
    )(h, pre_norm, w, cos_t, sin_t, conv_w, conv_b)


EVEN_MIX_R = 512
EVEN_MIX_STEPS = ROWS // EVEN_MIX_R
PROJ_PIECE = 256


def _even_mix_block(q_ref, kt_ref, vg_ref, cmix_ref, h_ref, wout_ref, gn_ref, pn_ref,
                    intra_ref, xi_ref, zeta_ref, gam_ref, bm_ref,
                    o_ref, state_ref, mix_prev_ref, mix_ref):
    C = RET_CHUNK
    chunks = EVEN_MIX_R // C

    row_k = lax.broadcasted_iota(jnp.int32, (LANES, C), 0)
    first_head = (row_k % HALF_TILE) < ROPE_HALF
    lane_v = lax.broadcasted_iota(jnp.int32, (C, 2 * RET_DV), 1)
    zero_k = jnp.zeros((LANES, C), BF16)
    zero_v = jnp.zeros((C, 2 * RET_DV), BF16)

    y_prev = []

    def project_piece():
        ns = slice(len(y_prev) * PROJ_PIECE, (len(y_prev) + 1) * PROJ_PIECE)
        y_prev.append(jnp.dot(mix_prev_ref[...], wout_ref[:RET_V, ns], preferred_element_type=F32)
                      + jnp.dot(cmix_ref[...], wout_ref[RET_V:, ns], preferred_element_type=F32))

    stages_per_piece = chunks * RET_PAIRS * PROJ_PIECE // D_MODEL
    for c, j in [(c, p) for c in range(chunks) for p in range(RET_PAIRS)]:
        rows = slice(c * C, (c + 1) * C)
        if (c * RET_PAIRS + j) % stages_per_piece == 0:
            project_piece()
        qp = q_ref[rows, j * LANES:(j + 1) * LANES]
        kt = kt_ref[j * LANES:(j + 1) * LANES, rows]
        vp = vg_ref[rows, j * 2 * RET_DV:(j + 1) * 2 * RET_DV]
        kt_heads = jnp.concatenate(
            [jnp.where(first_head, kt, zero_k), jnp.where(first_head, zero_k, kt)], axis=1)
        sc = jnp.dot(qp, kt_heads, preferred_element_type=F32)
        pm = (sc * intra_ref[j]).astype(BF16)
        vblk = jnp.concatenate(
            [jnp.where(lane_v < RET_DV, vp, zero_v), jnp.where(lane_v >= RET_DV, vp, zero_v)], axis=0)
        st = state_ref[j]
        r = (jnp.dot(pm, vblk, preferred_element_type=F32)
             + jnp.dot(qp, st.astype(BF16), preferred_element_type=F32) * xi_ref[j])
        kz = (kt.astype(F32) * zeta_ref[j]).astype(BF16)
        kv = jnp.dot(kz, vp, preferred_element_type=F32)
        state_ref[j] = st * gam_ref[j] + kv * bm_ref[...]
        for hh in range(2):
            hd = 2 * j + hh
            cols = slice(hd * RET_DV, (hd + 1) * RET_DV)
            y = _rms(r[:, hh * RET_DV:(hh + 1) * RET_DV], gn_ref[:, cols])
            gate = vg_ref[rows, RET_V + hd * RET_DV:RET_V + (hd + 1) * RET_DV].astype(F32)
            mix_ref[rows, cols] = (y * gate).astype(BF16)

    o_ref[...] = h_ref[...] + _rms(jnp.concatenate(y_prev, axis=1), pn_ref[...])


def _even_mix_kernel(q_ref, kt_ref, vg_ref, cmix_ref, h_ref, wout_ref, gn_ref, pn_ref,
                     intra_ref, xi_ref, zeta_ref, gam_ref, bm_ref,
                     o_ref, state_ref, mix_a_ref, mix_b_ref):
    t = pl.program_id(0)
    args = (q_ref, kt_ref, vg_ref, cmix_ref, h_ref, wout_ref, gn_ref, pn_ref,
            intra_ref, xi_ref, zeta_ref, gam_ref, bm_ref, o_ref, state_ref)

    @pl.when(t % (SEQ // EVEN_MIX_R) == 0)
    def _():
        state_ref[...] = jnp.zeros_like(state_ref)

    @pl.when(t == 0)
    def _():
        mix_b_ref[...] = jnp.zeros_like(mix_b_ref)

    last = EVEN_MIX_STEPS
    assert last % 2 == 0

    @pl.when((t % 2 == 0) & (t < last))
    def _():
        _even_mix_block(*args, mix_b_ref, mix_a_ref)

    @pl.when(t % 2 == 1)
    def _():
        _even_mix_block(*args, mix_a_ref, mix_b_ref)

    @pl.when(t == last)
    def _():
        y = (jnp.dot(mix_b_ref[...], wout_ref[:RET_V, :], preferred_element_type=F32)
             + jnp.dot(cmix_ref[...], wout_ref[RET_V:, :], preferred_element_type=F32))
        o_ref[...] = h_ref[...] + _rms(y, pn_ref[...])


def _retention_tables():
    h, c = RET_HEADS, RET_CHUNK
    log_gamma = np.log1p(-np.exp2(-5.0 - np.arange(h, dtype=np.float64)))
    i = np.arange(c, dtype=np.float64)
    rel = i[:, None] - i[None, :]
    intra = np.where(rel >= 0, np.exp(log_gamma[:, None, None] * np.maximum(rel, 0.0)), 0.0)
    xi = np.exp(log_gamma[:, None] * (i + 1.0))
    zeta = np.exp(log_gamma[:, None] * (c - 1.0 - i))
    gamma_c = np.exp(log_gamma * c)
    pair = np.arange(RET_PAIRS)
    intra_p = intra.reshape(RET_PAIRS, 2, c, c).transpose(0, 2, 1, 3).reshape(RET_PAIRS, c, 2 * c)
    xi_p = np.repeat(xi.reshape(RET_PAIRS, 2, c).transpose(0, 2, 1), RET_DV, axis=2)
    second = ((np.arange(2 * RET_DK) % HALF_TILE) >= ROPE_HALF).astype(np.int64)
    zeta_p = zeta.reshape(RET_PAIRS, 2, c)[pair[:, None, None], second[None, :, None], np.arange(c)[None, None, :]]
    gam_rows = gamma_c.reshape(RET_PAIRS, 2)[:, second]
    gam_p = np.broadcast_to(gam_rows[:, :, None], (RET_PAIRS, 2 * RET_DK, 2 * RET_DV))
    col_head = np.arange(2 * RET_DV) // RET_DV
    bm = (second[:, None] == col_head[None, :])
    as_f32 = lambda a: jnp.asarray(np.ascontiguousarray(a, dtype=np.float32))
    return as_f32(intra_p), as_f32(xi_p), as_f32(zeta_p), as_f32(gam_p), as_f32(bm)


def _even_mix(q, kt, vg, cmix, h, w_out, gn, post_norm, tables, layer):
    R = EVEN_MIX_R
    nt = EVEN_MIX_STEPS
    e = layer // 2
    intra_p, xi_p, zeta_p, gam_p, bm = tables
    mix_map = lambda t: (jnp.minimum(t, nt - 1), 0)
    row_map = lambda t: (jnp.maximum(t - 1, 0), 0)
    const2 = lambda t: (0, 0)
    const3 = lambda t: (0, 0, 0)
    return pl.pallas_call(
        _even_mix_kernel,
        grid=(nt + 1,),
        in_specs=[
            pl.BlockSpec((R, RET_QK), mix_map),
            pl.BlockSpec((RET_QK, R), lambda t: (0, jnp.minimum(t, nt - 1))),
            pl.BlockSpec((R, 2 * RET_V), mix_map),
            pl.BlockSpec((R, CONV_W), row_map),
            pl.BlockSpec((R, D_MODEL), row_map),
            pl.BlockSpec((None, EVEN_MIX, D_MODEL), lambda t: (e, 0, 0)),
            pl.BlockSpec((None, 1, RET_V), lambda t: (e, 0, 0)),
            pl.BlockSpec((None, 1, D_MODEL), lambda t: (layer, 0, 0)),
            pl.BlockSpec((RET_PAIRS, RET_CHUNK, 2 * RET_CHUNK), const3),
            pl.BlockSpec((RET_PAIRS, RET_CHUNK, 2 * RET_DV), const3),
            pl.BlockSpec((RET_PAIRS, RET_CHUNK, 2 * RET_DK), const3),
            pl.BlockSpec((RET_PAIRS, 2 * RET_DK, 2 * RET_DV), const3),
            pl.BlockSpec((2 * RET_DK, 2 * RET_DV), const2),
        ],
        out_specs=pl.BlockSpec((R, D_MODEL), row_map),
        out_shape=jax.ShapeDtypeStruct((ROWS, D_MODEL), F32),
        scratch_shapes=[
            pltpu.VMEM((RET_PAIRS, 2 * RET_DK, 2 * RET_DV), F32),
            pltpu.VMEM((R, RET_V), BF16),
            pltpu.VMEM((R, RET_V), BF16),
        ],
        compiler_params=pltpu.CompilerParams(
            dimension_semantics=("arbitrary",), vmem_limit_bytes=VMEM_LIMIT),
        name="even_mix",
    )(q, kt, vg, cmix, h, w_out, gn, post_norm, intra_p, xi_p, zeta_p, gam_p, bm)


ODD_R = 1024
ATTN_T = 512


def _odd_in_kernel(x_ref, g_ref, wa_ref, wg_ref, qn_ref, wq_ref, kvn_ref, wkv_ref, cos_ref, sin_ref,
                   q_ref, k_ref, vt_ref, sg_ref):
    u = _rms(x_ref[...], g_ref[...]).astype(BF16)
    a = jnp.dot(u, wa_ref[...], preferred_element_type=F32)
    cq = a[:, :Q_LORA]
    ckv = a[:, Q_LORA:Q_LORA + KV_LORA]
    kr = a[:, Q_LORA + KV_LORA:]
    q = jnp.dot(_rms(cq, qn_ref[...]).astype(BF16), wq_ref[...], preferred_element_type=F32) * (MLA_SCALE * LOG2E)
    kv = jnp.dot(_rms(ckv, kvn_ref[...]).astype(BF16), wkv_ref[...], preferred_element_type=F32)
    cos = cos_ref[...]
    sin = sin_ref[...]
    kr_rot = _rope_tile(kr, cos, sin)
    kr_heads = (kr_rot.astype(BF16), pltpu.roll(kr_rot, ROPE_HALF, 1).astype(BF16))
    for hd in range(MLA_HEADS):
        lo, hi = hd * LANES, (hd + 1) * LANES
        q_ref[0, hd, 0:LANES, :] = q[:, lo:hi].T.astype(BF16)
        if hd % 2 == 0:
            pair = MLA_V + (hd // 2) * LANES
            q_rot_t = _rope_tile(q[:, pair:pair + LANES], cos, sin).T.astype(BF16)
        q_ref[0, hd, LANES:2 * LANES, :] = q_rot_t
        k_ref[0, hd, :, 0:LANES] = kv[:, lo:hi].astype(BF16)
        k_ref[0, hd, :, LANES:2 * LANES] = kr_heads[hd % 2]
        for s in range(ODD_R // ATTN_T):
            vt_ref[0, hd, s] = kv[s * ATTN_T:(s + 1) * ATTN_T, MLA_V + lo:MLA_V + hi].T.astype(BF16)
    gate = jnp.dot(u, wg_ref[...], preferred_element_type=F32)
    sg_ref[...] = _silu(gate).astype(BF16)


def _odd_in(h, pre_norm, wa, wg, qn, wq, kvn, wkv, cos_t, sin_t, layer):
    R = ODD_R
    nb = SEQ // R
    o = layer // 2
    row_map = lambda b, i: (b * nb + i, 0)
    odd_map = lambda b, i: (o, 0, 0)
    return pl.pallas_call(
        _odd_in_kernel,
        grid=(BATCH, nb),
        in_specs=[
            pl.BlockSpec((R, D_MODEL), row_map),
            pl.BlockSpec((None, 1, D_MODEL), lambda b, i: (layer, 0, 0)),
            pl.BlockSpec((None, D_MODEL, ODD_A), odd_map),
            pl.BlockSpec((None, D_MODEL, MLA_V), odd_map),
            pl.BlockSpec((None, 1, Q_LORA), odd_map),
            pl.BlockSpec((None, Q_LORA, ODD_WQ), odd_map),
            pl.BlockSpec((None, 1, KV_LORA), odd_map),
            pl.BlockSpec((None, KV_LORA, 2 * MLA_V), odd_map),
            pl.BlockSpec((R, LANES), row_map),
            pl.BlockSpec((R, LANES), row_map),
        ],
        out_specs=[
            pl.BlockSpec((1, MLA_HEADS, MLA_QK_PAD, R), lambda b, i: (b, 0, 0, i)),
            pl.BlockSpec((1, MLA_HEADS, R, MLA_QK_PAD), lambda b, i: (b, 0, i, 0)),
            pl.BlockSpec((1, MLA_HEADS, R // ATTN_T, MLA_DV, ATTN_T), lambda b, i: (b, 0, i, 0, 0)),
            pl.BlockSpec((R, MLA_V), row_map),
        ],
        out_shape=[
            jax.ShapeDtypeStruct((BATCH, MLA_HEADS, MLA_QK_PAD, SEQ), BF16),
            jax.ShapeDtypeStruct((BATCH, MLA_HEADS, SEQ, MLA_QK_PAD), BF16),
            jax.ShapeDtypeStruct((BATCH, MLA_HEADS, SEQ // ATTN_T, MLA_DV, ATTN_T), BF16),
            jax.ShapeDtypeStruct((ROWS, MLA_V), BF16),
        ],
        compiler_params=pltpu.CompilerParams(
            dimension_semantics=("arbitrary", "arbitrary"), vmem_limit_bytes=VMEM_LIMIT),
        name="odd_in",
    )(h, pre_norm, wa, wg, qn, wq, kvn, wkv, cos_t, sin_t)


ATTN_HG = 8


def _attn_kernel(q_ref, qn_ref, k_ref, vt_ref, sg_ref, o_ref, m_ref, l_ref, acc_ref, st_ref):
    T = ATTN_T
    qi = pl.program_id(2)
    m_ref[...] = jnp.full(m_ref.shape, -jnp.inf, F32)
    l_ref[...] = jnp.zeros(l_ref.shape, F32)
    acc_ref[...] = jnp.zeros(acc_ref.shape, F32)

    def scores(hh, j, queries=q_ref):
        k = k_ref[0, hh, pl.ds(pl.multiple_of(j * T, T), T), :]
        return jnp.dot(k, queries[0, hh], preferred_element_type=F32)

    def consume(hh, j, st):
        m_old = m_ref[hh]
        m_new = jnp.maximum(m_old, jnp.max(st, axis=0, keepdims=True))
        alpha = jnp.exp2(m_old - m_new)
        p = jnp.exp2(st - m_new)
        l_ref[hh] = alpha * l_ref[hh] + jnp.sum(p, axis=0, keepdims=True)
        acc_ref[hh] = alpha * acc_ref[hh] + jnp.dot(
            vt_ref[0, hh, j], p.astype(BF16), preferred_element_type=F32)
        m_ref[hh] = m_new

    @pl.when(qi == 0)
    def _():
        for hh in range(ATTN_HG):
            st_ref[hh] = scores(hh, 0)

    def body(j, carry):
        for hh in range(ATTN_HG):
            st = st_ref[hh]
            st_ref[hh] = scores(hh, j + 1)
            consume(hh, j, st)
        return carry

    lax.fori_loop(0, qi, body, 0)

    H = T // 2
    causal = lax.broadcasted_iota(jnp.int32, (H, H), 0) <= lax.broadcasted_iota(jnp.int32, (H, H), 1)

    def consume_diagonal(hh, st):
        s_ee = jnp.where(causal, st[:H, :H], -jnp.inf)
        s_el = st[:H, H:]
        s_ll = jnp.where(causal, st[H:, H:], -jnp.inf)
        m_old = m_ref[hh]
        col_max = jnp.concatenate(
            [jnp.max(s_ee, axis=0, keepdims=True),
             jnp.maximum(jnp.max(s_el, axis=0, keepdims=True), jnp.max(s_ll, axis=0, keepdims=True))], axis=1)
        m_new = jnp.maximum(m_old, col_max)
        alpha = jnp.exp2(m_old - m_new)
        p_ee = jnp.exp2(s_ee - m_new[:, :H])
        p_el = jnp.exp2(s_el - m_new[:, H:])
        p_ll = jnp.exp2(s_ll - m_new[:, H:])
        col_sum = jnp.concatenate(
            [jnp.sum(p_ee, axis=0, keepdims=True),
             jnp.sum(p_el, axis=0, keepdims=True) + jnp.sum(p_ll, axis=0, keepdims=True)], axis=1)
        l_ref[hh] = alpha * l_ref[hh] + col_sum
        vt = vt_ref[0, hh, qi]
        pv = jnp.concatenate(
            [jnp.dot(vt[:, :H], p_ee.astype(BF16), preferred_element_type=F32),
             jnp.dot(vt, jnp.concatenate([p_el, p_ll], axis=0).astype(BF16), preferred_element_type=F32)], axis=1)
        acc_ref[hh] = alpha * acc_ref[hh] + pv
        m_ref[hh] = m_new

    for hh in range(ATTN_HG):
        st = st_ref[hh]
        st_ref[hh] = scores(hh, 0, qn_ref)
        consume_diagonal(hh, st)
    for hh in range(ATTN_HG):
        cols = slice(hh * MLA_DV, (hh + 1) * MLA_DV)
        o = (acc_ref[hh] * (1.0 / l_ref[hh])).T
        o_ref[0, :, cols] = (o * sg_ref[0, :, cols].astype(F32)).astype(BF16)


def _attn(q_cat, k_cat, vt, sg):
    T = ATTN_T
    HG = ATTN_HG
    nq = SEQ // T
    return pl.pallas_call(
        _attn_kernel,
        grid=(BATCH, MLA_HEADS // HG, nq),
        in_specs=[
            pl.BlockSpec((1, HG, MLA_QK_PAD, T), lambda b, h, i: (b, h, 0, i)),
            pl.BlockSpec((1, HG, MLA_QK_PAD, T), lambda b, h, i: (b, h, 0, jnp.minimum(i + 1, nq - 1))),
            pl.BlockSpec((1, HG, SEQ, MLA_QK_PAD), lambda b, h, i: (b, h, 0, 0), pipeline_mode=pl.Buffered(1)),
            pl.BlockSpec((1, HG, nq, MLA_DV, T), lambda b, h, i: (b, h, 0, 0, 0)),
            pl.BlockSpec((1, T, HG * MLA_DV), lambda b, h, i: (b, i, h)),
        ],
        out_specs=pl.BlockSpec((1, T, HG * MLA_DV), lambda b, h, i: (b, i, h)),
        out_shape=jax.ShapeDtypeStruct((BATCH, SEQ, MLA_V), BF16),
        scratch_shapes=[
            pltpu.VMEM((HG, 1, T), F32),
            pltpu.VMEM((HG, 1, T), F32),
            pltpu.VMEM((HG, MLA_DV, T), F32),
            pltpu.VMEM((HG, T, T), F32),
        ],
        compiler_params=pltpu.CompilerParams(
            dimension_semantics=("arbitrary", "arbitrary", "arbitrary"), vmem_limit_bytes=VMEM_LIMIT),
        name="odd_attn",
    )(q_cat, q_cat, k_cat, vt, sg.reshape(BATCH, SEQ, MLA_V))


ODD_OUT_R = 2048


def _odd_out_kernel(og_ref, h_ref, w_ref, pn_ref, out_ref):
    y = jnp.dot(og_ref[...], w_ref[...], preferred_element_type=F32)
    out_ref[...] = h_ref[...] + _rms(y, pn_ref[...])


def _odd_out(og, h, w, post_norm, layer):
    R = ODD_OUT_R
    row_map = lambda i: (i, 0)
    return pl.pallas_call(
        _odd_out_kernel,
        grid=(ROWS // R,),
        in_specs=[
            pl.BlockSpec((R, MLA_V), row_map),
            pl.BlockSpec((R, D_MODEL), row_map),
            pl.BlockSpec((None, MLA_V, D_MODEL), lambda i: (layer // 2, 0, 0)),
            pl.BlockSpec((None, 1, D_MODEL), lambda i: (layer, 0, 0)),
        ],
        out_specs=pl.BlockSpec((R, D_MODEL), row_map),
        out_shape=jax.ShapeDtypeStruct((ROWS, D_MODEL), F32),
        compiler_params=pltpu.CompilerParams(
            dimension_semantics=("arbitrary",), vmem_limit_bytes=VMEM_LIMIT),
        name="odd_out",
    )(og, h, w, post_norm)


PREP_ROW_BLOCKS = 4


def _lane_iota(rows):
    return lax.broadcasted_iota(jnp.int32, (rows, LANES), 1)


def _copy_rows(vec_refs, out_refs):
    @pl.when((pl.program_id(0) == 0) & (pl.program_id(1) == 0))
    def _():
        for src, dst in zip(vec_refs, out_refs):
            for layer in range(src.shape[0]):
                dst[layer] = src[layer:layer + 1, :]


def _prep_even_kernel(win_ref, wout_ref, *rest):
    vec_refs, (win_o_ref, wout_o_ref), vec_out_refs = rest[:4], rest[4:6], rest[6:]
    _copy_rows(vec_refs, vec_out_refs)
    rows = win_ref.shape[0]
    lane = _lane_iota(rows)
    keep = (lane < ROPE_HALF) | (lane >= LANES - ROPE_HALF)
    for t in range(_V0 // LANES):
        cs = slice(t * LANES, (t + 1) * LANES)
        x = win_ref[:, cs]
        moved = jnp.where(lane < HALF_TILE, pltpu.roll(x, LANES - ROPE_HALF, 1), pltpu.roll(x, ROPE_HALF, 1))
        y = jnp.where(keep, x, moved)
        if t * LANES >= _K0:
            y = y * (RET_DK ** -0.5)
        win_o_ref[:, cs] = y.astype(BF16)
    win_o_ref[:, _V0:_CB0] = win_ref[:, _V0:_CB0].astype(BF16)
    for t in range(CONV_W // LANES):
        for k in range(4):
            src = _CB0 + k * CONV_W + t * LANES
            dst = _CB0 + (4 * t + k) * LANES
            win_o_ref[:, dst:dst + LANES] = win_ref[:, src:src + LANES].astype(BF16)
    wout_o_ref[...] = wout_ref[...].astype(BF16)


def _vec_specs(vecs):
    ins = [pl.BlockSpec(v.shape, lambda *_: (0, 0)) for v in vecs]
    outs = [pl.BlockSpec((v.shape[0], 1, v.shape[1]), lambda *_: (0, 0, 0)) for v in vecs]
    shapes = [jax.ShapeDtypeStruct((v.shape[0], 1, v.shape[1]), v.dtype) for v in vecs]
    return ins, outs, shapes


def _prep_even(even_w_in, even_w_out, vecs):
    n = even_w_in.shape[0]
    rb = PREP_ROW_BLOCKS
    spec = lambda rows, cols: pl.BlockSpec((None, rows // rb, cols), lambda e, r: (e, r, 0))
    vec_in, vec_out, vec_shapes = _vec_specs(vecs)
    outs = pl.pallas_call(
        _prep_even_kernel,
        grid=(n, rb),
        in_specs=[spec(D_MODEL, EVEN_IN), spec(EVEN_MIX, D_MODEL)] + vec_in,
        out_specs=[spec(D_MODEL, EVEN_IN), spec(EVEN_MIX, D_MODEL)] + vec_out,
        out_shape=[jax.ShapeDtypeStruct((n, D_MODEL, EVEN_IN), BF16),
                   jax.ShapeDtypeStruct((n, EVEN_MIX, D_MODEL), BF16)] + vec_shapes,
        compiler_params=pltpu.CompilerParams(
            dimension_semantics=("arbitrary", "arbitrary"), vmem_limit_bytes=VMEM_LIMIT),
        name="prep_even",
    )(even_w_in, even_w_out, *vecs)
    return outs[0], outs[1], outs[2:]


def _prep_odd_kernel(wint_ref, wqb_ref, wkvb_ref, wout_ref, qn_ref, kvn_ref,
                     wa_ref, wg_ref, wq_ref, wkv_ref, wo_ref, qn_o_ref, kvn_o_ref):
    _copy_rows((qn_ref, kvn_ref), (qn_o_ref, kvn_o_ref))
    lat = Q_LORA + KV_LORA
    cols = wint_ref.shape[1]
    wa_ref[:, :lat] = wint_ref[:lat, :].T.astype(BF16)
    zeros = jnp.zeros((ROPE_HALF, cols), F32)
    rope_rows = jnp.concatenate(
        [wint_ref[lat:lat + ROPE_HALF, :], zeros, wint_ref[lat + ROPE_HALF:lat + MLA_ROPE, :], zeros], axis=0)
    wa_ref[:, lat:] = rope_rows.T.astype(BF16)
    wg_ref[...] = wint_ref[lat + MLA_ROPE:, :].T.astype(BF16)
    lane = _lane_iota(wqb_ref.shape[0])
    for hp in range(MLA_HEADS // 2):
        t0, t1, t2 = (wqb_ref[:, (3 * hp + k) * LANES:(3 * hp + k + 1) * LANES] for k in range(3))
        ha, hb = 2 * hp, 2 * hp + 1
        wq_ref[:, ha * LANES:(ha + 1) * LANES] = t0.astype(BF16)
        wq_ref[:, hb * LANES:(hb + 1) * LANES] = jnp.where(
            lane < HALF_TILE, pltpu.roll(t1, HALF_TILE, 1), pltpu.roll(t2, HALF_TILE, 1)).astype(BF16)
        pair = jnp.where(lane < ROPE_HALF, t1,
                         jnp.where(lane < HALF_TILE, pltpu.roll(t2, LANES - ROPE_HALF, 1),
                                   jnp.where(lane < HALF_TILE + ROPE_HALF, pltpu.roll(t1, ROPE_HALF, 1), t2)))
        wq_ref[:, MLA_V + hp * LANES:MLA_V + (hp + 1) * LANES] = pair.astype(BF16)
    for hd in range(MLA_HEADS):
        wkv_ref[:, hd * LANES:(hd + 1) * LANES] = wkvb_ref[:, 2 * hd * LANES:(2 * hd + 1) * LANES].astype(BF16)
        wkv_ref[:, MLA_V + hd * LANES:MLA_V + (hd + 1) * LANES] = (
            wkvb_ref[:, (2 * hd + 1) * LANES:(2 * hd + 2) * LANES].astype(BF16))
    wo_ref[...] = wout_ref[...].astype(BF16)


def _prep_odd(odd_w_in, w_qb, w_kvb, odd_w_out, vecs):
    n = odd_w_in.shape[0]
    rb = PREP_ROW_BLOCKS
    spec = lambda rows, cols: pl.BlockSpec((None, rows // rb, cols), lambda o, r: (o, r, 0))
    shape = lambda rows, cols: jax.ShapeDtypeStruct((n, rows, cols), BF16)
    vec_in, vec_out, vec_shapes = _vec_specs(vecs)
    odd_w_in_t = jnp.swapaxes(odd_w_in, 1, 2)
    return pl.pallas_call(
        _prep_odd_kernel,
        grid=(n, rb),
        in_specs=[pl.BlockSpec((None, ODD_IN, D_MODEL // rb), lambda o, r: (o, 0, r)),
                  spec(Q_LORA, MLA_HEADS * MLA_QK),
                  spec(KV_LORA, 2 * MLA_V), spec(MLA_V, D_MODEL)] + vec_in,
        out_specs=[spec(D_MODEL, ODD_A), spec(D_MODEL, MLA_V), spec(Q_LORA, ODD_WQ),
                   spec(KV_LORA, 2 * MLA_V), spec(MLA_V, D_MODEL)] + vec_out,
        out_shape=[shape(D_MODEL, ODD_A), shape(D_MODEL, MLA_V), shape(Q_LORA, ODD_WQ),
                   shape(KV_LORA, 2 * MLA_V), shape(MLA_V, D_MODEL)] + vec_shapes,
        compiler_params=pltpu.CompilerParams(
            dimension_semantics=("arbitrary", "arbitrary"), vmem_limit_bytes=VMEM_LIMIT),
        name="prep_odd",
    )(odd_w_in_t, w_qb, w_kvb, odd_w_out, *vecs)


def kernel(x, positions, pre_norm, post_norm, even_w_in, even_conv_w, even_conv_b, ret_gn, even_w_out,
           odd_w_in, q_a_norm, w_qb, kv_a_norm, w_kvb, odd_w_out):
    cos_t, sin_t = _rope_tables(positions)
    tables = _retention_tables()
    ew_in, ew_out, (pre_norm, post_norm, ret_gn, even_conv_b) = _prep_even(
        even_w_in, even_w_out, (pre_norm, post_norm, ret_gn, even_conv_b))
    wa, wg, wq, wkv, wo, q_a_norm, kv_a_norm = _prep_odd(odd_w_in, w_qb, w_kvb, odd_w_out, (q_a_norm, kv_a_norm))
    h = x.reshape(ROWS, D_MODEL)
    for layer in range(DEPTH):
        if layer % 2 == 0:
            q, kt, vg, cmix = _even_in(h, pre_norm, ew_in, cos_t, sin_t, even_conv_w, even_conv_b, layer)
            h = _even_mix(q, kt, vg, cmix, h, ew_out, ret_gn, post_norm, tables, layer)
        else:
            q_cat, k_cat, vt, sg = _odd_in(h, pre_norm, wa, wg, q_a_norm, wq, kv_a_norm, wkv, cos_t, sin_t, layer)
            og = _attn(q_cat, k_cat, vt, sg)
            h = _odd_out(og.reshape(ROWS, MLA_V), h, wo, post_norm, layer)
    return h.reshape(BATCH, SEQ, D_MODEL)
```
